```python
import math
import jax, jax.numpy as jnp
from jax import lax
import numpy as np

D_MODEL = 1024
BATCH = 4
SEQ = 4096
DEPTH = 1

HEAD_DIM = 64
D_MIX = D_MODEL
D_ATTN = D_MIX // 2
D_SGU = D_MIX - D_ATTN
N_ATTN_HEADS = D_ATTN // HEAD_DIM
N_SGU_GROUPS = D_SGU // HEAD_DIM
SGU_GROUP_DIM = D_SGU // N_SGU_GROUPS
D_IN_PROJ = 3 * D_ATTN + 2 * D_SGU
DILATED_BRANCHES = ((128, 1), (512, 4), (2048, 16))
SGU_CHUNK = 128
D_FF = 4 * D_MODEL
N_REL_BUCKETS = 32
REL_MAX_DISTANCE = 1024
RMS_EPS = 1e-6
LN_EPS = 1e-5
NEG_INF = -1e30

kernel_name = "hymba_dilated_attn_gmlp_sandwich"


def rms_norm(x, g):
    xf = x.astype(jnp.float32)
    y = xf * lax.rsqrt(jnp.mean(xf * xf, axis=-1, keepdims=True) + RMS_EPS)
    return (y * g.astype(jnp.float32)).astype(x.dtype)


def layer_norm(x, g, b):
    xf = x.astype(jnp.float32)
    mu = jnp.mean(xf, axis=-1, keepdims=True)
    xc = xf - mu
    y = xc * lax.rsqrt(jnp.mean(xc * xc, axis=-1, keepdims=True) + LN_EPS)
    return (y * g.astype(jnp.float32) + b.astype(jnp.float32)).astype(x.dtype)


def t5_bucket(rel):
    half = N_REL_BUCKETS // 2
    max_exact = half // 2
    ret = jnp.where(rel > 0, half, 0)
    n = jnp.abs(rel)
    nf = jnp.maximum(n, 1).astype(jnp.float32)
    large = max_exact + (jnp.log(nf / max_exact) / math.log(REL_MAX_DISTANCE / max_exact)
                         * (half - max_exact)).astype(jnp.int32)
    large = jnp.minimum(large, half - 1)
    return ret + jnp.where(n < max_exact, n, large)


def dilated_window_branch(q, k, v, rel_bias, window, dil):
    B, S, H, Dh = q.shape
    half = window // (2 * dil)
    blk = half
    L = S // dil
    nb = -(-L // blk)
    Lp = nb * blk

    def to_blocks(t):
        t = t.reshape(B, L, dil, H, Dh).transpose(0, 2, 1, 3, 4)
        t = jnp.pad(t, ((0, 0), (0, 0), (0, Lp - L), (0, 0), (0, 0)))
        return t.reshape(B, dil, nb, blk, H, Dh)

    def neighbourhood(t):
        tp = jnp.pad(t, ((0, 0), (0, 0), (1, 1), (0, 0), (0, 0), (0, 0)))
        return jnp.concatenate([tp[:, :, :-2], tp[:, :, 1:-1], tp[:, :, 2:]], axis=3)

    qb = to_blocks(q)
    kw = neighbourhood(to_blocks(k))
    vw = neighbourhood(to_blocks(v))

    q_local = jnp.arange(blk)
    k_local = jnp.arange(3 * blk) - blk
    rel = k_local[None, :] - q_local[:, None]
    bias = rel_bias[t5_bucket(rel * dil)].astype(jnp.float32)
    k_idx = jnp.arange(nb)[:, None] * blk + k_local[None, :]
    valid = (jnp.abs(rel) <= half)[None] & ((k_idx >= 0) & (k_idx < L))[:, None, :]

    s = jnp.einsum('brnqhd,brnkhd->brnhqk', qb, kw).astype(jnp.float32)
    s = s + bias.transpose(2, 0, 1)
    s = jnp.where(valid[:, None], s, NEG_INF)
    m = jnp.max(s, axis=-1, keepdims=True)
    p = jnp.exp(s - m)
    den = jnp.sum(p, axis=-1, keepdims=True)
    o = jnp.einsum('brnhqk,brnkhd->brnqhd', (p / den).astype(v.dtype), vw)
    lse = (m + jnp.log(den))[..., 0].transpose(0, 1, 2, 4, 3)

    o = o.reshape(B, dil, Lp, H, Dh)[:, :, :L].transpose(0, 2, 1, 3, 4).reshape(B, S, H, Dh)
    lse = lse.reshape(B, dil, Lp, H)[:, :, :L].transpose(0, 2, 1, 3).reshape(B, S, H)
    return o, lse


def dilated_attention(q, k, v, rel_bias):
    outs, lses = [], []
    for window, dil in DILATED_BRANCHES:
        o, l = dilated_window_branch(q, k, v, rel_bias, window, dil)
        outs.append(o)
        lses.append(l)
    w = jax.nn.softmax(jnp.stack(lses, axis=0), axis=0)
    return jnp.einsum('ibsh,ibshd->bshd', w.astype(q.dtype), jnp.stack(outs, axis=0))


def spatial_gating(u, v, ln_g, ln_b, w_s, b_s):
    B, S, _ = u.shape
    nc = S // SGU_CHUNK
    v = layer_norm(v, ln_g, ln_b).reshape(B, nc, SGU_CHUNK, N_SGU_GROUPS, SGU_GROUP_DIM)
    mixed = jnp.einsum('gts,bnsgc->bntgc', w_s, v) + b_s.T[None, None, :, :, None]
    return u * mixed.reshape(B, S, D_SGU)


def hybrid_layer(x, g_pre_mix, w_in, sgu_ln_g, sgu_ln_b, sgu_w, sgu_b, w_out, g_post_mix,
                 g_pre_ffn, w_ff1, w_ff2, g_post_ffn, rel_bias):
    B, S, _ = x.shape
    h = rms_norm(x, g_pre_mix)
    z = h @ w_in
    q, k, v, zu, zv = jnp.split(
        z, [D_ATTN, 2 * D_ATTN, 3 * D_ATTN, 3 * D_ATTN + D_SGU], axis=-1)
    q = q.reshape(B, S, N_ATTN_HEADS, HEAD_DIM) * (HEAD_DIM ** -0.5)
    k = k.reshape(B, S, N_ATTN_HEADS, HEAD_DIM)
    v = v.reshape(B, S, N_ATTN_HEADS, HEAD_DIM)
    attn = dilated_attention(q, k, v, rel_bias).reshape(B, S, D_ATTN)
    sgu = spatial_gating(jax.nn.gelu(zu), jax.nn.gelu(zv), sgu_ln_g, sgu_ln_b, sgu_w, sgu_b)
    mix = jnp.concatenate([attn, sgu], axis=-1) @ w_out
    x = x + rms_norm(mix, g_post_mix)
    h = rms_norm(x, g_pre_ffn)
    f = jnp.square(jax.nn.relu(h @ w_ff1)) @ w_ff2
    return x + rms_norm(f, g_post_ffn)


def setup_inputs(seed: int = 0) -> dict:
    key = jax.random.key(seed)
    ks = jax.random.split(key, 16)
    f32 = jnp.float32

    def nrm(k, shape, scale):
        return jax.random.normal(k, shape, f32) * scale

    def gain(k, shape):
        return 1.0 + 0.05 * jax.random.normal(k, shape, f32)

    return {
        "x": jax.random.normal(ks[0], (BATCH, SEQ, D_MODEL), f32),
        "g_pre_mix": gain(ks[1], (DEPTH, D_MODEL)),
        "w_in": nrm(ks[2], (DEPTH, D_MODEL, D_IN_PROJ), D_MODEL ** -0.5),
        "sgu_ln_g": gain(ks[3], (DEPTH, D_SGU)),
        "sgu_ln_b": nrm(ks[4], (DEPTH, D_SGU), 0.02),
        "sgu_w": nrm(ks[5], (DEPTH, N_SGU_GROUPS, SGU_CHUNK, SGU_CHUNK), SGU_CHUNK ** -0.5),
        "sgu_b": gain(ks[6], (DEPTH, N_SGU_GROUPS, SGU_CHUNK)),
        "w_out": nrm(ks[7], (DEPTH, D_MIX, D_MODEL), D_MIX ** -0.5),
        "g_post_mix": gain(ks[8], (DEPTH, D_MODEL)),
        "g_pre_ffn": gain(ks[9], (DEPTH, D_MODEL)),
        "w_ff1": nrm(ks[10], (DEPTH, D_MODEL, D_FF), D_MODEL ** -0.5),
        "w_ff2": nrm(ks[11], (DEPTH, D_FF, D_MODEL), D_FF ** -0.5),
        "g_post_ffn": gain(ks[12], (DEPTH, D_MODEL)),
        "rel_bias": nrm(ks[13], (N_REL_BUCKETS, N_ATTN_HEADS), 0.5),
    }


def reference(x, g_pre_mix, w_in, sgu_ln_g, sgu_ln_b, sgu_w, sgu_b, w_out, g_post_mix,
              g_pre_ffn, w_ff1, w_ff2, g_post_ffn, rel_bias):
    for layer in range(DEPTH):
        x = hybrid_layer(x, g_pre_mix[layer], w_in[layer], sgu_ln_g[layer], sgu_ln_b[layer],
                         sgu_w[layer], sgu_b[layer], w_out[layer], g_post_mix[layer],
                         g_pre_ffn[layer], w_ff1[layer], w_ff2[layer], g_post_ffn[layer],
                         rel_bias)
    return x
```

```python
import functools
import math

import jax
import jax.numpy as jnp
from jax import lax
from jax.experimental import pallas as pl
from jax.experimental.pallas import tpu as pltpu

HEAD_DIM = 64
DILATED_BRANCHES = ((128, 1), (512, 4), (2048, 16))
SGU_CHUNK = 128
N_REL_BUCKETS = 32
REL_MAX_DISTANCE = 1024
RMS_EPS = 1e-6
LN_EPS = 1e-5
NEG_INF = -1e30

LANES = 128
HALF_WINDOW = 64
Q_TILE = 128
K_TILE = Q_TILE + 2 * HALF_WINDOW
VMEM_LIMIT_BYTES = 56 * 1024 * 1024

_F32 = jnp.float32
_BF16 = jnp.bfloat16


def _rms_norm(x, g):
    return x * lax.rsqrt(jnp.mean(x * x, axis=-1, keepdims=True) + RMS_EPS) * g


def _inproj_kernel(x_ref, g_ref, w_ref, lng_ref, lnb_ref, wcat_ref, bs_ref, qkv_ref, sgu_ref,
                   *, d_attn, d_sgu):
    tm = x_ref.shape[0]
    h = _rms_norm(x_ref[...], g_ref[...]).astype(_BF16)
    n_qkv = 3 * d_attn

    z = jnp.dot(h, w_ref[:, :n_qkv], preferred_element_type=_F32)
    qkv_ref[:, :d_attn] = z[:, :d_attn] * (HEAD_DIM ** -0.5)
    qkv_ref[:, d_attn:] = z[:, d_attn:]

    zg = jax.nn.gelu(jnp.dot(h, w_ref[:, n_qkv:], preferred_element_type=_F32))
    u = zg[:, :d_sgu]
    v = zg[:, d_sgu:]
    mu = jnp.mean(v, axis=-1, keepdims=True)
    vc = v - mu
    vn = vc * lax.rsqrt(jnp.mean(vc * vc, axis=-1, keepdims=True) + LN_EPS) * lng_ref[...] + lnb_ref[...]

    lane = lax.broadcasted_iota(jnp.int32, (SGU_CHUNK, LANES), 1)
    lo = lane < HEAD_DIM
    for c in range(tm // SGU_CHUNK):
        rows = slice(c * SGU_CHUNK, (c + 1) * SGU_CHUNK)
        for j in range(d_sgu // LANES):
            cols = slice(j * LANES, (j + 1) * LANES)
            slab = vn[rows, cols]
            rhs = jnp.concatenate([jnp.where(lo, slab, 0.0), jnp.where(lo, 0.0, slab)], axis=0)
            mixed = jnp.dot(wcat_ref[j], rhs.astype(_BF16), preferred_element_type=_F32) + bs_ref[j]
            sgu_ref[rows, cols] = (u[rows, cols] * mixed).astype(_BF16)


def _inproj_call(x2d, g_pre_mix, w_in, sgu_ln_g, sgu_ln_b, sgu_w, sgu_b, *, d_attn, d_sgu, tm):
    n_tok, d_model = x2d.shape
    n_slab = d_sgu // LANES
    n_groups = sgu_w.shape[0]
    wcat = (sgu_w.reshape(n_slab, n_groups // n_slab, SGU_CHUNK, SGU_CHUNK)
            .transpose(0, 2, 1, 3).reshape(n_slab, SGU_CHUNK, 2 * SGU_CHUNK).astype(_BF16))
    bs = jnp.repeat(sgu_b.reshape(n_slab, 2, SGU_CHUNK).transpose(0, 2, 1), HEAD_DIM, axis=2)
    d_in = w_in.shape[1]
    const = lambda *shape: pl.BlockSpec(shape, lambda i: (0,) * len(shape))
    return pl.pallas_call(
        functools.partial(_inproj_kernel, d_attn=d_attn, d_sgu=d_sgu),
        grid=(n_tok // tm,),
        in_specs=[
            pl.BlockSpec((tm, d_model), lambda i: (i, 0)),
            const(1, d_model),
            const(d_model, d_in),
            const(1, d_sgu),
            const(1, d_sgu),
            const(n_slab, SGU_CHUNK, 2 * SGU_CHUNK),
            const(n_slab, SGU_CHUNK, LANES),
        ],
        out_specs=[
            pl.BlockSpec((tm, 3 * d_attn), lambda i: (i, 0)),
            pl.BlockSpec((tm, d_sgu), lambda i: (i, 0)),
        ],
        out_shape=[
            jax.ShapeDtypeStruct((n_tok, 3 * d_attn), _F32),
            jax.ShapeDtypeStruct((n_tok, d_sgu), _BF16),
        ],
        compiler_params=pltpu.CompilerParams(
            dimension_semantics=("arbitrary",), vmem_limit_bytes=VMEM_LIMIT_BYTES),
        name="inproj_sgu",
    )(x2d, g_pre_mix.reshape(1, -1), w_in.astype(_BF16), sgu_ln_g.reshape(1, -1),
      sgu_ln_b.reshape(1, -1), wcat, bs)


def _t5_bucket(rel):
    half = N_REL_BUCKETS // 2
    max_exact = half // 2
    ret = jnp.where(rel > 0, half, 0)
    n = jnp.abs(rel)
    nf = jnp.maximum(n, 1).astype(jnp.float32)
    large = max_exact + (jnp.log(nf / max_exact) / math.log(REL_MAX_DISTANCE / max_exact)
                         * (half - max_exact)).astype(jnp.int32)
    large = jnp.minimum(large, half - 1)
    return ret + jnp.where(n < max_exact, n, large)


def _bucket_tables():
    rel = (jnp.arange(K_TILE) - HALF_WINDOW)[None, :] - jnp.arange(Q_TILE)[:, None]
    tabs = [jnp.where(jnp.abs(rel) <= HALF_WINDOW, _t5_bucket(rel * dil), -1)
            for _, dil in DILATED_BRANCHES]
    return jnp.stack(tabs).astype(jnp.int32)


def _attn_kernel(relb_ref, bucket_ref, q_ref, k_ref, v_ref, o_ref,
                 kp_ref, vp_ref, bias_ref, acc_ref, m_ref, l_ref, *, seq):
    hp = pl.program_id(0)
    lane_q = lax.broadcasted_iota(jnp.int32, (Q_TILE, LANES), 1) < HEAD_DIM
    col = lax.broadcasted_iota(jnp.int32, (Q_TILE, K_TILE), 1)

    @pl.when(pl.program_id(1) == 0)
    def _():
        for i in range(len(DILATED_BRANCHES)):
            bucket = bucket_ref[i]
            for hh in range(2):
                tile = jnp.zeros((Q_TILE, K_TILE), _F32)
                for b in range(N_REL_BUCKETS):
                    tile = jnp.where(bucket == b, relb_ref[b, 2 * hp + hh], tile)
                tile = jnp.where(bucket < 0, NEG_INF, tile)
                rows = slice(hh * Q_TILE, (hh + 1) * Q_TILE)
                bias_ref[i, 0, rows, :] = jnp.where(col < HALF_WINDOW, NEG_INF, tile)
                bias_ref[i, 1, rows, :] = tile
                bias_ref[i, 2, rows, :] = jnp.where(col >= K_TILE - HALF_WINDOW, NEG_INF, tile)

    zero_pad = jnp.zeros((HALF_WINDOW, LANES), _BF16)
    for ref in (kp_ref, vp_ref):
        ref[:HALF_WINDOW, :] = zero_pad
        ref[HALF_WINDOW + seq:, :] = zero_pad

    for i, (_, dil) in enumerate(DILATED_BRANCHES):
        sub_len = seq // dil
        n_tiles = sub_len // Q_TILE
        copy_rows = min(sub_len, 512)
        for r in range(dil):
            for c0 in range(0, sub_len, copy_rows):
                src = pl.ds(r + dil * c0, copy_rows, stride=dil) if dil > 1 else pl.ds(c0, copy_rows)
                dst = pl.ds(HALF_WINDOW + r * sub_len + c0, copy_rows)
                kp_ref[dst, :] = k_ref[src, :].astype(_BF16)
                vp_ref[dst, :] = v_ref[src, :].astype(_BF16)

        def tile_body(j, carry, i=i, dil=dil, n_tiles=n_tiles):
            r = j // n_tiles
            n = j % n_tiles
            start = r + dil * Q_TILE * n
            rows = pl.ds(start, Q_TILE, stride=dil) if dil > 1 else pl.ds(pl.multiple_of(start, Q_TILE), Q_TILE)
            q = q_ref[rows, :]
            q2 = jnp.concatenate([jnp.where(lane_q, q, 0.0), jnp.where(lane_q, 0.0, q)], axis=0).astype(_BF16)
            win = pl.ds(pl.multiple_of(j * Q_TILE, Q_TILE), K_TILE)
            s = lax.dot_general(q2, kp_ref[win, :], (((1,), (1,)), ((), ())), preferred_element_type=_F32)
            variant = jnp.where(n == 0, 0, jnp.where(n == n_tiles - 1, 2, 1))
            s = s + bias_ref[i, variant]
            m2 = jnp.max(s, axis=-1, keepdims=True)
            p = jnp.exp(s - m2)
            l2 = jnp.sum(p, axis=-1, keepdims=True)
            pv = jnp.dot(p.astype(_BF16), vp_ref[win, :], preferred_element_type=_F32)
            o_t = jnp.where(lane_q, pv[:Q_TILE], pv[Q_TILE:])
            m_t = jnp.where(lane_q, m2[:Q_TILE], m2[Q_TILE:])
            l_t = jnp.where(lane_q, l2[:Q_TILE], l2[Q_TILE:])
            if i == 0:
                acc_ref[rows, :] = o_t
                m_ref[rows, :] = m_t
                l_ref[rows, :] = l_t
            else:
                m_old = m_ref[rows, :]
                m_new = jnp.maximum(m_old, m_t)
                a = jnp.exp(m_old - m_new)
                b = jnp.exp(m_t - m_new)
                acc_ref[rows, :] = a * acc_ref[rows, :] + b * o_t
                l_ref[rows, :] = a * l_ref[rows, :] + b * l_t
                m_ref[rows, :] = m_new
            return carry

        lax.fori_loop(0, seq // Q_TILE, tile_body, 0)

    o_ref[...] = (acc_ref[...] / l_ref[...]).astype(o_ref.dtype)


def _attn_call(qkv, rel_bias, *, batch, seq, d_attn):
    n_pairs = d_attn // LANES
    qkv3 = qkv.reshape(batch, seq, 3 * d_attn)
    slab = lambda off: pl.BlockSpec((None, seq, LANES), lambda hp, b: (b, 0, off + hp))
    n_br = len(DILATED_BRANCHES)
    return pl.pallas_call(
        functools.partial(_attn_kernel, seq=seq),
        grid=(n_pairs, batch),
        in_specs=[
            pl.BlockSpec(memory_space=pltpu.SMEM),
            pl.BlockSpec((n_br, Q_TILE, K_TILE), lambda hp, b: (0, 0, 0)),
            slab(0), slab(n_pairs), slab(2 * n_pairs),
        ],
        out_specs=pl.BlockSpec((None, seq, LANES), lambda hp, b: (b, 0, hp)),
        out_shape=jax.ShapeDtypeStruct((batch, seq, d_attn), _BF16),
        scratch_shapes=[
            pltpu.VMEM((seq + 2 * HALF_WINDOW, LANES), _BF16),
            pltpu.VMEM((seq + 2 * HALF_WINDOW, LANES), _BF16),
            pltpu.VMEM((n_br, 3, 2 * Q_TILE, K_TILE), _F32),
            pltpu.VMEM((seq, LANES), _F32),
            pltpu.VMEM((seq, LANES), _F32),
            pltpu.VMEM((seq, LANES), _F32),
        ],
        compiler_params=pltpu.CompilerParams(
            dimension_semantics=("arbitrary", "arbitrary"), vmem_limit_bytes=VMEM_LIMIT_BYTES),
        name="dilated_attn",
    )(rel_bias, _bucket_tables(), qkv3, qkv3, qkv3)


def _outproj_ffn_kernel(attn_ref, sgu_ref, x_ref, wo_ref, g1_ref, g2_ref, w1_ref, w2_ref, g3_ref,
                        o_ref, *, d_attn, ff_chunk):
    mix = (jnp.dot(attn_ref[...], wo_ref[:d_attn, :], preferred_element_type=_F32)
           + jnp.dot(sgu_ref[...], wo_ref[d_attn:, :], preferred_element_type=_F32))
    x1 = x_ref[...] + _rms_norm(mix, g1_ref[...])
    h = _rms_norm(x1, g2_ref[...]).astype(_BF16)
    f = jnp.zeros_like(x1)
    for c0 in range(0, w1_ref.shape[1], ff_chunk):
        a = jnp.maximum(jnp.dot(h, w1_ref[:, c0:c0 + ff_chunk], preferred_element_type=_F32), 0.0)
        f = f + jnp.dot((a * a).astype(_BF16), w2_ref[c0:c0 + ff_chunk, :], preferred_element_type=_F32)
    o_ref[...] = x1 + _rms_norm(f, g3_ref[...])


def _outproj_ffn_call(attn2d, sgu2d, x2d, w_out, g_post_mix, g_pre_ffn, w_ff1, w_ff2, g_post_ffn, *, tm):
    n_tok, d_model = x2d.shape
    d_attn = attn2d.shape[1]
    d_sgu = sgu2d.shape[1]
    d_ff = w_ff1.shape[1]
    const = lambda *shape: pl.BlockSpec(shape, lambda i: (0,) * len(shape), pipeline_mode=pl.Buffered(1))
    return pl.pallas_call(
        functools.partial(_outproj_ffn_kernel, d_attn=d_attn, ff_chunk=1024),
        grid=(n_tok // tm,),
        in_specs=[
            pl.BlockSpec((tm, d_attn), lambda i: (i, 0)),
            pl.BlockSpec((tm, d_sgu), lambda i: (i, 0)),
            pl.BlockSpec((tm, d_model), lambda i: (i, 0)),
            const(d_attn + d_sgu, d_model),
            const(1, d_model),
            const(1, d_model),
            const(d_model, d_ff),
            const(d_ff, d_model),
            const(1, d_model),
        ],
        out_specs=pl.BlockSpec((tm, d_model), lambda i: (i, 0)),
        out_shape=jax.ShapeDtypeStruct((n_tok, d_model), _F32),
        compiler_params=pltpu.CompilerParams(
            dimension_semantics=("arbitrary",), vmem_limit_bytes=VMEM_LIMIT_BYTES),
        name="outproj_ffn",
    )(attn2d, sgu2d, x2d, w_out.astype(_BF16), g_post_mix.reshape(1, -1), g_pre_ffn.reshape(1, -1),
      w_ff1.astype(_BF16), w_ff2.astype(_BF16), g_post_ffn.reshape(1, -1))


def kernel(x, g_pre_mix, w_in, sgu_ln_g, sgu_ln_b, sgu_w, sgu_b, w_out, g_post_mix, g_pre_ffn,
           w_ff1, w_ff2, g_post_ffn, rel_bias):
    batch, seq, d_model = x.shape
    d_sgu = sgu_ln_g.shape[-1]
    d_attn = w_out.shape[1] - d_sgu
    assert d_attn % LANES == 0 and d_sgu % LANES == 0 and sgu_w.shape[-1] == SGU_CHUNK
    assert all(seq % (dil * Q_TILE) == 0 and win == 2 * HALF_WINDOW * dil for win, dil in DILATED_BRANCHES)
    x2d = x.reshape(batch * seq, d_model)
    for layer in range(g_pre_mix.shape[0]):
        qkv, sgu = _inproj_call(x2d, g_pre_mix[layer], w_in[layer], sgu_ln_g[layer], sgu_ln_b[layer],
                                sgu_w[layer], sgu_b[layer], d_attn=d_attn, d_sgu=d_sgu, tm=512)
        attn = _attn_call(qkv, rel_bias, batch=batch, seq=seq, d_attn=d_attn)
        x2d = _outproj_ffn_call(attn.reshape(batch * seq, d_attn), sgu, x2d, w_out[layer],
                                g_post_mix[layer], g_pre_ffn[layer], w_ff1[layer], w_ff2[layer],
                                g_post_ffn[layer], tm=512)
    return x2d.reshape(batch, seq, d_model)
```

```python
import functools
import math

import jax
import jax.numpy as jnp
from jax import lax
from jax.experimental import pallas as pl
from jax.experimental.pallas import tpu as pltpu

HEAD_DIM = 64
DILATED_BRANCHES = ((128, 1), (512, 4), (2048, 16))
SGU_CHUNK = 128
N_REL_BUCKETS = 32
REL_MAX_DISTANCE = 1024
RMS_EPS = 1e-6
LN_EPS = 1e-5
NEG_INF = -1e30

LANES = 128
HALF_WINDOW = 64
Q_TILE = 128
K_TILE = Q_TILE + 2 * HALF_WINDOW
TILE_GROUP = 4
MERGE_ROWS = 256
LOG2E = math.log2(math.e)
VMEM_LIMIT_BYTES = 56 * 1024 * 1024

_F32 = jnp.float32
_BF16 = jnp.bfloat16


def _rms_norm(x, g):
    return x * lax.rsqrt(jnp.mean(x * x, axis=-1, keepdims=True) + RMS_EPS) * g


def _inproj_kernel(x_ref, g_ref, w_ref, lng_ref, lnb_ref, wcat_ref, bs_ref, qkv_ref, sgu_ref,
                   *, d_attn, d_sgu):
    tm = x_ref.shape[0]
    h = _rms_norm(x_ref[...], g_ref[...]).astype(_BF16)
    n_qkv = 3 * d_attn

    z = jnp.dot(h, w_ref[:, :n_qkv], preferred_element_type=_F32)
    qkv_ref[:, :d_attn] = z[:, :d_attn] * (HEAD_DIM ** -0.5 * LOG2E)
    qkv_ref[:, d_attn:] = z[:, d_attn:]

    zg = jax.nn.gelu(jnp.dot(h, w_ref[:, n_qkv:], preferred_element_type=_F32))
    u = zg[:, :d_sgu]
    v = zg[:, d_sgu:]
    mu = jnp.mean(v, axis=-1, keepdims=True)
    vc = v - mu
    vn = vc * lax.rsqrt(jnp.mean(vc * vc, axis=-1, keepdims=True) + LN_EPS) * lng_ref[...] + lnb_ref[...]

    lane = lax.broadcasted_iota(jnp.int32, (SGU_CHUNK, LANES), 1)
    lo = lane < HEAD_DIM
    for c in range(tm // SGU_CHUNK):
        rows = slice(c * SGU_CHUNK, (c + 1) * SGU_CHUNK)
        for j in range(d_sgu // LANES):
            cols = slice(j * LANES, (j + 1) * LANES)
            slab = vn[rows, cols]
            rhs = jnp.concatenate([jnp.where(lo, slab, 0.0), jnp.where(lo, 0.0, slab)], axis=0)
            mixed = jnp.dot(wcat_ref[j], rhs.astype(_BF16), preferred_element_type=_F32) + bs_ref[j]
            sgu_ref[rows, cols] = (u[rows, cols] * mixed).astype(_BF16)


def _inproj_call(x2d, g_pre_mix, w_in, sgu_ln_g, sgu_ln_b, sgu_w, sgu_b, *, d_attn, d_sgu, tm):
    n_tok, d_model = x2d.shape
    n_slab = d_sgu // LANES
    n_groups = sgu_w.shape[0]
    wcat = (sgu_w.reshape(n_slab, n_groups // n_slab, SGU_CHUNK, SGU_CHUNK)
            .transpose(0, 2, 1, 3).reshape(n_slab, SGU_CHUNK, 2 * SGU_CHUNK).astype(_BF16))
    bs = jnp.repeat(sgu_b.reshape(n_slab, 2, SGU_CHUNK).transpose(0, 2, 1), HEAD_DIM, axis=2)
    d_in = w_in.shape[1]
    const = lambda *shape: pl.BlockSpec(shape, lambda i: (0,) * len(shape))
    return pl.pallas_call(
        functools.partial(_inproj_kernel, d_attn=d_attn, d_sgu=d_sgu),
        grid=(n_tok // tm,),
        in_specs=[
            pl.BlockSpec((tm, d_model), lambda i: (i, 0)),
            const(1, d_model),
            const(d_model, d_in),
            const(1, d_sgu),
            const(1, d_sgu),
            const(n_slab, SGU_CHUNK, 2 * SGU_CHUNK),
            const(n_slab, SGU_CHUNK, LANES),
        ],
        out_specs=[
            pl.BlockSpec((tm, 3 * d_attn), lambda i: (i, 0)),
            pl.BlockSpec((tm, d_sgu), lambda i: (i, 0)),
        ],
        out_shape=[
            jax.ShapeDtypeStruct((n_tok, 3 * d_attn), _F32),
            jax.ShapeDtypeStruct((n_tok, d_sgu), _BF16),
        ],
        compiler_params=pltpu.CompilerParams(
            dimension_semantics=("arbitrary",), vmem_limit_bytes=VMEM_LIMIT_BYTES),
        name="inproj_sgu",
    )(x2d, g_pre_mix.reshape(1, -1), w_in.astype(_BF16), sgu_ln_g.reshape(1, -1),
      sgu_ln_b.reshape(1, -1), wcat, bs)


def _t5_bucket(rel):
    half = N_REL_BUCKETS // 2
    max_exact = half // 2
    ret = jnp.where(rel > 0, half, 0)
    n = jnp.abs(rel)
    nf = jnp.maximum(n, 1).astype(jnp.float32)
    large = max_exact + (jnp.log(nf / max_exact) / math.log(REL_MAX_DISTANCE / max_exact)
                         * (half - max_exact)).astype(jnp.int32)
    large = jnp.minimum(large, half - 1)
    return ret + jnp.where(n < max_exact, n, large)


def _bucket_tables():
    rel = (jnp.arange(K_TILE) - HALF_WINDOW)[None, :] - jnp.arange(Q_TILE)[:, None]
    tabs = [jnp.where(jnp.abs(rel) <= HALF_WINDOW, _t5_bucket(rel * dil), -1)
            for _, dil in DILATED_BRANCHES]
    return jnp.stack(tabs).astype(jnp.int32)


def _attn_kernel(relb_ref, bucket_ref, q_ref, k_ref, v_ref, o_ref,
                 kp_ref, vp_ref, bias_ref, ob_ref, mb_ref, lb_ref, *, seq):
    hp = pl.program_id(0)
    lane_q = lax.broadcasted_iota(jnp.int32, (Q_TILE, LANES), 1) < HEAD_DIM
    col = lax.broadcasted_iota(jnp.int32, (Q_TILE, K_TILE), 1)

    @pl.when(pl.program_id(1) == 0)
    def _():
        for i in range(len(DILATED_BRANCHES)):
            bucket = bucket_ref[i]
            for hh in range(2):
                tile = jnp.zeros((Q_TILE, K_TILE), _F32)
                for b in range(N_REL_BUCKETS):
                    tile = jnp.where(bucket == b, relb_ref[b, 2 * hp + hh] * LOG2E, tile)
                tile = jnp.where(bucket < 0, NEG_INF, tile)
                rows = slice(hh * Q_TILE, (hh + 1) * Q_TILE)
                bias_ref[i, 0, rows, :] = jnp.where(col < HALF_WINDOW, NEG_INF, tile)
                bias_ref[i, 1, rows, :] = tile
                bias_ref[i, 2, rows, :] = jnp.where(col >= K_TILE - HALF_WINDOW, NEG_INF, tile)
        vp_ref[:, LANES:] = jnp.ones((vp_ref.shape[0], LANES), _BF16)

    zero_pad = jnp.zeros((HALF_WINDOW, LANES), _BF16)
    for ref in (kp_ref, vp_ref):
        ref[:HALF_WINDOW, :LANES] = zero_pad
        ref[HALF_WINDOW + seq:, :LANES] = zero_pad

    for i, (_, dil) in enumerate(DILATED_BRANCHES):
        sub_len = seq // dil
        n_tiles = sub_len // Q_TILE
        copy_rows = min(sub_len, 512)
        for r in range(dil):
            for c0 in range(0, sub_len, copy_rows):
                src = pl.ds(r + dil * c0, copy_rows, stride=dil) if dil > 1 else pl.ds(c0, copy_rows)
                dst = pl.ds(HALF_WINDOW + r * sub_len + c0, copy_rows)
                kp_ref[dst, :] = k_ref[src, :].astype(_BF16)
                vp_ref[dst, :LANES] = v_ref[src, :].astype(_BF16)

        def group_body(jg, carry, i=i, dil=dil, n_tiles=n_tiles):
            tiles = []
            for g in range(TILE_GROUP):
                j = jg * TILE_GROUP + g
                r = j // n_tiles
                n = j % n_tiles
                start = r + dil * Q_TILE * n
                rows = (pl.ds(start, Q_TILE, stride=dil) if dil > 1
                        else pl.ds(pl.multiple_of(start, Q_TILE), Q_TILE))
                win = pl.ds(pl.multiple_of(j * Q_TILE, Q_TILE), K_TILE)
                variant = jnp.where(n == 0, 0, jnp.where(n == n_tiles - 1, 2, 1))
                tiles.append((rows, win, variant))
            scores = []
            for rows, win, variant in tiles:
                q = q_ref[rows, :]
                q2 = jnp.concatenate([jnp.where(lane_q, q, 0.0), jnp.where(lane_q, 0.0, q)], axis=0)
                s = lax.dot_general(q2.astype(_BF16), kp_ref[win, :], (((1,), (1,)), ((), ())),
                                    preferred_element_type=_F32)
                scores.append(s + bias_ref[i, variant])
            maxes = [jnp.max(s, axis=-1, keepdims=True) for s in scores]
            probs = [jnp.exp2(s - m2).astype(_BF16) for s, m2 in zip(scores, maxes)]
            for (rows, win, _), p, m2 in zip(tiles, probs, maxes):
                pv = jnp.dot(p, vp_ref[win, :], preferred_element_type=_F32)
                ob_ref[i, rows, :] = jnp.where(lane_q, pv[:Q_TILE, :LANES], pv[Q_TILE:, :LANES])
                lb_ref[i, rows, :] = jnp.where(lane_q, pv[:Q_TILE, LANES:], pv[Q_TILE:, LANES:])
                mb_ref[i, rows, :] = jnp.where(lane_q, m2[:Q_TILE], m2[Q_TILE:])
            return carry

        lax.fori_loop(0, seq // (Q_TILE * TILE_GROUP), group_body, 0)

    def merge_body(t, carry):
        rows = pl.ds(pl.multiple_of(t * MERGE_ROWS, MERGE_ROWS), MERGE_ROWS)
        ms = [mb_ref[i, rows, :] for i in range(len(DILATED_BRANCHES))]
        m = functools.reduce(jnp.maximum, ms)
        num = jnp.zeros((MERGE_ROWS, LANES), _F32)
        den = jnp.zeros((MERGE_ROWS, LANES), _F32)
        for i, m_i in enumerate(ms):
            w = jnp.exp2(m_i - m)
            num = num + w * ob_ref[i, rows, :]
            den = den + w * lb_ref[i, rows, :]
        o_ref[rows, :] = (num / den).astype(o_ref.dtype)
        return carry

    lax.fori_loop(0, seq // MERGE_ROWS, merge_body, 0)


def _attn_call(qkv, rel_bias, *, batch, seq, d_attn):
    n_pairs = d_attn // LANES
    qkv3 = qkv.reshape(batch, seq, 3 * d_attn)
    slab = lambda off: pl.BlockSpec((None, seq, LANES), lambda hp, b: (b, 0, off + hp))
    n_br = len(DILATED_BRANCHES)
    return pl.pallas_call(
        functools.partial(_attn_kernel, seq=seq),
        grid=(n_pairs, batch),
        in_specs=[
            pl.BlockSpec(memory_space=pltpu.SMEM),
            pl.BlockSpec((n_br, Q_TILE, K_TILE), lambda hp, b: (0, 0, 0)),
            slab(0), slab(n_pairs), slab(2 * n_pairs),
        ],
        out_specs=pl.BlockSpec((None, seq, LANES), lambda hp, b: (b, 0, hp)),
        out_shape=jax.ShapeDtypeStruct((batch, seq, d_attn), _BF16),
        scratch_shapes=[
            pltpu.VMEM((seq + 2 * HALF_WINDOW, LANES), _BF16),
            pltpu.VMEM((seq + 2 * HALF_WINDOW, 2 * LANES), _BF16),
            pltpu.VMEM((n_br, 3, 2 * Q_TILE, K_TILE), _F32),
            pltpu.VMEM((n_br, seq, LANES), _F32),
            pltpu.VMEM((n_br, seq, LANES), _F32),
            pltpu.VMEM((n_br, seq, LANES), _F32),
        ],
        compiler_params=pltpu.CompilerParams(
            dimension_semantics=("arbitrary", "arbitrary"), vmem_limit_bytes=VMEM_LIMIT_BYTES),
        name="dilated_attn",
    )(rel_bias, _bucket_tables(), qkv3, qkv3, qkv3)


def _outproj_ffn_kernel(attn_ref, sgu_ref, x_ref, wo_ref, g1_ref, g2_ref, w1_ref, w2_ref, g3_ref,
                        o_ref, *, d_attn, ff_chunk):
    mix = (jnp.dot(attn_ref[...], wo_ref[:d_attn, :], preferred_element_type=_F32)
           + jnp.dot(sgu_ref[...], wo_ref[d_attn:, :], preferred_element_type=_F32))
    x1 = x_ref[...] + _rms_norm(mix, g1_ref[...])
    h = _rms_norm(x1, g2_ref[...]).astype(_BF16)
    f = jnp.zeros_like(x1)
    for c0 in range(0, w1_ref.shape[1], ff_chunk):
        a = jnp.maximum(jnp.dot(h, w1_ref[:, c0:c0 + ff_chunk], preferred_element_type=_F32), 0.0)
        f = f + jnp.dot((a * a).astype(_BF16), w2_ref[c0:c0 + ff_chunk, :], preferred_element_type=_F32)
    o_ref[...] = x1 + _rms_norm(f, g3_ref[...])


def _outproj_ffn_call(attn2d, sgu2d, x2d, w_out, g_post_mix, g_pre_ffn, w_ff1, w_ff2, g_post_ffn, *, tm):
    n_tok, d_model = x2d.shape
    d_attn = attn2d.shape[1]
    d_sgu = sgu2d.shape[1]
    d_ff = w_ff1.shape[1]
    const = lambda *shape: pl.BlockSpec(shape, lambda i: (0,) * len(shape), pipeline_mode=pl.Buffered(1))
    return pl.pallas_call(
        functools.partial(_outproj_ffn_kernel, d_attn=d_attn, ff_chunk=1024),
        grid=(n_tok // tm,),
        in_specs=[
            pl.BlockSpec((tm, d_attn), lambda i: (i, 0)),
            pl.BlockSpec((tm, d_sgu), lambda i: (i, 0)),
            pl.BlockSpec((tm, d_model), lambda i: (i, 0)),
            const(d_attn + d_sgu, d_model),
            const(1, d_model),
            const(1, d_model),
            const(d_model, d_ff),
            const(d_ff, d_model),
            const(1, d_model),
        ],
        out_specs=pl.BlockSpec((tm, d_model), lambda i: (i, 0)),
        out_shape=jax.ShapeDtypeStruct((n_tok, d_model), _F32),
        compiler_params=pltpu.CompilerParams(
            dimension_semantics=("arbitrary",), vmem_limit_bytes=VMEM_LIMIT_BYTES),
        name="outproj_ffn",
    )(attn2d, sgu2d, x2d, w_out.astype(_BF16), g_post_mix.reshape(1, -1), g_pre_ffn.reshape(1, -1),
      w_ff1.astype(_BF16), w_ff2.astype(_BF16), g_post_ffn.reshape(1, -1))


def kernel(x, g_pre_mix, w_in, sgu_ln_g, sgu_ln_b, sgu_w, sgu_b, w_out, g_post_mix, g_pre_ffn,
           w_ff1, w_ff2, g_post_ffn, rel_bias):
    batch, seq, d_model = x.shape
    d_sgu = sgu_ln_g.shape[-1]
    d_attn = w_out.shape[1] - d_sgu
    assert d_attn % LANES == 0 and d_sgu % LANES == 0 and sgu_w.shape[-1] == SGU_CHUNK
    assert all(seq % (dil * Q_TILE) == 0 and win == 2 * HALF_WINDOW * dil for win, dil in DILATED_BRANCHES)
    x2d = x.reshape(batch * seq, d_model)
    for layer in range(g_pre_mix.shape[0]):
        qkv, sgu = _inproj_call(x2d, g_pre_mix[layer], w_in[layer], sgu_ln_g[layer], sgu_ln_b[layer],
                                sgu_w[layer], sgu_b[layer], d_attn=d_attn, d_sgu=d_sgu, tm=512)
        attn = _attn_call(qkv, rel_bias, batch=batch, seq=seq, d_attn=d_attn)
        x2d = _outproj_ffn_call(attn.reshape(batch * seq, d_attn), sgu, x2d, w_out[layer],
                                g_post_mix[layer], g_pre_ffn[layer], w_ff1[layer], w_ff2[layer],
                                g_post_ffn[layer], tm=512)
    return x2d.reshape(batch, seq, d_model)
```

```python
import functools
import math

import jax
import jax.numpy as jnp
from jax import lax
from jax.experimental import pallas as pl
from jax.experimental.pallas import tpu as pltpu

HEAD_DIM = 64
DILATED_BRANCHES = ((128, 1), (512, 4), (2048, 16))
SGU_CHUNK = 128
N_REL_BUCKETS = 32
REL_MAX_DISTANCE = 1024
RMS_EPS = 1e-6
LN_EPS = 1e-5
NEG_INF = -1e30

LANES = 128
HALF_WINDOW = 64
Q_TILE = 128
K_TILE = Q_TILE + 2 * HALF_WINDOW
TILE_GROUP = 4
MERGE_ROWS = 256
LOG2E = math.log2(math.e)
VMEM_LIMIT_BYTES = 56 * 1024 * 1024

_F32 = jnp.float32
_BF16 = jnp.bfloat16


def _rms_norm(x, g):
    return x * lax.rsqrt(jnp.mean(x * x, axis=-1, keepdims=True) + RMS_EPS) * g


def _inproj_kernel(x_ref, g_ref, w_ref, lng_ref, lnb_ref, wcat_ref, bs_ref, qkv_ref, sgu_ref,
                   *, d_attn, d_sgu, n_sub):
    sub = x_ref.shape[0] // n_sub
    n_qkv = 3 * d_attn
    lane = lax.broadcasted_iota(jnp.int32, (SGU_CHUNK, LANES), 1)
    lo = lane < HEAD_DIM
    pair = 2 * SGU_CHUNK
    assert sub % pair == 0

    def normed(t):
        return _rms_norm(x_ref[t * sub:(t + 1) * sub, :], g_ref[...]).astype(_BF16)

    def gating(t, zg):
        zg = jax.nn.gelu(zg)
        u = zg[:, :d_sgu]
        v = zg[:, d_sgu:]
        mu = jnp.mean(v, axis=-1, keepdims=True)
        vc = v - mu
        vn = vc * lax.rsqrt(jnp.mean(vc * vc, axis=-1, keepdims=True) + LN_EPS) * lng_ref[...] + lnb_ref[...]
        for c0 in range(0, sub, pair):
            for j in range(d_sgu // LANES):
                cols = slice(j * LANES, (j + 1) * LANES)
                stacked = []
                for c in (c0, c0 + SGU_CHUNK):
                    slab = vn[c:c + SGU_CHUNK, cols]
                    stacked.append(jnp.concatenate([jnp.where(lo, slab, 0.0), jnp.where(lo, 0.0, slab)], axis=0))
                rhs = jnp.concatenate(stacked, axis=1).astype(_BF16)
                mixed = jnp.dot(wcat_ref[j], rhs, preferred_element_type=_F32)
                for k, c in enumerate((c0, c0 + SGU_CHUNK)):
                    m = mixed[:, k * LANES:(k + 1) * LANES] + bs_ref[j]
                    rows = slice(t * sub + c, t * sub + c + SGU_CHUNK)
                    sgu_ref[rows, cols] = (u[c:c + SGU_CHUNK, cols] * m).astype(_BF16)

    h = normed(0)
    pending = None
    for t in range(n_sub):
        h_next = normed(t + 1) if t + 1 < n_sub else None
        rows = slice(t * sub, (t + 1) * sub)
        z = jnp.dot(h, w_ref[:, :n_qkv], preferred_element_type=_F32)
        qkv_ref[rows, :d_attn] = z[:, :d_attn] * (HEAD_DIM ** -0.5 * LOG2E)
        qkv_ref[rows, d_attn:] = z[:, d_attn:]
        zg = jnp.dot(h, w_ref[:, n_qkv:], preferred_element_type=_F32)
        if pending is not None:
            gating(*pending)
        pending = (t, zg)
        h = h_next
    gating(*pending)


def _inproj_call(x2d, g_pre_mix, w_in, sgu_ln_g, sgu_ln_b, sgu_w, sgu_b, *, d_attn, d_sgu, tm):
    n_tok, d_model = x2d.shape
    n_slab = d_sgu // LANES
    n_groups = sgu_w.shape[0]
    wcat = (sgu_w.reshape(n_slab, n_groups // n_slab, SGU_CHUNK, SGU_CHUNK)
            .transpose(0, 2, 1, 3).reshape(n_slab, SGU_CHUNK, 2 * SGU_CHUNK).astype(_BF16))
    bs = jnp.repeat(sgu_b.reshape(n_slab, 2, SGU_CHUNK).transpose(0, 2, 1), HEAD_DIM, axis=2)
    d_in = w_in.shape[1]
    const = lambda *shape: pl.BlockSpec(shape, lambda i: (0,) * len(shape))
    return pl.pallas_call(
        functools.partial(_inproj_kernel, d_attn=d_attn, d_sgu=d_sgu, n_sub=tm // 256),
        grid=(n_tok // tm,),
        in_specs=[
            pl.BlockSpec((tm, d_model), lambda i: (i, 0)),
            const(1, d_model),
            const(d_model, d_in),
            const(1, d_sgu),
            const(1, d_sgu),
            const(n_slab, SGU_CHUNK, 2 * SGU_CHUNK),
            const(n_slab, SGU_CHUNK, LANES),
        ],
        out_specs=[
            pl.BlockSpec((tm, 3 * d_attn), lambda i: (i, 0)),
            pl.BlockSpec((tm, d_sgu), lambda i: (i, 0)),
        ],
        out_shape=[
            jax.ShapeDtypeStruct((n_tok, 3 * d_attn), _F32),
            jax.ShapeDtypeStruct((n_tok, d_sgu), _BF16),
        ],
        compiler_params=pltpu.CompilerParams(
            dimension_semantics=("arbitrary",), vmem_limit_bytes=VMEM_LIMIT_BYTES),
        name="inproj_sgu",
    )(x2d, g_pre_mix.reshape(1, -1), w_in.astype(_BF16), sgu_ln_g.reshape(1, -1),
      sgu_ln_b.reshape(1, -1), wcat, bs)


def _t5_bucket(rel):
    half = N_REL_BUCKETS // 2
    max_exact = half // 2
    ret = jnp.where(rel > 0, half, 0)
    n = jnp.abs(rel)
    nf = jnp.maximum(n, 1).astype(jnp.float32)
    large = max_exact + (jnp.log(nf / max_exact) / math.log(REL_MAX_DISTANCE / max_exact)
                         * (half - max_exact)).astype(jnp.int32)
    large = jnp.minimum(large, half - 1)
    return ret + jnp.where(n < max_exact, n, large)


def _bucket_tables():
    rel = (jnp.arange(K_TILE) - HALF_WINDOW)[None, :] - jnp.arange(Q_TILE)[:, None]
    tabs = [jnp.where(jnp.abs(rel) <= HALF_WINDOW, _t5_bucket(rel * dil), -1)
            for _, dil in DILATED_BRANCHES]
    return jnp.stack(tabs).astype(jnp.int32)


def _attn_kernel(relb_ref, bucket_ref, q_ref, k_ref, v_ref, o_ref,
                 kp_ref, vp_ref, bias_ref, p_ref, ob_ref, mb_ref, lb_ref, *, seq):
    hp = pl.program_id(0)
    n_br = len(DILATED_BRANCHES)
    n_groups = seq // (Q_TILE * TILE_GROUP)
    assert n_groups % 2 == 0
    lane_q = lax.broadcasted_iota(jnp.int32, (Q_TILE, LANES), 1) < HEAD_DIM
    col = lax.broadcasted_iota(jnp.int32, (Q_TILE, K_TILE), 1)

    @pl.when(pl.program_id(1) == 0)
    def _():
        for i in range(len(DILATED_BRANCHES)):
            bucket = bucket_ref[i]
            for hh in range(2):
                tile = jnp.zeros((Q_TILE, K_TILE), _F32)
                for b in range(N_REL_BUCKETS):
                    tile = jnp.where(bucket == b, relb_ref[b, 2 * hp + hh] * LOG2E, tile)
                tile = jnp.where(bucket < 0, NEG_INF, tile)
                rows = slice(hh * Q_TILE, (hh + 1) * Q_TILE)
                bias_ref[i, 0, rows, :] = jnp.where(col < HALF_WINDOW, NEG_INF, tile)
                bias_ref[i, 1, rows, :] = tile
                bias_ref[i, 2, rows, :] = jnp.where(col >= K_TILE - HALF_WINDOW, NEG_INF, tile)
        vp_ref[:, :, LANES:] = jnp.ones((n_br, vp_ref.shape[1], LANES), _BF16)

    zero_pad = jnp.zeros((HALF_WINDOW, LANES), _BF16)
    for i, (_, dil) in enumerate(DILATED_BRANCHES):
        for ref in (kp_ref, vp_ref):
            ref[i, :HALF_WINDOW, :LANES] = zero_pad
            ref[i, HALF_WINDOW + seq:, :LANES] = zero_pad
        sub_len = seq // dil
        copy_rows = min(sub_len, 512)
        for r in range(dil):
            for c0 in range(0, sub_len, copy_rows):
                src = pl.ds(r + dil * c0, copy_rows, stride=dil) if dil > 1 else pl.ds(c0, copy_rows)
                dst = pl.ds(HALF_WINDOW + r * sub_len + c0, copy_rows)
                kp_ref[i, dst, :] = k_ref[src, :].astype(_BF16)
                vp_ref[i, dst, :LANES] = v_ref[src, :].astype(_BF16)

    def group_tiles(i, jg):
        dil = DILATED_BRANCHES[i][1]
        n_tiles = seq // (dil * Q_TILE)
        tiles = []
        for g in range(TILE_GROUP):
            j = jg * TILE_GROUP + g
            r = j // n_tiles
            n = j % n_tiles
            start = r + dil * Q_TILE * n
            if isinstance(j, int):
                variant = 0 if n == 0 else (2 if n == n_tiles - 1 else 1)
                win = pl.ds(j * Q_TILE, K_TILE)
            else:
                variant = jnp.where(n == 0, 0, jnp.where(n == n_tiles - 1, 2, 1))
                win = pl.ds(pl.multiple_of(j * Q_TILE, Q_TILE), K_TILE)
                if dil == 1:
                    start = pl.multiple_of(start, Q_TILE)
            rows = pl.ds(start, Q_TILE, stride=dil) if dil > 1 else pl.ds(start, Q_TILE)
            tiles.append((rows, win, variant))
        return tiles

    def stage_scores(i, jg, slot):
        tiles = group_tiles(i, jg)
        scores = []
        for rows, win, variant in tiles:
            q = q_ref[rows, :]
            q2 = jnp.concatenate([jnp.where(lane_q, q, 0.0), jnp.where(lane_q, 0.0, q)], axis=0)
            s = lax.dot_general(q2.astype(_BF16), kp_ref[i, win, :], (((1,), (1,)), ((), ())),
                                preferred_element_type=_F32)
            scores.append(s + bias_ref[i, variant])
        maxes = [jnp.max(s, axis=-1, keepdims=True) for s in scores]
        for g, ((rows, _, _), s, m2) in enumerate(zip(tiles, scores, maxes)):
            p_ref[slot, g] = jnp.exp2(s - m2).astype(_BF16)
            mb_ref[i, rows, :] = jnp.where(lane_q, m2[:Q_TILE], m2[Q_TILE:])

    def stage_values(i, jg, slot):
        for g, (rows, win, _) in enumerate(group_tiles(i, jg)):
            pv = jnp.dot(p_ref[slot, g], vp_ref[i, win, :], preferred_element_type=_F32)
            ob_ref[i, rows, :] = jnp.where(lane_q, pv[:Q_TILE, :LANES], pv[Q_TILE:, :LANES])
            lb_ref[i, rows, :] = jnp.where(lane_q, pv[:Q_TILE, LANES:], pv[Q_TILE:, LANES:])

    stage_scores(0, 0, 0)
    for i in range(n_br):
        def pipelined(u, carry, i=i):
            stage_scores(i, 2 * u + 1, 1)
            stage_values(i, 2 * u, 0)
            stage_scores(i, 2 * u + 2, 0)
            stage_values(i, 2 * u + 1, 1)
            return carry

        lax.fori_loop(0, n_groups // 2 - 1, pipelined, 0)
        stage_scores(i, n_groups - 1, 1)
        stage_values(i, n_groups - 2, 0)
        if i + 1 < n_br:
            stage_scores(i + 1, 0, 0)
        stage_values(i, n_groups - 1, 1)

    def merge_body(t, carry):
        rows = pl.ds(pl.multiple_of(t * MERGE_ROWS, MERGE_ROWS), MERGE_ROWS)
        ms = [mb_ref[i, rows, :] for i in range(len(DILATED_BRANCHES))]
        m = functools.reduce(jnp.maximum, ms)
        num = jnp.zeros((MERGE_ROWS, LANES), _F32)
        den = jnp.zeros((MERGE_ROWS, LANES), _F32)
        for i, m_i in enumerate(ms):
            w = jnp.exp2(m_i - m)
            num = num + w * ob_ref[i, rows, :]
            den = den + w * lb_ref[i, rows, :]
        o_ref[rows, :] = (num / den).astype(o_ref.dtype)
        return carry

    lax.fori_loop(0, seq // MERGE_ROWS, merge_body, 0)


def _attn_call(qkv, rel_bias, *, batch, seq, d_attn):
    n_pairs = d_attn // LANES
    qkv3 = qkv.reshape(batch, seq, 3 * d_attn)
    slab = lambda off: pl.BlockSpec((None, seq, LANES), lambda hp, b: (b, 0, off + hp))
    n_br = len(DILATED_BRANCHES)
    return pl.pallas_call(
        functools.partial(_attn_kernel, seq=seq),
        grid=(n_pairs, batch),
        in_specs=[
            pl.BlockSpec(memory_space=pltpu.SMEM),
            pl.BlockSpec((n_br, Q_TILE, K_TILE), lambda hp, b: (0, 0, 0)),
            slab(0), slab(n_pairs), slab(2 * n_pairs),
        ],
        out_specs=pl.BlockSpec((None, seq, LANES), lambda hp, b: (b, 0, hp)),
        out_shape=jax.ShapeDtypeStruct((batch, seq, d_attn), _BF16),
        scratch_shapes=[
            pltpu.VMEM((n_br, seq + 2 * HALF_WINDOW, LANES), _BF16),
            pltpu.VMEM((n_br, seq + 2 * HALF_WINDOW, 2 * LANES), _BF16),
            pltpu.VMEM((n_br, 3, 2 * Q_TILE, K_TILE), _F32),
            pltpu.VMEM((2, TILE_GROUP, 2 * Q_TILE, K_TILE), _BF16),
            pltpu.VMEM((n_br, seq, LANES), _F32),
            pltpu.VMEM((n_br, seq, LANES), _F32),
            pltpu.VMEM((n_br, seq, LANES), _F32),
        ],
        compiler_params=pltpu.CompilerParams(
            dimension_semantics=("arbitrary", "arbitrary"), vmem_limit_bytes=VMEM_LIMIT_BYTES),
        name="dilated_attn",
    )(rel_bias, _bucket_tables(), qkv3, qkv3, qkv3)


def _outproj_ffn_kernel(attn_ref, sgu_ref, x_ref, wo_ref, g1_ref, g2_ref, w1_ref, w2_ref, g3_ref,
                        o_ref, *, d_attn, ff_chunk, n_sub):
    sub = x_ref.shape[0] // n_sub
    rows = [slice(t * sub, (t + 1) * sub) for t in range(n_sub)]
    chunks = range(0, w1_ref.shape[1], ff_chunk)

    def out_proj(t):
        return (jnp.dot(attn_ref[rows[t], :], wo_ref[:d_attn, :], preferred_element_type=_F32)
                + jnp.dot(sgu_ref[rows[t], :], wo_ref[d_attn:, :], preferred_element_type=_F32))

    def mid_norms(t, mix):
        x1 = x_ref[rows[t], :] + _rms_norm(mix, g1_ref[...])
        return x1, _rms_norm(x1, g2_ref[...]).astype(_BF16)

    def ffn_chunk(h, f, c0):
        a = jnp.maximum(jnp.dot(h, w1_ref[:, c0:c0 + ff_chunk], preferred_element_type=_F32), 0.0)
        part = jnp.dot((a * a).astype(_BF16), w2_ref[c0:c0 + ff_chunk, :], preferred_element_type=_F32)
        return part if f is None else f + part

    mix = out_proj(0)
    prev = None
    for t in range(n_sub):
        nxt_mix = out_proj(t + 1) if t + 1 < n_sub else None
        x1, h = mid_norms(t, mix)
        f = None
        for ci, c0 in enumerate(chunks):
            f = ffn_chunk(h, f, c0)
            if ci == 0 and prev is not None:
                pt, px1, pf = prev
                o_ref[rows[pt], :] = px1 + _rms_norm(pf, g3_ref[...])
        prev = (t, x1, f)
        mix = nxt_mix
    pt, px1, pf = prev
    o_ref[rows[pt], :] = px1 + _rms_norm(pf, g3_ref[...])


def _outproj_ffn_call(attn2d, sgu2d, x2d, w_out, g_post_mix, g_pre_ffn, w_ff1, w_ff2, g_post_ffn, *, tm):
    n_tok, d_model = x2d.shape
    d_attn = attn2d.shape[1]
    d_sgu = sgu2d.shape[1]
    d_ff = w_ff1.shape[1]
    const = lambda *shape: pl.BlockSpec(shape, lambda i: (0,) * len(shape), pipeline_mode=pl.Buffered(1))
    return pl.pallas_call(
        functools.partial(_outproj_ffn_kernel, d_attn=d_attn, ff_chunk=1024, n_sub=4),
        grid=(n_tok // tm,),
        in_specs=[
            pl.BlockSpec((tm, d_attn), lambda i: (i, 0)),
            pl.BlockSpec((tm, d_sgu), lambda i: (i, 0)),
            pl.BlockSpec((tm, d_model), lambda i: (i, 0)),
            const(d_attn + d_sgu, d_model),
            const(1, d_model),
            const(1, d_model),
            const(d_model, d_ff),
            const(d_ff, d_model),
            const(1, d_model),
        ],
        out_specs=pl.BlockSpec((tm, d_model), lambda i: (i, 0)),
        out_shape=jax.ShapeDtypeStruct((n_tok, d_model), _F32),
        compiler_params=pltpu.CompilerParams(
            dimension_semantics=("arbitrary",), vmem_limit_bytes=VMEM_LIMIT_BYTES),
        name="outproj_ffn",
    )(attn2d, sgu2d, x2d, w_out.astype(_BF16), g_post_mix.reshape(1, -1), g_pre_ffn.reshape(1, -1),
      w_ff1.astype(_BF16), w_ff2.astype(_BF16), g_post_ffn.reshape(1, -1))


def kernel(x, g_pre_mix, w_in, sgu_ln_g, sgu_ln_b, sgu_w, sgu_b, w_out, g_post_mix, g_pre_ffn,
           w_ff1, w_ff2, g_post_ffn, rel_bias):
    batch, seq, d_model = x.shape
    d_sgu = sgu_ln_g.shape[-1]
    d_attn = w_out.shape[1] - d_sgu
    assert d_attn % LANES == 0 and d_sgu % LANES == 0 and sgu_w.shape[-1] == SGU_CHUNK
    assert all(seq % (dil * Q_TILE) == 0 and win == 2 * HALF_WINDOW * dil for win, dil in DILATED_BRANCHES)
    x2d = x.reshape(batch * seq, d_model)
    for layer in range(g_pre_mix.shape[0]):
        qkv, sgu = _inproj_call(x2d, g_pre_mix[layer], w_in[layer], sgu_ln_g[layer], sgu_ln_b[layer],
                                sgu_w[layer], sgu_b[layer], d_attn=d_attn, d_sgu=d_sgu, tm=1024)
        attn = _attn_call(qkv, rel_bias, batch=batch, seq=seq, d_attn=d_attn)
        x2d = _outproj_ffn_call(attn.reshape(batch * seq, d_attn), sgu, x2d, w_out[layer],
                                g_post_mix[layer], g_pre_ffn[layer], w_ff1[layer], w_ff2[layer],
                                g_post_ffn[layer], tm=1024)
    return x2d.reshape(batch, seq, d_model)
```

```python
import functools
import math

import jax
import jax.numpy as jnp
from jax import lax
from jax.experimental import pallas as pl
from jax.experimental.pallas import tpu as pltpu

HEAD_DIM = 64
DILATED_BRANCHES = ((128, 1), (512, 4), (2048, 16))
SGU_CHUNK = 128
N_REL_BUCKETS = 32
REL_MAX_DISTANCE = 1024
RMS_EPS = 1e-6
LN_EPS = 1e-5
NEG_INF = -1e30

LANES = 128
HALF_WINDOW = 64
Q_TILE = 128
K_TILE = Q_TILE + 2 * HALF_WINDOW
TILE_GROUP = 4
MERGE_ROWS = 256
COPY_ROWS = 512
LOG2E = math.log2(math.e)
VMEM_LIMIT_BYTES = 56 * 1024 * 1024

_F32 = jnp.float32
_BF16 = jnp.bfloat16


def _rms_norm(x, g):
    return x * lax.rsqrt(jnp.mean(x * x, axis=-1, keepdims=True) + RMS_EPS) * g


def _inproj_kernel(x_ref, g_ref, w_ref, lng_ref, lnb_ref, wcat_ref, bs_ref, qkv_ref, sgu_ref,
                   *, d_attn, d_sgu, n_sub):
    sub = x_ref.shape[0] // n_sub
    n_qkv = 3 * d_attn
    lane = lax.broadcasted_iota(jnp.int32, (SGU_CHUNK, LANES), 1)
    lo = lane < HEAD_DIM
    pair = 2 * SGU_CHUNK
    assert sub % pair == 0

    def normed(t):
        return _rms_norm(x_ref[t * sub:(t + 1) * sub, :], g_ref[...]).astype(_BF16)

    def gating(t, zg):
        zg = jax.nn.gelu(zg)
        u = zg[:, :d_sgu]
        v = zg[:, d_sgu:]
        mu = jnp.mean(v, axis=-1, keepdims=True)
        vc = v - mu
        vn = vc * lax.rsqrt(jnp.mean(vc * vc, axis=-1, keepdims=True) + LN_EPS) * lng_ref[...] + lnb_ref[...]
        for c0 in range(0, sub, pair):
            for j in range(d_sgu // LANES):
                cols = slice(j * LANES, (j + 1) * LANES)
                stacked = []
                for c in (c0, c0 + SGU_CHUNK):
                    slab = vn[c:c + SGU_CHUNK, cols]
                    stacked.append(jnp.concatenate([jnp.where(lo, slab, 0.0), jnp.where(lo, 0.0, slab)], axis=0))
                rhs = jnp.concatenate(stacked, axis=1).astype(_BF16)
                mixed = jnp.dot(wcat_ref[j], rhs, preferred_element_type=_F32)
                for k, c in enumerate((c0, c0 + SGU_CHUNK)):
                    m = mixed[:, k * LANES:(k + 1) * LANES] + bs_ref[j]
                    rows = slice(t * sub + c, t * sub + c + SGU_CHUNK)
                    sgu_ref[rows, cols] = (u[c:c + SGU_CHUNK, cols] * m).astype(_BF16)

    h = normed(0)
    pending = None
    for t in range(n_sub):
        h_next = normed(t + 1) if t + 1 < n_sub else None
        rows = slice(t * sub, (t + 1) * sub)
        z = jnp.dot(h, w_ref[:, :n_qkv], preferred_element_type=_F32)
        qkv_ref[rows, :d_attn] = z[:, :d_attn] * (HEAD_DIM ** -0.5 * LOG2E)
        qkv_ref[rows, d_attn:] = z[:, d_attn:]
        zg = jnp.dot(h, w_ref[:, n_qkv:], preferred_element_type=_F32)
        if pending is not None:
            gating(*pending)
        pending = (t, zg)
        h = h_next
    gating(*pending)


def _inproj_call(x2d, g_pre_mix, w_in, sgu_ln_g, sgu_ln_b, sgu_w, sgu_b, *, d_attn, d_sgu, tm):
    n_tok, d_model = x2d.shape
    n_slab = d_sgu // LANES
    n_groups = sgu_w.shape[0]
    wcat = (sgu_w.reshape(n_slab, n_groups // n_slab, SGU_CHUNK, SGU_CHUNK)
            .transpose(0, 2, 1, 3).reshape(n_slab, SGU_CHUNK, 2 * SGU_CHUNK).astype(_BF16))
    bs = jnp.repeat(sgu_b.reshape(n_slab, 2, SGU_CHUNK).transpose(0, 2, 1), HEAD_DIM, axis=2)
    d_in = w_in.shape[1]
    const = lambda *shape: pl.BlockSpec(shape, lambda i: (0,) * len(shape))
    return pl.pallas_call(
        functools.partial(_inproj_kernel, d_attn=d_attn, d_sgu=d_sgu, n_sub=tm // 256),
        grid=(n_tok // tm,),
        in_specs=[
            pl.BlockSpec((tm, d_model), lambda i: (i, 0)),
            const(1, d_model),
            const(d_model, d_in),
            const(1, d_sgu),
            const(1, d_sgu),
            const(n_slab, SGU_CHUNK, 2 * SGU_CHUNK),
            const(n_slab, SGU_CHUNK, LANES),
        ],
        out_specs=[
            pl.BlockSpec((tm, 3 * d_attn), lambda i: (i, 0)),
            pl.BlockSpec((tm, d_sgu), lambda i: (i, 0)),
        ],
        out_shape=[
            jax.ShapeDtypeStruct((n_tok, 3 * d_attn), _F32),
            jax.ShapeDtypeStruct((n_tok, d_sgu), _BF16),
        ],
        compiler_params=pltpu.CompilerParams(
            dimension_semantics=("arbitrary",), vmem_limit_bytes=VMEM_LIMIT_BYTES),
        name="inproj_sgu",
    )(x2d, g_pre_mix.reshape(1, -1), w_in.astype(_BF16), sgu_ln_g.reshape(1, -1),
      sgu_ln_b.reshape(1, -1), wcat, bs)


def _t5_bucket(rel):
    half = N_REL_BUCKETS // 2
    max_exact = half // 2
    ret = jnp.where(rel > 0, half, 0)
    n = jnp.abs(rel)
    nf = jnp.maximum(n, 1).astype(jnp.float32)
    large = max_exact + (jnp.log(nf / max_exact) / math.log(REL_MAX_DISTANCE / max_exact)
                         * (half - max_exact)).astype(jnp.int32)
    large = jnp.minimum(large, half - 1)
    return ret + jnp.where(n < max_exact, n, large)


def _bucket_tables():
    rel = (jnp.arange(K_TILE) - HALF_WINDOW)[None, :] - jnp.arange(Q_TILE)[:, None]
    tabs = [jnp.where(jnp.abs(rel) <= HALF_WINDOW, _t5_bucket(rel * dil), -1)
            for _, dil in DILATED_BRANCHES]
    return jnp.stack(tabs).astype(jnp.int32)


def _attn_kernel(relb_ref, bucket_ref, q_ref, k_ref, v_ref, o_ref,
                 kp_ref, vp_ref, perm_ref, bias_ref, p_ref, ob_ref, mb_ref, lb_ref, *, seq):
    hp = pl.program_id(0)
    n_br = len(DILATED_BRANCHES)
    n_groups = seq // (Q_TILE * TILE_GROUP)
    assert n_groups % 2 == 0
    d1 = DILATED_BRANCHES[1][1]
    assert [d for _, d in DILATED_BRANCHES] == [1, d1, d1 * d1]
    cls_len = seq // d1
    lane_q = lax.broadcasted_iota(jnp.int32, (Q_TILE, LANES), 1) < HEAD_DIM
    col = lax.broadcasted_iota(jnp.int32, (Q_TILE, K_TILE), 1)

    @pl.when(pl.program_id(1) == 0)
    def _():
        for i in range(n_br):
            bucket = bucket_ref[i]
            for hh in range(2):
                tile = jnp.full((Q_TILE, K_TILE), relb_ref[0, 2 * hp + hh] * LOG2E, _F32)
                for b in range(1, N_REL_BUCKETS):
                    tile = jnp.where(bucket >= b, relb_ref[b, 2 * hp + hh] * LOG2E, tile)
                tile = jnp.where(bucket < 0, NEG_INF, tile)
                rows = slice(hh * Q_TILE, (hh + 1) * Q_TILE)
                bias_ref[i, 0, rows, :] = jnp.where(col < HALF_WINDOW, NEG_INF, tile)
                bias_ref[i, 1, rows, :] = tile
                bias_ref[i, 2, rows, :] = jnp.where(col >= K_TILE - HALF_WINDOW, NEG_INF, tile)
    zero_pad = jnp.zeros((HALF_WINDOW, LANES), _BF16)
    for i in range(n_br):
        for ref in (kp_ref, vp_ref):
            ref[i, :HALF_WINDOW, :] = zero_pad
            ref[i, HALF_WINDOW + seq:, :] = zero_pad
    for c0 in range(0, seq, COPY_ROWS):
        kp_ref[0, pl.ds(HALF_WINDOW + c0, COPY_ROWS), :] = k_ref[pl.ds(c0, COPY_ROWS), :].astype(_BF16)
        vp_ref[0, pl.ds(HALF_WINDOW + c0, COPY_ROWS), :] = v_ref[pl.ds(c0, COPY_ROWS), :].astype(_BF16)
    for r in range(d1):
        for c0 in range(0, cls_len, COPY_ROWS):
            src = pl.ds(r + d1 * c0, COPY_ROWS, stride=d1)
            dst = pl.ds(r * cls_len + c0, COPY_ROWS)
            padded = pl.ds(HALF_WINDOW + r * cls_len + c0, COPY_ROWS)
            perm_ref[0, dst, :] = q_ref[src, :]
            for a, (x_ref, xp_ref) in enumerate(((k_ref, kp_ref), (v_ref, vp_ref)), start=1):
                x = x_ref[src, :]
                perm_ref[a, dst, :] = x
                xp_ref[1, padded, :] = x.astype(_BF16)
    sub_len = cls_len // d1
    for r2 in range(d1 * d1):
        src = pl.ds((r2 % d1) * cls_len + r2 // d1, sub_len, stride=d1)
        padded = pl.ds(HALF_WINDOW + r2 * sub_len, sub_len)
        kp_ref[2, padded, :] = perm_ref[1, src, :].astype(_BF16)
        vp_ref[2, padded, :] = perm_ref[2, src, :].astype(_BF16)

    def group_tiles(i, jg):
        n_tiles = seq // (DILATED_BRANCHES[i][1] * Q_TILE)
        tiles = []
        for g in range(TILE_GROUP):
            j = jg * TILE_GROUP + g
            n = j % n_tiles
            static = isinstance(j, int)
            if static:
                variant = 0 if n == 0 else (2 if n == n_tiles - 1 else 1)
            else:
                variant = jnp.where(n == 0, 0, jnp.where(n == n_tiles - 1, 2, 1))
            tile_start = j * Q_TILE if static else pl.multiple_of(j * Q_TILE, Q_TILE)
            if i < 2:
                rows = pl.ds(tile_start, Q_TILE)
            else:
                r2 = j // n_tiles
                rows = pl.ds((r2 % d1) * cls_len + r2 // d1 + d1 * Q_TILE * n, Q_TILE, stride=d1)
            tiles.append((rows, pl.ds(tile_start, K_TILE), variant))
        return tiles

    def stage_scores(i, jg, slot):
        tiles = group_tiles(i, jg)
        scores = []
        for rows, win, variant in tiles:
            q = q_ref[rows, :] if i == 0 else perm_ref[0, rows, :]
            q2 = jnp.concatenate([jnp.where(lane_q, q, 0.0), jnp.where(lane_q, 0.0, q)], axis=0)
            s = lax.dot_general(q2.astype(_BF16), kp_ref[i, win, :], (((1,), (1,)), ((), ())),
                                preferred_element_type=_F32)
            scores.append(s + bias_ref[i, variant])
        maxes = [jnp.max(s, axis=-1, keepdims=True) for s in scores]
        for g, ((rows, _, _), s, m2) in enumerate(zip(tiles, scores, maxes)):
            p_ref[slot, g] = jnp.exp2(s - m2).astype(_BF16)
            mb_ref[i, rows, :] = jnp.where(lane_q, m2[:Q_TILE], m2[Q_TILE:])

    ones = jnp.ones((K_TILE, LANES), _BF16)

    def stage_values(i, jg, slot):
        for g, (rows, win, _) in enumerate(group_tiles(i, jg)):
            v1 = jnp.concatenate([vp_ref[i, win, :], ones], axis=1)
            pv = jnp.dot(p_ref[slot, g], v1, preferred_element_type=_F32)
            ob_ref[i, rows, :] = jnp.where(lane_q, pv[:Q_TILE, :LANES], pv[Q_TILE:, :LANES])
            lb_ref[i, rows, :] = jnp.where(lane_q, pv[:Q_TILE, LANES:], pv[Q_TILE:, LANES:])

    stage_scores(0, 0, 0)
    for i in range(n_br):
        def pipelined(u, carry, i=i):
            stage_scores(i, 2 * u + 1, 1)
            stage_values(i, 2 * u, 0)
            stage_scores(i, 2 * u + 2, 0)
            stage_values(i, 2 * u + 1, 1)
            return carry

        lax.fori_loop(0, n_groups // 2 - 1, pipelined, 0)
        stage_scores(i, n_groups - 1, 1)
        stage_values(i, n_groups - 2, 0)
        if i + 1 < n_br:
            stage_scores(i + 1, 0, 0)
        stage_values(i, n_groups - 1, 1)

    blocks_per_class = cls_len // MERGE_ROWS

    def merge_body(t, carry):
        cls_rows = pl.ds(pl.multiple_of(t * MERGE_ROWS, MERGE_ROWS), MERGE_ROWS)
        nat_rows = pl.ds(t // blocks_per_class + d1 * MERGE_ROWS * (t % blocks_per_class), MERGE_ROWS, stride=d1)
        rows = [nat_rows] + [cls_rows] * (n_br - 1)
        ms = [mb_ref[i, rows[i], :] for i in range(n_br)]
        m = functools.reduce(jnp.maximum, ms)
        num = jnp.zeros((MERGE_ROWS, LANES), _F32)
        den = jnp.zeros((MERGE_ROWS, LANES), _F32)
        for i, m_i in enumerate(ms):
            w = jnp.exp2(m_i - m)
            num = num + w * ob_ref[i, rows[i], :]
            den = den + w * lb_ref[i, rows[i], :]
        perm_ref[1, nat_rows, :] = num / den
        return carry

    lax.fori_loop(0, seq // MERGE_ROWS, merge_body, 0)
    for c0 in range(0, seq, COPY_ROWS):
        o_ref[pl.ds(c0, COPY_ROWS), :] = perm_ref[1, pl.ds(c0, COPY_ROWS), :].astype(o_ref.dtype)


def _attn_call(qkv, rel_bias, *, batch, seq, d_attn):
    n_pairs = d_attn // LANES
    qkv3 = qkv.reshape(batch, seq, 3 * d_attn)
    slab = lambda off: pl.BlockSpec((None, seq, LANES), lambda hp, b: (b, 0, off + hp))
    n_br = len(DILATED_BRANCHES)
    return pl.pallas_call(
        functools.partial(_attn_kernel, seq=seq),
        grid=(n_pairs, batch),
        in_specs=[
            pl.BlockSpec(memory_space=pltpu.SMEM),
            pl.BlockSpec((n_br, Q_TILE, K_TILE), lambda hp, b: (0, 0, 0)),
            slab(0), slab(n_pairs), slab(2 * n_pairs),
        ],
        out_specs=pl.BlockSpec((None, seq, LANES), lambda hp, b: (b, 0, hp)),
        out_shape=jax.ShapeDtypeStruct((batch, seq, d_attn), _BF16),
        scratch_shapes=[
            pltpu.VMEM((n_br, seq + 2 * HALF_WINDOW, LANES), _BF16),
            pltpu.VMEM((n_br, seq + 2 * HALF_WINDOW, LANES), _BF16),
            pltpu.VMEM((3, seq, LANES), _F32),
            pltpu.VMEM((n_br, 3, 2 * Q_TILE, K_TILE), _F32),
            pltpu.VMEM((2, TILE_GROUP, 2 * Q_TILE, K_TILE), _BF16),
            pltpu.VMEM((n_br, seq, LANES), _F32),
            pltpu.VMEM((n_br, seq, LANES), _F32),
            pltpu.VMEM((n_br, seq, LANES), _F32),
        ],
        compiler_params=pltpu.CompilerParams(
            dimension_semantics=("arbitrary", "arbitrary"), vmem_limit_bytes=VMEM_LIMIT_BYTES),
        name="dilated_attn",
    )(rel_bias, _bucket_tables(), qkv3, qkv3, qkv3)


def _outproj_ffn_kernel(attn_ref, sgu_ref, x_ref, wo_ref, g1_ref, g2_ref, w1_ref, w2_ref, g3_ref,
                        o_ref, *, d_attn, ff_chunk, n_sub):
    sub = x_ref.shape[0] // n_sub
    rows = [slice(t * sub, (t + 1) * sub) for t in range(n_sub)]
    chunks = range(0, w1_ref.shape[1], ff_chunk)

    def out_proj(t):
        return (jnp.dot(attn_ref[rows[t], :], wo_ref[:d_attn, :], preferred_element_type=_F32)
                + jnp.dot(sgu_ref[rows[t], :], wo_ref[d_attn:, :], preferred_element_type=_F32))

    def mid_norms(t, mix):
        x1 = x_ref[rows[t], :] + _rms_norm(mix, g1_ref[...])
        return x1, _rms_norm(x1, g2_ref[...]).astype(_BF16)

    def ffn_chunk(h, f, c0):
        a = jnp.maximum(jnp.dot(h, w1_ref[:, c0:c0 + ff_chunk], preferred_element_type=_F32), 0.0)
        part = jnp.dot((a * a).astype(_BF16), w2_ref[c0:c0 + ff_chunk, :], preferred_element_type=_F32)
        return part if f is None else f + part

    mix = out_proj(0)
    prev = None
    for t in range(n_sub):
        nxt_mix = out_proj(t + 1) if t + 1 < n_sub else None
        x1, h = mid_norms(t, mix)
        f = None
        for ci, c0 in enumerate(chunks):
            f = ffn_chunk(h, f, c0)
            if ci == 0 and prev is not None:
                pt, px1, pf = prev
                o_ref[rows[pt], :] = px1 + _rms_norm(pf, g3_ref[...])
        prev = (t, x1, f)
        mix = nxt_mix
    pt, px1, pf = prev
    o_ref[rows[pt], :] = px1 + _rms_norm(pf, g3_ref[...])


def _outproj_ffn_call(attn2d, sgu2d, x2d, w_out, g_post_mix, g_pre_ffn, w_ff1, w_ff2, g_post_ffn, *, tm):
    n_tok, d_model = x2d.shape
    d_attn = attn2d.shape[1]
    d_sgu = sgu2d.shape[1]
    d_ff = w_ff1.shape[1]
    const = lambda *shape: pl.BlockSpec(shape, lambda i: (0,) * len(shape), pipeline_mode=pl.Buffered(1))
    return pl.pallas_call(
        functools.partial(_outproj_ffn_kernel, d_attn=d_attn, ff_chunk=1024, n_sub=4),
        grid=(n_tok // tm,),
        in_specs=[
            pl.BlockSpec((tm, d_attn), lambda i: (i, 0)),
            pl.BlockSpec((tm, d_sgu), lambda i: (i, 0)),
            pl.BlockSpec((tm, d_model), lambda i: (i, 0)),
            const(d_attn + d_sgu, d_model),
            const(1, d_model),
            const(1, d_model),
            const(d_model, d_ff),
            const(d_ff, d_model),
            const(1, d_model),
        ],
        out_specs=pl.BlockSpec((tm, d_model), lambda i: (i, 0)),
        out_shape=jax.ShapeDtypeStruct((n_tok, d_model), _F32),
        compiler_params=pltpu.CompilerParams(
            dimension_semantics=("arbitrary",), vmem_limit_bytes=VMEM_LIMIT_BYTES),
        name="outproj_ffn",
    )(attn2d, sgu2d, x2d, w_out.astype(_BF16), g_post_mix.reshape(1, -1), g_pre_ffn.reshape(1, -1),
      w_ff1.astype(_BF16), w_ff2.astype(_BF16), g_post_ffn.reshape(1, -1))


def kernel(x, g_pre_mix, w_in, sgu_ln_g, sgu_ln_b, sgu_w, sgu_b, w_out, g_post_mix, g_pre_ffn,
           w_ff1, w_ff2, g_post_ffn, rel_bias):
    batch, seq, d_model = x.shape
    d_sgu = sgu_ln_g.shape[-1]
    d_attn = w_out.shape[1] - d_sgu
    assert d_attn % LANES == 0 and d_sgu % LANES == 0 and sgu_w.shape[-1] == SGU_CHUNK
    assert all(seq % (dil * Q_TILE) == 0 and win == 2 * HALF_WINDOW * dil for win, dil in DILATED_BRANCHES)
    x2d = x.reshape(batch * seq, d_model)
    for layer in range(g_pre_mix.shape[0]):
        qkv, sgu = _inproj_call(x2d, g_pre_mix[layer], w_in[layer], sgu_ln_g[layer], sgu_ln_b[layer],
                                sgu_w[layer], sgu_b[layer], d_attn=d_attn, d_sgu=d_sgu, tm=1024)
        attn = _attn_call(qkv, rel_bias, batch=batch, seq=seq, d_attn=d_attn)
        x2d = _outproj_ffn_call(attn.reshape(batch * seq, d_attn), sgu, x2d, w_out[layer],
                                g_post_mix[layer], g_pre_ffn[layer], w_ff1[layer], w_ff2[layer],
                                g_post_ffn[layer], tm=1024)
    return x2d.reshape(batch, seq, d_model)
```

```python
import functools
import math

import jax
import jax.numpy as jnp
from jax import lax
from jax.experimental import pallas as pl
from jax.experimental.pallas import tpu as pltpu

HEAD_DIM = 64
DILATED_BRANCHES = ((128, 1), (512, 4), (2048, 16))
SGU_CHUNK = 128
N_REL_BUCKETS = 32
REL_MAX_DISTANCE = 1024
RMS_EPS = 1e-6
LN_EPS = 1e-5
NEG_INF = -1e30

LANES = 128
HALF_WINDOW = 64
Q_TILE = 128
K_TILE = Q_TILE + 2 * HALF_WINDOW
TILE_GROUP = 4
MERGE_ROWS = 256
COPY_ROWS = 512
LOG2E = math.log2(math.e)
VMEM_LIMIT_BYTES = 56 * 1024 * 1024

_F32 = jnp.float32
_BF16 = jnp.bfloat16


def _rms_norm(x, g):
    return x * lax.rsqrt(jnp.mean(x * x, axis=-1, keepdims=True) + RMS_EPS) * g


def _inproj_kernel(x_ref, g_ref, w_ref, lng_ref, lnb_ref, wcat_ref, bs_ref, qkv_ref, sgu_ref,
                   *, d_attn, d_sgu, n_sub):
    sub = x_ref.shape[0] // n_sub
    n_qkv = 3 * d_attn
    lane = lax.broadcasted_iota(jnp.int32, (SGU_CHUNK, LANES), 1)
    lo = lane < HEAD_DIM
    pair = 2 * SGU_CHUNK
    assert sub % pair == 0

    def normed(t):
        return _rms_norm(x_ref[t * sub:(t + 1) * sub, :], g_ref[...]).astype(_BF16)

    def gating(t, zg):
        zg = jax.nn.gelu(zg)
        u = zg[:, :d_sgu]
        v = zg[:, d_sgu:]
        mu = jnp.mean(v, axis=-1, keepdims=True)
        vc = v - mu
        vn = vc * lax.rsqrt(jnp.mean(vc * vc, axis=-1, keepdims=True) + LN_EPS) * lng_ref[...] + lnb_ref[...]
        for c0 in range(0, sub, pair):
            for j in range(d_sgu // LANES):
                cols = slice(j * LANES, (j + 1) * LANES)
                stacked = []
                for c in (c0, c0 + SGU_CHUNK):
                    slab = vn[c:c + SGU_CHUNK, cols]
                    stacked.append(jnp.concatenate([jnp.where(lo, slab, 0.0), jnp.where(lo, 0.0, slab)], axis=0))
                rhs = jnp.concatenate(stacked, axis=1).astype(_BF16)
                mixed = jnp.dot(wcat_ref[j], rhs, preferred_element_type=_F32)
                for k, c in enumerate((c0, c0 + SGU_CHUNK)):
                    m = mixed[:, k * LANES:(k + 1) * LANES] + bs_ref[j]
                    rows = slice(t * sub + c, t * sub + c + SGU_CHUNK)
                    sgu_ref[rows, cols] = (u[c:c + SGU_CHUNK, cols] * m).astype(_BF16)

    h = normed(0)
    pending = None
    for t in range(n_sub):
        h_next = normed(t + 1) if t + 1 < n_sub else None
        rows = slice(t * sub, (t + 1) * sub)
        z = jnp.dot(h, w_ref[:, :n_qkv], preferred_element_type=_F32)
        qkv_ref[rows, :d_attn] = z[:, :d_attn] * (HEAD_DIM ** -0.5 * LOG2E)
        qkv_ref[rows, d_attn:] = z[:, d_attn:]
        zg = jnp.dot(h, w_ref[:, n_qkv:], preferred_element_type=_F32)
        if pending is not None:
            gating(*pending)
        pending = (t, zg)
        h = h_next
    gating(*pending)


def _inproj_call(x2d, g_pre_mix, w_in, sgu_ln_g, sgu_ln_b, sgu_w, sgu_b, *, d_attn, d_sgu, tm):
    n_tok, d_model = x2d.shape
    n_slab = d_sgu // LANES
    n_groups = sgu_w.shape[0]
    wcat = (sgu_w.reshape(n_slab, n_groups // n_slab, SGU_CHUNK, SGU_CHUNK)
            .transpose(0, 2, 1, 3).reshape(n_slab, SGU_CHUNK, 2 * SGU_CHUNK).astype(_BF16))
    bs = jnp.repeat(sgu_b.reshape(n_slab, 2, SGU_CHUNK).transpose(0, 2, 1), HEAD_DIM, axis=2)
    d_in = w_in.shape[1]
    const = lambda *shape: pl.BlockSpec(shape, lambda i: (0,) * len(shape))
    return pl.pallas_call(
        functools.partial(_inproj_kernel, d_attn=d_attn, d_sgu=d_sgu, n_sub=tm // 256),
        grid=(n_tok // tm,),
        in_specs=[
            pl.BlockSpec((tm, d_model), lambda i: (i, 0)),
            const(1, d_model),
            const(d_model, d_in),
            const(1, d_sgu),
            const(1, d_sgu),
            const(n_slab, SGU_CHUNK, 2 * SGU_CHUNK),
            const(n_slab, SGU_CHUNK, LANES),
        ],
        out_specs=[
            pl.BlockSpec((tm, 3 * d_attn), lambda i: (i, 0)),
            pl.BlockSpec((tm, d_sgu), lambda i: (i, 0)),
        ],
        out_shape=[
            jax.ShapeDtypeStruct((n_tok, 3 * d_attn), _F32),
            jax.ShapeDtypeStruct((n_tok, d_sgu), _BF16),
        ],
        compiler_params=pltpu.CompilerParams(
            dimension_semantics=("arbitrary",), vmem_limit_bytes=VMEM_LIMIT_BYTES),
        name="inproj_sgu",
    )(x2d, g_pre_mix.reshape(1, -1), w_in.astype(_BF16), sgu_ln_g.reshape(1, -1),
      sgu_ln_b.reshape(1, -1), wcat, bs)


def _t5_bucket(rel):
    half = N_REL_BUCKETS // 2
    max_exact = half // 2
    ret = jnp.where(rel > 0, half, 0)
    n = jnp.abs(rel)
    nf = jnp.maximum(n, 1).astype(jnp.float32)
    large = max_exact + (jnp.log(nf / max_exact) / math.log(REL_MAX_DISTANCE / max_exact)
                         * (half - max_exact)).astype(jnp.int32)
    large = jnp.minimum(large, half - 1)
    return ret + jnp.where(n < max_exact, n, large)


def _bucket_tables():
    rel = (jnp.arange(K_TILE) - HALF_WINDOW)[None, :] - jnp.arange(Q_TILE)[:, None]
    tabs = [jnp.where(jnp.abs(rel) <= HALF_WINDOW, _t5_bucket(rel * dil), -1)
            for _, dil in DILATED_BRANCHES]
    return jnp.stack(tabs).astype(jnp.int32)


def _attn_kernel(relb_ref, bucket_ref, q_ref, k_ref, v_ref, o_ref,
                 kp_ref, vp_ref, perm_ref, bias_ref, p_ref, ob_ref, mb_ref, lb_ref, *, seq):
    hp = pl.program_id(0)
    n_br = len(DILATED_BRANCHES)
    n_groups = seq // (Q_TILE * TILE_GROUP)
    assert n_groups % 2 == 0
    d1 = DILATED_BRANCHES[1][1]
    assert [d for _, d in DILATED_BRANCHES] == [1, d1, d1 * d1]
    cls_len = seq // d1
    lane_q = lax.broadcasted_iota(jnp.int32, (Q_TILE, LANES), 1) < HEAD_DIM
    col = lax.broadcasted_iota(jnp.int32, (Q_TILE, K_TILE), 1)

    @pl.when(pl.program_id(1) == 0)
    def _():
        for i in range(n_br):
            bucket = bucket_ref[i]
            for hh in range(2):
                tile = jnp.full((Q_TILE, K_TILE), relb_ref[0, 2 * hp + hh] * LOG2E, _F32)
                for b in range(1, N_REL_BUCKETS):
                    tile = jnp.where(bucket >= b, relb_ref[b, 2 * hp + hh] * LOG2E, tile)
                tile = jnp.where(bucket < 0, NEG_INF, tile)
                rows = slice(hh * Q_TILE, (hh + 1) * Q_TILE)
                bias_ref[i, 0, rows, :] = jnp.where(col < HALF_WINDOW, NEG_INF, tile)
                bias_ref[i, 1, rows, :] = tile
                bias_ref[i, 2, rows, :] = jnp.where(col >= K_TILE - HALF_WINDOW, NEG_INF, tile)
    zero_pad = jnp.zeros((HALF_WINDOW, LANES), _BF16)
    for i in range(n_br):
        for ref in (kp_ref, vp_ref):
            ref[i, :HALF_WINDOW, :] = zero_pad
            ref[i, HALF_WINDOW + seq:, :] = zero_pad
    for c0 in range(0, seq, COPY_ROWS):
        kp_ref[0, pl.ds(HALF_WINDOW + c0, COPY_ROWS), :] = k_ref[pl.ds(c0, COPY_ROWS), :].astype(_BF16)
        vp_ref[0, pl.ds(HALF_WINDOW + c0, COPY_ROWS), :] = v_ref[pl.ds(c0, COPY_ROWS), :].astype(_BF16)
    for r in range(d1):
        for c0 in range(0, cls_len, COPY_ROWS):
            src = pl.ds(r + d1 * c0, COPY_ROWS, stride=d1)
            dst = pl.ds(r * cls_len + c0, COPY_ROWS)
            padded = pl.ds(HALF_WINDOW + r * cls_len + c0, COPY_ROWS)
            perm_ref[0, dst, :] = q_ref[src, :]
            for a, (x_ref, xp_ref) in enumerate(((k_ref, kp_ref), (v_ref, vp_ref)), start=1):
                x = x_ref[src, :]
                perm_ref[a, dst, :] = x
                xp_ref[1, padded, :] = x.astype(_BF16)
    sub_len = cls_len // d1
    for r2 in range(d1 * d1):
        src = pl.ds((r2 % d1) * cls_len + r2 // d1, sub_len, stride=d1)
        padded = pl.ds(HALF_WINDOW + r2 * sub_len, sub_len)
        kp_ref[2, padded, :] = perm_ref[1, src, :].astype(_BF16)
        vp_ref[2, padded, :] = perm_ref[2, src, :].astype(_BF16)

    def group_tiles(i, jg):
        n_tiles = seq // (DILATED_BRANCHES[i][1] * Q_TILE)
        tiles = []
        for g in range(TILE_GROUP):
            j = jg * TILE_GROUP + g
            n = j % n_tiles
            static = isinstance(j, int)
            if static:
                variant = 0 if n == 0 else (2 if n == n_tiles - 1 else 1)
            else:
                variant = jnp.where(n == 0, 0, jnp.where(n == n_tiles - 1, 2, 1))
            tile_start = j * Q_TILE if static else pl.multiple_of(j * Q_TILE, Q_TILE)
            if i < 2:
                rows = pl.ds(tile_start, Q_TILE)
            else:
                r2 = j // n_tiles
                rows = pl.ds((r2 % d1) * cls_len + r2 // d1 + d1 * Q_TILE * n, Q_TILE, stride=d1)
            tiles.append((rows, pl.ds(tile_start, K_TILE), variant))
        return tiles

    def stage_scores(i, jg, slot):
        tiles = group_tiles(i, jg)
        scores = []
        for rows, win, variant in tiles:
            q = q_ref[rows, :] if i == 0 else perm_ref[0, rows, :]
            q2 = jnp.concatenate([jnp.where(lane_q, q, 0.0), jnp.where(lane_q, 0.0, q)], axis=0)
            s = lax.dot_general(q2.astype(_BF16), kp_ref[i, win, :], (((1,), (1,)), ((), ())),
                                preferred_element_type=_F32)
            scores.append(s + bias_ref[i, variant])
        maxes = [jnp.max(s, axis=-1, keepdims=True) for s in scores]
        for g, ((rows, _, _), s, m2) in enumerate(zip(tiles, scores, maxes)):
            p_ref[slot, g] = jnp.exp2(s - m2).astype(_BF16)
            mb_ref[i, rows, :] = jnp.where(lane_q, m2[:Q_TILE], m2[Q_TILE:])

    ones = jnp.ones((K_TILE, LANES), _BF16)

    def stage_values(i, jg, slot):
        for g, (rows, win, _) in enumerate(group_tiles(i, jg)):
            v1 = jnp.concatenate([vp_ref[i, win, :], ones], axis=1)
            pv = jnp.dot(p_ref[slot, g], v1, preferred_element_type=_F32)
            ob_ref[i, rows, :] = jnp.where(lane_q, pv[:Q_TILE, :LANES], pv[Q_TILE:, :LANES])
            lb_ref[i, rows, :] = jnp.where(lane_q, pv[:Q_TILE, LANES:], pv[Q_TILE:, LANES:])

    stage_scores(0, 0, 0)
    for i in range(n_br):
        for t in range(n_groups):
            if t + 1 < n_groups:
                stage_scores(i, t + 1, (t + 1) % 2)
            elif i + 1 < n_br:
                stage_scores(i + 1, 0, 0)
            stage_values(i, t, t % 2)

    blocks_per_class = cls_len // MERGE_ROWS

    def merge_body(t, carry):
        cls_rows = pl.ds(pl.multiple_of(t * MERGE_ROWS, MERGE_ROWS), MERGE_ROWS)
        nat_rows = pl.ds(t // blocks_per_class + d1 * MERGE_ROWS * (t % blocks_per_class), MERGE_ROWS, stride=d1)
        rows = [nat_rows] + [cls_rows] * (n_br - 1)
        ms = [mb_ref[i, rows[i], :] for i in range(n_br)]
        m = functools.reduce(jnp.maximum, ms)
        num = jnp.zeros((MERGE_ROWS, LANES), _F32)
        den = jnp.zeros((MERGE_ROWS, LANES), _F32)
        for i, m_i in enumerate(ms):
            w = jnp.exp2(m_i - m)
            num = num + w * ob_ref[i, rows[i], :]
            den = den + w * lb_ref[i, rows[i], :]
        perm_ref[1, nat_rows, :] = num / den
        return carry

    lax.fori_loop(0, seq // MERGE_ROWS, merge_body, 0)
    for c0 in range(0, seq, COPY_ROWS):
        o_ref[pl.ds(c0, COPY_ROWS), :] = perm_ref[1, pl.ds(c0, COPY_ROWS), :].astype(o_ref.dtype)


def _attn_call(qkv, rel_bias, *, batch, seq, d_attn):
    n_pairs = d_attn // LANES
    qkv3 = qkv.reshape(batch, seq, 3 * d_attn)
    slab = lambda off: pl.BlockSpec((None, seq, LANES), lambda hp, b: (b, 0, off + hp))
    n_br = len(DILATED_BRANCHES)
    return pl.pallas_call(
        functools.partial(_attn_kernel, seq=seq),
        grid=(n_pairs, batch),
        in_specs=[
            pl.BlockSpec(memory_space=pltpu.SMEM),
            pl.BlockSpec((n_br, Q_TILE, K_TILE), lambda hp, b: (0, 0, 0)),
            slab(0), slab(n_pairs), slab(2 * n_pairs),
        ],
        out_specs=pl.BlockSpec((None, seq, LANES), lambda hp, b: (b, 0, hp)),
        out_shape=jax.ShapeDtypeStruct((batch, seq, d_attn), _BF16),
        scratch_shapes=[
            pltpu.VMEM((n_br, seq + 2 * HALF_WINDOW, LANES), _BF16),
            pltpu.VMEM((n_br, seq + 2 * HALF_WINDOW, LANES), _BF16),
            pltpu.VMEM((3, seq, LANES), _F32),
            pltpu.VMEM((n_br, 3, 2 * Q_TILE, K_TILE), _F32),
            pltpu.VMEM((2, TILE_GROUP, 2 * Q_TILE, K_TILE), _BF16),
            pltpu.VMEM((n_br, seq, LANES), _F32),
            pltpu.VMEM((n_br, seq, LANES), _F32),
            pltpu.VMEM((n_br, seq, LANES), _F32),
        ],
        compiler_params=pltpu.CompilerParams(
            dimension_semantics=("arbitrary", "arbitrary"), vmem_limit_bytes=VMEM_LIMIT_BYTES),
        name="dilated_attn",
    )(rel_bias, _bucket_tables(), qkv3, qkv3, qkv3)


def _outproj_ffn_kernel(attn_ref, sgu_ref, x_ref, wo_ref, g1_ref, g2_ref, w1_ref, w2_ref, g3_ref,
                        o_ref, *, d_attn, ff_chunk, n_sub):
    sub = x_ref.shape[0] // n_sub
    rows = [slice(t * sub, (t + 1) * sub) for t in range(n_sub)]
    chunks = range(0, w1_ref.shape[1], ff_chunk)

    def out_proj(t):
        return (jnp.dot(attn_ref[rows[t], :], wo_ref[:d_attn, :], preferred_element_type=_F32)
                + jnp.dot(sgu_ref[rows[t], :], wo_ref[d_attn:, :], preferred_element_type=_F32))

    def mid_norms(t, mix):
        x1 = x_ref[rows[t], :] + _rms_norm(mix, g1_ref[...])
        return x1, _rms_norm(x1, g2_ref[...]).astype(_BF16)

    def ffn_chunk(h, f, c0):
        a = jnp.maximum(jnp.dot(h, w1_ref[:, c0:c0 + ff_chunk], preferred_element_type=_F32), 0.0)
        part = jnp.dot((a * a).astype(_BF16), w2_ref[c0:c0 + ff_chunk, :], preferred_element_type=_F32)
        return part if f is None else f + part

    mix = out_proj(0)
    prev = None
    for t in range(n_sub):
        nxt_mix = out_proj(t + 1) if t + 1 < n_sub else None
        x1, h = mid_norms(t, mix)
        f = None
        for ci, c0 in enumerate(chunks):
            f = ffn_chunk(h, f, c0)
            if ci == 0 and prev is not None:
                pt, px1, pf = prev
                o_ref[rows[pt], :] = px1 + _rms_norm(pf, g3_ref[...])
        prev = (t, x1, f)
        mix = nxt_mix
    pt, px1, pf = prev
    o_ref[rows[pt], :] = px1 + _rms_norm(pf, g3_ref[...])


def _outproj_ffn_call(attn2d, sgu2d, x2d, w_out, g_post_mix, g_pre_ffn, w_ff1, w_ff2, g_post_ffn, *, tm):
    n_tok, d_model = x2d.shape
    d_attn = attn2d.shape[1]
    d_sgu = sgu2d.shape[1]
    d_ff = w_ff1.shape[1]
    const = lambda *shape: pl.BlockSpec(shape, lambda i: (0,) * len(shape), pipeline_mode=pl.Buffered(1))
    return pl.pallas_call(
        functools.partial(_outproj_ffn_kernel, d_attn=d_attn, ff_chunk=1024, n_sub=4),
        grid=(n_tok // tm,),
        in_specs=[
            pl.BlockSpec((tm, d_attn), lambda i: (i, 0)),
            pl.BlockSpec((tm, d_sgu), lambda i: (i, 0)),
            pl.BlockSpec((tm, d_model), lambda i: (i, 0)),
            const(d_attn + d_sgu, d_model),
            const(1, d_model),
            const(1, d_model),
            const(d_model, d_ff),
            const(d_ff, d_model),
            const(1, d_model),
        ],
        out_specs=pl.BlockSpec((tm, d_model), lambda i: (i, 0)),
        out_shape=jax.ShapeDtypeStruct((n_tok, d_model), _F32),
        compiler_params=pltpu.CompilerParams(
            dimension_semantics=("arbitrary",), vmem_limit_bytes=VMEM_LIMIT_BYTES),
        name="outproj_ffn",
    )(attn2d, sgu2d, x2d, w_out.astype(_BF16), g_post_mix.reshape(1, -1), g_pre_ffn.reshape(1, -1),
      w_ff1.astype(_BF16), w_ff2.astype(_BF16), g_post_ffn.reshape(1, -1))


def kernel(x, g_pre_mix, w_in, sgu_ln_g, sgu_ln_b, sgu_w, sgu_b, w_out, g_post_mix, g_pre_ffn,
           w_ff1, w_ff2, g_post_ffn, rel_bias):
    batch, seq, d_model = x.shape
    d_sgu = sgu_ln_g.shape[-1]
    d_attn = w_out.shape[1] - d_sgu
    assert d_attn % LANES == 0 and d_sgu % LANES == 0 and sgu_w.shape[-1] == SGU_CHUNK
    assert all(seq % (dil * Q_TILE) == 0 and win == 2 * HALF_WINDOW * dil for win, dil in DILATED_BRANCHES)
    x2d = x.reshape(batch * seq, d_model)
    for layer in range(g_pre_mix.shape[0]):
        qkv, sgu = _inproj_call(x2d, g_pre_mix[layer], w_in[layer], sgu_ln_g[layer], sgu_ln_b[layer],
                                sgu_w[layer], sgu_b[layer], d_attn=d_attn, d_sgu=d_sgu, tm=1024)
        attn = _attn_call(qkv, rel_bias, batch=batch, seq=seq, d_attn=d_attn)
        x2d = _outproj_ffn_call(attn.reshape(batch * seq, d_attn), sgu, x2d, w_out[layer],
                                g_post_mix[layer], g_pre_ffn[layer], w_ff1[layer], w_ff2[layer],
                                g_post_ffn[layer], tm=1024)
    return x2d.reshape(batch, seq, d_model)
```

```python
import functools
import math

import jax
import jax.numpy as jnp
from jax import lax
from jax.experimental import pallas as pl
from jax.experimental.pallas import tpu as pltpu

HEAD_DIM = 64
DILATED_BRANCHES = ((128, 1), (512, 4), (2048, 16))
SGU_CHUNK = 128
N_REL_BUCKETS = 32
REL_MAX_DISTANCE = 1024
RMS_EPS = 1e-6
LN_EPS = 1e-5
NEG_INF = -1e30

LANES = 128
HALF_WINDOW = 64
Q_TILE = 128
K_TILE = Q_TILE + 2 * HALF_WINDOW
TILE_GROUP = 4
MERGE_ROWS = 256
COPY_ROWS = 512
LOG2E = math.log2(math.e)
VMEM_LIMIT_BYTES = 56 * 1024 * 1024

_F32 = jnp.float32
_BF16 = jnp.bfloat16


def _rms_norm(x, g):
    return x * lax.rsqrt(jnp.mean(x * x, axis=-1, keepdims=True) + RMS_EPS) * g


def _inproj_kernel(x_ref, g_ref, w_ref, lng_ref, lnb_ref, wcat_ref, bs_ref, *rest,
                   d_attn, d_sgu, n_sub, n_later):
    later_f32, (qkv_ref, sgu_ref), later_bf16 = rest[:n_later], rest[n_later:n_later + 2], rest[n_later + 2:]
    sub = x_ref.shape[0] // n_sub
    n_qkv = 3 * d_attn
    lane = lax.broadcasted_iota(jnp.int32, (SGU_CHUNK, LANES), 1)
    lo = lane < HEAD_DIM
    pair = 2 * SGU_CHUNK
    assert sub % pair == 0

    def normed(t):
        return _rms_norm(x_ref[t * sub:(t + 1) * sub, :], g_ref[...]).astype(_BF16)

    def gating(t, zg):
        zg = jax.nn.gelu(zg)
        u = zg[:, :d_sgu]
        v = zg[:, d_sgu:]
        mu = jnp.mean(v, axis=-1, keepdims=True)
        vc = v - mu
        vn = vc * lax.rsqrt(jnp.mean(vc * vc, axis=-1, keepdims=True) + LN_EPS) * lng_ref[...] + lnb_ref[...]
        for c0 in range(0, sub, pair):
            for j in range(d_sgu // LANES):
                cols = slice(j * LANES, (j + 1) * LANES)
                stacked = []
                for c in (c0, c0 + SGU_CHUNK):
                    slab = vn[c:c + SGU_CHUNK, cols]
                    stacked.append(jnp.concatenate([jnp.where(lo, slab, 0.0), jnp.where(lo, 0.0, slab)], axis=0))
                rhs = jnp.concatenate(stacked, axis=1).astype(_BF16)
                mixed = jnp.dot(wcat_ref[j], rhs, preferred_element_type=_F32)
                for k, c in enumerate((c0, c0 + SGU_CHUNK)):
                    m = mixed[:, k * LANES:(k + 1) * LANES] + bs_ref[j]
                    rows = slice(t * sub + c, t * sub + c + SGU_CHUNK)
                    sgu_ref[rows, cols] = (u[c:c + SGU_CHUNK, cols] * m).astype(_BF16)

    h = normed(0)
    pending = None
    for t in range(n_sub):
        h_next = normed(t + 1) if t + 1 < n_sub else None
        rows = slice(t * sub, (t + 1) * sub)
        z = jnp.dot(h, w_ref[:, :n_qkv], preferred_element_type=_F32)
        qkv_ref[rows, :d_attn] = z[:, :d_attn] * (HEAD_DIM ** -0.5 * LOG2E)
        qkv_ref[rows, d_attn:] = z[:, d_attn:]
        zg = jnp.dot(h, w_ref[:, n_qkv:], preferred_element_type=_F32)
        if t == 0:
            for src_ref, dst_ref in zip(later_f32, later_bf16):
                dst_ref[...] = src_ref[...].astype(_BF16)
        if pending is not None:
            gating(*pending)
        pending = (t, zg)
        h = h_next
    gating(*pending)


def _inproj_call(x2d, g_pre_mix, w_in, sgu_ln_g, sgu_ln_b, sgu_w, sgu_b, later_weights, *, d_attn, d_sgu, tm):
    n_tok, d_model = x2d.shape
    n_steps = n_tok // tm
    row_block = lambda w: pl.BlockSpec((w.shape[0] // n_steps, w.shape[1]), lambda i: (i, 0))
    assert all(w.shape[0] % (16 * n_steps) == 0 for w in later_weights)
    n_slab = d_sgu // LANES
    n_groups = sgu_w.shape[0]
    wcat = (sgu_w.reshape(n_slab, n_groups // n_slab, SGU_CHUNK, SGU_CHUNK)
            .transpose(0, 2, 1, 3).reshape(n_slab, SGU_CHUNK, 2 * SGU_CHUNK).astype(_BF16))
    bs = jnp.repeat(sgu_b.reshape(n_slab, 2, SGU_CHUNK).transpose(0, 2, 1), HEAD_DIM, axis=2)
    d_in = w_in.shape[1]
    const = lambda *shape: pl.BlockSpec(shape, lambda i: (0,) * len(shape))
    return pl.pallas_call(
        functools.partial(_inproj_kernel, d_attn=d_attn, d_sgu=d_sgu, n_sub=tm // 256,
                          n_later=len(later_weights)),
        grid=(n_steps,),
        in_specs=[
            pl.BlockSpec((tm, d_model), lambda i: (i, 0)),
            const(1, d_model),
            const(d_model, d_in),
            const(1, d_sgu),
            const(1, d_sgu),
            const(n_slab, SGU_CHUNK, 2 * SGU_CHUNK),
            const(n_slab, SGU_CHUNK, LANES),
        ] + [row_block(w) for w in later_weights],
        out_specs=[
            pl.BlockSpec((tm, 3 * d_attn), lambda i: (i, 0)),
            pl.BlockSpec((tm, d_sgu), lambda i: (i, 0)),
        ] + [row_block(w) for w in later_weights],
        out_shape=[
            jax.ShapeDtypeStruct((n_tok, 3 * d_attn), _F32),
            jax.ShapeDtypeStruct((n_tok, d_sgu), _BF16),
        ] + [jax.ShapeDtypeStruct(w.shape, _BF16) for w in later_weights],
        compiler_params=pltpu.CompilerParams(
            dimension_semantics=("arbitrary",), vmem_limit_bytes=VMEM_LIMIT_BYTES),
        name="inproj_sgu",
    )(x2d, g_pre_mix.reshape(1, -1), w_in.astype(_BF16), sgu_ln_g.reshape(1, -1),
      sgu_ln_b.reshape(1, -1), wcat, bs, *later_weights)


def _t5_bucket(rel):
    half = N_REL_BUCKETS // 2
    max_exact = half // 2
    ret = jnp.where(rel > 0, half, 0)
    n = jnp.abs(rel)
    nf = jnp.maximum(n, 1).astype(jnp.float32)
    large = max_exact + (jnp.log(nf / max_exact) / math.log(REL_MAX_DISTANCE / max_exact)
                         * (half - max_exact)).astype(jnp.int32)
    large = jnp.minimum(large, half - 1)
    return ret + jnp.where(n < max_exact, n, large)


def _bucket_tables():
    rel = jnp.arange(K_TILE) - HALF_WINDOW
    tabs = [jnp.where(jnp.abs(rel) <= HALF_WINDOW, _t5_bucket(rel * dil), -1)
            for _, dil in DILATED_BRANCHES]
    return jnp.broadcast_to(jnp.stack(tabs)[:, None, :], (len(tabs), 8, K_TILE)).astype(jnp.int32)


def _attn_kernel(relb_ref, bucket_ref, q_ref, k_ref, v_ref, o_ref,
                 kp_ref, vp_ref, perm_ref, bias_ref, p_ref, ob_ref, mb_ref, lb_ref, *, seq):
    hp = pl.program_id(0)
    n_br = len(DILATED_BRANCHES)
    n_groups = seq // (Q_TILE * TILE_GROUP)
    assert n_groups % 2 == 0
    d1 = DILATED_BRANCHES[1][1]
    assert [d for _, d in DILATED_BRANCHES] == [1, d1, d1 * d1]
    cls_len = seq // d1
    lane_q = lax.broadcasted_iota(jnp.int32, (Q_TILE, LANES), 1) < HEAD_DIM
    col = lax.broadcasted_iota(jnp.int32, (Q_TILE, K_TILE), 1)

    @pl.when(pl.program_id(1) == 0)
    def _():
        for i in range(n_br):
            bucket = bucket_ref[i]
            for hh in range(2):
                row = jnp.full(bucket.shape, relb_ref[0, 2 * hp + hh] * LOG2E, _F32)
                for b in range(1, N_REL_BUCKETS):
                    row = jnp.where(bucket >= b, relb_ref[b, 2 * hp + hh] * LOG2E, row)
                row = jnp.where(bucket < 0, NEG_INF, row)
                tile = pltpu.roll(jnp.broadcast_to(row[:1], (Q_TILE, K_TILE)), 0, 1, stride=1, stride_axis=0)
                rows = slice(hh * Q_TILE, (hh + 1) * Q_TILE)
                bias_ref[i, 0, rows, :] = jnp.where(col < HALF_WINDOW, NEG_INF, tile)
                bias_ref[i, 1, rows, :] = tile
                bias_ref[i, 2, rows, :] = jnp.where(col >= K_TILE - HALF_WINDOW, NEG_INF, tile)
    zero_pad = jnp.zeros((HALF_WINDOW, LANES), _BF16)
    for i in range(n_br):
        for ref in (kp_ref, vp_ref):
            ref[i, :HALF_WINDOW, :] = zero_pad
            ref[i, HALF_WINDOW + seq:, :] = zero_pad
    for c0 in range(0, seq, COPY_ROWS):
        kp_ref[0, pl.ds(HALF_WINDOW + c0, COPY_ROWS), :] = k_ref[pl.ds(c0, COPY_ROWS), :].astype(_BF16)
        vp_ref[0, pl.ds(HALF_WINDOW + c0, COPY_ROWS), :] = v_ref[pl.ds(c0, COPY_ROWS), :].astype(_BF16)
    for r in range(d1):
        for c0 in range(0, cls_len, COPY_ROWS):
            src = pl.ds(r + d1 * c0, COPY_ROWS, stride=d1)
            dst = pl.ds(r * cls_len + c0, COPY_ROWS)
            padded = pl.ds(HALF_WINDOW + r * cls_len + c0, COPY_ROWS)
            perm_ref[0, dst, :] = q_ref[src, :]
            for a, (x_ref, xp_ref) in enumerate(((k_ref, kp_ref), (v_ref, vp_ref)), start=1):
                x = x_ref[src, :]
                perm_ref[a, dst, :] = x
                xp_ref[1, padded, :] = x.astype(_BF16)
    sub_len = cls_len // d1
    for r2 in range(d1 * d1):
        src = pl.ds((r2 % d1) * cls_len + r2 // d1, sub_len, stride=d1)
        padded = pl.ds(HALF_WINDOW + r2 * sub_len, sub_len)
        kp_ref[2, padded, :] = perm_ref[1, src, :].astype(_BF16)
        vp_ref[2, padded, :] = perm_ref[2, src, :].astype(_BF16)

    def group_tiles(i, jg):
        n_tiles = seq // (DILATED_BRANCHES[i][1] * Q_TILE)
        tiles = []
        for g in range(TILE_GROUP):
            j = jg * TILE_GROUP + g
            n = j % n_tiles
            static = isinstance(j, int)
            if static:
                variant = 0 if n == 0 else (2 if n == n_tiles - 1 else 1)
            else:
                variant = jnp.where(n == 0, 0, jnp.where(n == n_tiles - 1, 2, 1))
            tile_start = j * Q_TILE if static else pl.multiple_of(j * Q_TILE, Q_TILE)
            if i < 2:
                rows = pl.ds(tile_start, Q_TILE)
            else:
                r2 = j // n_tiles
                rows = pl.ds((r2 % d1) * cls_len + r2 // d1 + d1 * Q_TILE * n, Q_TILE, stride=d1)
            tiles.append((rows, pl.ds(tile_start, K_TILE), variant))
        return tiles

    def stage_scores(i, jg, slot):
        tiles = group_tiles(i, jg)
        scores = []
        for rows, win, variant in tiles:
            q = q_ref[rows, :] if i == 0 else perm_ref[0, rows, :]
            q2 = jnp.concatenate([jnp.where(lane_q, q, 0.0), jnp.where(lane_q, 0.0, q)], axis=0)
            s = lax.dot_general(q2.astype(_BF16), kp_ref[i, win, :], (((1,), (1,)), ((), ())),
                                preferred_element_type=_F32)
            scores.append(s + bias_ref[i, variant])
        maxes = [jnp.max(s, axis=-1, keepdims=True) for s in scores]
        for g, ((rows, _, _), s, m2) in enumerate(zip(tiles, scores, maxes)):
            p_ref[slot, g] = jnp.exp2(s - m2).astype(_BF16)
            mb_ref[i, rows, :] = jnp.where(lane_q, m2[:Q_TILE], m2[Q_TILE:])

    ones = jnp.ones((K_TILE, LANES), _BF16)

    def stage_values(i, jg, slot):
        for g, (rows, win, _) in enumerate(group_tiles(i, jg)):
            v1 = jnp.concatenate([vp_ref[i, win, :], ones], axis=1)
            pv = jnp.dot(p_ref[slot, g], v1, preferred_element_type=_F32)
            ob_ref[i, rows, :] = jnp.where(lane_q, pv[:Q_TILE, :LANES], pv[Q_TILE:, :LANES])
            lb_ref[i, rows, :] = jnp.where(lane_q, pv[:Q_TILE, LANES:], pv[Q_TILE:, LANES:])

    stage_scores(0, 0, 0)
    for i in range(n_br):
        for t in range(n_groups):
            if t + 1 < n_groups:
                stage_scores(i, t + 1, (t + 1) % 2)
            elif i + 1 < n_br:
                stage_scores(i + 1, 0, 0)
            stage_values(i, t, t % 2)

    blocks_per_class = cls_len // MERGE_ROWS

    def merge_body(t, carry):
        cls_rows = pl.ds(pl.multiple_of(t * MERGE_ROWS, MERGE_ROWS), MERGE_ROWS)
        nat_rows = pl.ds(t // blocks_per_class + d1 * MERGE_ROWS * (t % blocks_per_class), MERGE_ROWS, stride=d1)
        rows = [nat_rows] + [cls_rows] * (n_br - 1)
        ms = [mb_ref[i, rows[i], :] for i in range(n_br)]
        m = functools.reduce(jnp.maximum, ms)
        num = jnp.zeros((MERGE_ROWS, LANES), _F32)
        den = jnp.zeros((MERGE_ROWS, LANES), _F32)
        for i, m_i in enumerate(ms):
            w = jnp.exp2(m_i - m)
            num = num + w * ob_ref[i, rows[i], :]
            den = den + w * lb_ref[i, rows[i], :]
        perm_ref[1, nat_rows, :] = num / den
        return carry

    lax.fori_loop(0, seq // MERGE_ROWS, merge_body, 0)
    for c0 in range(0, seq, COPY_ROWS):
        o_ref[pl.ds(c0, COPY_ROWS), :] = perm_ref[1, pl.ds(c0, COPY_ROWS), :].astype(o_ref.dtype)


def _attn_call(qkv, rel_bias, *, batch, seq, d_attn):
    n_pairs = d_attn // LANES
    qkv3 = qkv.reshape(batch, seq, 3 * d_attn)
    slab = lambda off: pl.BlockSpec((None, seq, LANES), lambda hp, b: (b, 0, off + hp))
    n_br = len(DILATED_BRANCHES)
    return pl.pallas_call(
        functools.partial(_attn_kernel, seq=seq),
        grid=(n_pairs, batch),
        in_specs=[
            pl.BlockSpec(memory_space=pltpu.SMEM),
            pl.BlockSpec((n_br, 8, K_TILE), lambda hp, b: (0, 0, 0)),
            slab(0), slab(n_pairs), slab(2 * n_pairs),
        ],
        out_specs=pl.BlockSpec((None, seq, LANES), lambda hp, b: (b, 0, hp)),
        out_shape=jax.ShapeDtypeStruct((batch, seq, d_attn), _BF16),
        scratch_shapes=[
            pltpu.VMEM((n_br, seq + 2 * HALF_WINDOW, LANES), _BF16),
            pltpu.VMEM((n_br, seq + 2 * HALF_WINDOW, LANES), _BF16),
            pltpu.VMEM((3, seq, LANES), _F32),
            pltpu.VMEM((n_br, 3, 2 * Q_TILE, K_TILE), _F32),
            pltpu.VMEM((2, TILE_GROUP, 2 * Q_TILE, K_TILE), _BF16),
            pltpu.VMEM((n_br, seq, LANES), _F32),
            pltpu.VMEM((n_br, seq, LANES), _F32),
            pltpu.VMEM((n_br, seq, LANES), _F32),
        ],
        compiler_params=pltpu.CompilerParams(
            dimension_semantics=("arbitrary", "arbitrary"), vmem_limit_bytes=VMEM_LIMIT_BYTES),
        name="dilated_attn",
    )(rel_bias, _bucket_tables(), qkv3, qkv3, qkv3)


def _outproj_ffn_kernel(attn_ref, sgu_ref, x_ref, wo_ref, g1_ref, g2_ref, w1_ref, w2_ref, g3_ref,
                        o_ref, *, d_attn, ff_chunk, n_sub):
    sub = x_ref.shape[0] // n_sub
    rows = [slice(t * sub, (t + 1) * sub) for t in range(n_sub)]
    chunks = range(0, w1_ref.shape[1], ff_chunk)

    def out_proj(t):
        return (jnp.dot(attn_ref[rows[t], :], wo_ref[:d_attn, :], preferred_element_type=_F32)
                + jnp.dot(sgu_ref[rows[t], :], wo_ref[d_attn:, :], preferred_element_type=_F32))

    def mid_norms(t, mix):
        x1 = x_ref[rows[t], :] + _rms_norm(mix, g1_ref[...])
        return x1, _rms_norm(x1, g2_ref[...]).astype(_BF16)

    def ffn_chunk(h, f, c0):
        a = jnp.maximum(jnp.dot(h, w1_ref[:, c0:c0 + ff_chunk], preferred_element_type=_F32), 0.0)
        part = jnp.dot((a * a).astype(_BF16), w2_ref[c0:c0 + ff_chunk, :], preferred_element_type=_F32)
        return part if f is None else f + part

    mix = out_proj(0)
    prev = None
    for t in range(n_sub):
        nxt_mix = out_proj(t + 1) if t + 1 < n_sub else None
        x1, h = mid_norms(t, mix)
        f = None
        for ci, c0 in enumerate(chunks):
            f = ffn_chunk(h, f, c0)
            if ci == 0 and prev is not None:
                pt, px1, pf = prev
                o_ref[rows[pt], :] = px1 + _rms_norm(pf, g3_ref[...])
        prev = (t, x1, f)
        mix = nxt_mix
    pt, px1, pf = prev
    o_ref[rows[pt], :] = px1 + _rms_norm(pf, g3_ref[...])


def _outproj_ffn_call(attn2d, sgu2d, x2d, w_out, g_post_mix, g_pre_ffn, w_ff1, w_ff2, g_post_ffn, *, tm):
    n_tok, d_model = x2d.shape
    d_attn = attn2d.shape[1]
    d_sgu = sgu2d.shape[1]
    d_ff = w_ff1.shape[1]
    const = lambda *shape: pl.BlockSpec(shape, lambda i: (0,) * len(shape), pipeline_mode=pl.Buffered(1))
    return pl.pallas_call(
        functools.partial(_outproj_ffn_kernel, d_attn=d_attn, ff_chunk=1024, n_sub=4),
        grid=(n_tok // tm,),
        in_specs=[
            pl.BlockSpec((tm, d_attn), lambda i: (i, 0)),
            pl.BlockSpec((tm, d_sgu), lambda i: (i, 0)),
            pl.BlockSpec((tm, d_model), lambda i: (i, 0)),
            const(d_attn + d_sgu, d_model),
            const(1, d_model),
            const(1, d_model),
            const(d_model, d_ff),
            const(d_ff, d_model),
            const(1, d_model),
        ],
        out_specs=pl.BlockSpec((tm, d_model), lambda i: (i, 0)),
        out_shape=jax.ShapeDtypeStruct((n_tok, d_model), _F32),
        compiler_params=pltpu.CompilerParams(
            dimension_semantics=("arbitrary",), vmem_limit_bytes=VMEM_LIMIT_BYTES),
        name="outproj_ffn",
    )(attn2d, sgu2d, x2d, w_out, g_post_mix.reshape(1, -1), g_pre_ffn.reshape(1, -1),
      w_ff1, w_ff2, g_post_ffn.reshape(1, -1))


def kernel(x, g_pre_mix, w_in, sgu_ln_g, sgu_ln_b, sgu_w, sgu_b, w_out, g_post_mix, g_pre_ffn,
           w_ff1, w_ff2, g_post_ffn, rel_bias):
    batch, seq, d_model = x.shape
    d_sgu = sgu_ln_g.shape[-1]
    d_attn = w_out.shape[1] - d_sgu
    assert d_attn % LANES == 0 and d_sgu % LANES == 0 and sgu_w.shape[-1] == SGU_CHUNK
    assert all(seq % (dil * Q_TILE) == 0 and win == 2 * HALF_WINDOW * dil for win, dil in DILATED_BRANCHES)
    x2d = x.reshape(batch * seq, d_model)
    for layer in range(g_pre_mix.shape[0]):
        qkv, sgu, w_out_bf, w_ff1_bf, w_ff2_bf = _inproj_call(
            x2d, g_pre_mix[layer], w_in[layer], sgu_ln_g[layer], sgu_ln_b[layer], sgu_w[layer], sgu_b[layer],
            (w_out[layer], w_ff1[layer], w_ff2[layer]), d_attn=d_attn, d_sgu=d_sgu, tm=1024)
        attn = _attn_call(qkv, rel_bias, batch=batch, seq=seq, d_attn=d_attn)
        x2d = _outproj_ffn_call(attn.reshape(batch * seq, d_attn), sgu, x2d, w_out_bf,
                                g_post_mix[layer], g_pre_ffn[layer], w_ff1_bf, w_ff2_bf,
                                g_post_ffn[layer], tm=1024)
    return x2d.reshape(batch, seq, d_model)
```

```python
import functools
import math

import jax
import jax.numpy as jnp
from jax import lax
from jax.experimental import pallas as pl
from jax.experimental.pallas import tpu as pltpu

HEAD_DIM = 64
DILATED_BRANCHES = ((128, 1), (512, 4), (2048, 16))
SGU_CHUNK = 128
N_REL_BUCKETS = 32
REL_MAX_DISTANCE = 1024
RMS_EPS = 1e-6
LN_EPS = 1e-5
NEG_INF = -1e30

LANES = 128
HALF_WINDOW = 64
Q_TILE = 128
K_TILE = Q_TILE + 2 * HALF_WINDOW
TILE_GROUP = 2
MERGE_ROWS = 256
COPY_ROWS = 512
CAST_COLS = 512
LOG2E = math.log2(math.e)
VMEM_LIMIT_BYTES = 56 * 1024 * 1024

_F32 = jnp.float32
_BF16 = jnp.bfloat16


def _rms_norm(x, g):
    return x * lax.rsqrt(jnp.mean(x * x, axis=-1, keepdims=True) + RMS_EPS) * g


def _inproj_kernel(x_ref, g_ref, w32_ref, lng_ref, lnb_ref, wcat_ref, bs_ref, *rest,
                   d_attn, d_sgu, n_sub, n_later):
    later_f32, (qkv_ref, sgu_ref), later_bf16 = rest[:n_later], rest[n_later:n_later + 2], rest[n_later + 2:-1]
    w_ref = rest[-1]

    @pl.when(pl.program_id(0) == 0)
    def _():
        for c0 in range(0, w_ref.shape[1], CAST_COLS):
            w_ref[:, c0:c0 + CAST_COLS] = w32_ref[:, c0:c0 + CAST_COLS].astype(_BF16)

    sub = x_ref.shape[0] // n_sub
    n_qkv = 3 * d_attn
    lane = lax.broadcasted_iota(jnp.int32, (SGU_CHUNK, LANES), 1)
    lo = lane < HEAD_DIM
    pair = 2 * SGU_CHUNK
    assert sub % pair == 0

    def normed(t):
        return _rms_norm(x_ref[t * sub:(t + 1) * sub, :], g_ref[...]).astype(_BF16)

    def gating(t, zg):
        zg = jax.nn.gelu(zg)
        u = zg[:, :d_sgu]
        v = zg[:, d_sgu:]
        mu = jnp.mean(v, axis=-1, keepdims=True)
        vc = v - mu
        vn = vc * lax.rsqrt(jnp.mean(vc * vc, axis=-1, keepdims=True) + LN_EPS) * lng_ref[...] + lnb_ref[...]
        for c0 in range(0, sub, pair):
            for j in range(d_sgu // LANES):
                cols = slice(j * LANES, (j + 1) * LANES)
                stacked = []
                for c in (c0, c0 + SGU_CHUNK):
                    slab = vn[c:c + SGU_CHUNK, cols]
                    stacked.append(jnp.concatenate([jnp.where(lo, slab, 0.0), jnp.where(lo, 0.0, slab)], axis=0))
                rhs = jnp.concatenate(stacked, axis=1).astype(_BF16)
                mixed = jnp.dot(wcat_ref[j], rhs, preferred_element_type=_F32)
                for k, c in enumerate((c0, c0 + SGU_CHUNK)):
                    m = mixed[:, k * LANES:(k + 1) * LANES] + bs_ref[j]
                    rows = slice(t * sub + c, t * sub + c + SGU_CHUNK)
                    sgu_ref[rows, cols] = (u[c:c + SGU_CHUNK, cols] * m).astype(_BF16)

    h = normed(0)
    pending = None
    for t in range(n_sub):
        h_next = normed(t + 1) if t + 1 < n_sub else None
        rows = slice(t * sub, (t + 1) * sub)
        z = jnp.dot(h, w_ref[:, :n_qkv], preferred_element_type=_F32)
        qkv_ref[rows, :d_attn] = z[:, :d_attn] * (HEAD_DIM ** -0.5 * LOG2E)
        qkv_ref[rows, d_attn:] = z[:, d_attn:]
        zg = jnp.dot(h, w_ref[:, n_qkv:], preferred_element_type=_F32)
        if t == 0:
            for src_ref, dst_ref in zip(later_f32, later_bf16):
                dst_ref[...] = src_ref[...].astype(_BF16)
        if pending is not None:
            gating(*pending)
        pending = (t, zg)
        h = h_next
    gating(*pending)


def _inproj_call(x2d, g_pre_mix, w_in, sgu_ln_g, sgu_ln_b, sgu_w, sgu_b, later_weights, *, d_attn, d_sgu, tm):
    n_tok, d_model = x2d.shape
    n_steps = n_tok // tm
    row_block = lambda w: pl.BlockSpec((w.shape[0] // n_steps, w.shape[1]), lambda i: (i, 0))
    assert all(w.shape[0] % (16 * n_steps) == 0 for w in later_weights)
    n_slab = d_sgu // LANES
    n_groups = sgu_w.shape[0]
    wcat = (sgu_w.reshape(n_slab, n_groups // n_slab, SGU_CHUNK, SGU_CHUNK)
            .transpose(0, 2, 1, 3).reshape(n_slab, SGU_CHUNK, 2 * SGU_CHUNK).astype(_BF16))
    bs = jnp.repeat(sgu_b.reshape(n_slab, 2, SGU_CHUNK).transpose(0, 2, 1), HEAD_DIM, axis=2)
    d_in = w_in.shape[1]
    const = lambda *shape: pl.BlockSpec(shape, lambda i: (0,) * len(shape))
    return pl.pallas_call(
        functools.partial(_inproj_kernel, d_attn=d_attn, d_sgu=d_sgu, n_sub=tm // 256,
                          n_later=len(later_weights)),
        grid=(n_steps,),
        in_specs=[
            pl.BlockSpec((tm, d_model), lambda i: (i, 0)),
            const(1, d_model),
            pl.BlockSpec((d_model, d_in), lambda i: (0, 0), pipeline_mode=pl.Buffered(1)),
            const(1, d_sgu),
            const(1, d_sgu),
            const(n_slab, SGU_CHUNK, 2 * SGU_CHUNK),
            const(n_slab, SGU_CHUNK, LANES),
        ] + [row_block(w) for w in later_weights],
        out_specs=[
            pl.BlockSpec((tm, 3 * d_attn), lambda i: (i, 0)),
            pl.BlockSpec((tm, d_sgu), lambda i: (i, 0)),
        ] + [row_block(w) for w in later_weights],
        out_shape=[
            jax.ShapeDtypeStruct((n_tok, 3 * d_attn), _F32),
            jax.ShapeDtypeStruct((n_tok, d_sgu), _BF16),
        ] + [jax.ShapeDtypeStruct(w.shape, _BF16) for w in later_weights],
        scratch_shapes=[pltpu.VMEM((d_model, d_in), _BF16)],
        compiler_params=pltpu.CompilerParams(
            dimension_semantics=("arbitrary",), vmem_limit_bytes=VMEM_LIMIT_BYTES),
        name="inproj_sgu",
    )(x2d, g_pre_mix.reshape(1, -1), w_in, sgu_ln_g.reshape(1, -1),
      sgu_ln_b.reshape(1, -1), wcat, bs, *later_weights)


def _t5_bucket(rel):
    half = N_REL_BUCKETS // 2
    max_exact = half // 2
    ret = jnp.where(rel > 0, half, 0)
    n = jnp.abs(rel)
    nf = jnp.maximum(n, 1).astype(jnp.float32)
    large = max_exact + (jnp.log(nf / max_exact) / math.log(REL_MAX_DISTANCE / max_exact)
                         * (half - max_exact)).astype(jnp.int32)
    large = jnp.minimum(large, half - 1)
    return ret + jnp.where(n < max_exact, n, large)


def _bucket_tables():
    rel = jnp.arange(K_TILE) - HALF_WINDOW
    tabs = [jnp.where(jnp.abs(rel) <= HALF_WINDOW, _t5_bucket(rel * dil), -1)
            for _, dil in DILATED_BRANCHES]
    return jnp.broadcast_to(jnp.stack(tabs)[:, None, :], (len(tabs), 8, K_TILE)).astype(jnp.int32)


def _attn_kernel(relb_ref, bucket_ref, q_ref, k_ref, v_ref, o_ref,
                 kp_ref, vp_ref, perm_ref, bias_ref, p_ref, ob_ref, mb_ref, lb_ref, *, seq):
    hp = pl.program_id(0)
    n_br = len(DILATED_BRANCHES)
    n_groups = seq // (Q_TILE * TILE_GROUP)
    assert n_groups % 2 == 0
    d1 = DILATED_BRANCHES[1][1]
    assert [d for _, d in DILATED_BRANCHES] == [1, d1, d1 * d1]
    cls_len = seq // d1
    lane_q = lax.broadcasted_iota(jnp.int32, (Q_TILE, LANES), 1) < HEAD_DIM
    col = lax.broadcasted_iota(jnp.int32, (Q_TILE, K_TILE), 1)

    @pl.when(pl.program_id(1) == 0)
    def _():
        for i in range(n_br):
            bucket = bucket_ref[i]
            for hh in range(2):
                row = jnp.full(bucket.shape, relb_ref[0, 2 * hp + hh] * LOG2E, _F32)
                for b in range(1, N_REL_BUCKETS):
                    row = jnp.where(bucket >= b, relb_ref[b, 2 * hp + hh] * LOG2E, row)
                row = jnp.where(bucket < 0, NEG_INF, row)
                tile = pltpu.roll(jnp.broadcast_to(row[:1], (Q_TILE, K_TILE)), 0, 1, stride=1, stride_axis=0)
                rows = slice(hh * Q_TILE, (hh + 1) * Q_TILE)
                bias_ref[i, 0, rows, :] = jnp.where(col < HALF_WINDOW, NEG_INF, tile)
                bias_ref[i, 1, rows, :] = tile
                bias_ref[i, 2, rows, :] = jnp.where(col >= K_TILE - HALF_WINDOW, NEG_INF, tile)
    zero_pad = jnp.zeros((HALF_WINDOW, LANES), _BF16)
    for i in range(n_br):
        for ref in (kp_ref, vp_ref):
            ref[i, :HALF_WINDOW, :] = zero_pad
            ref[i, HALF_WINDOW + seq:, :] = zero_pad
    for c0 in range(0, seq, COPY_ROWS):
        kp_ref[0, pl.ds(HALF_WINDOW + c0, COPY_ROWS), :] = k_ref[pl.ds(c0, COPY_ROWS), :].astype(_BF16)
        vp_ref[0, pl.ds(HALF_WINDOW + c0, COPY_ROWS), :] = v_ref[pl.ds(c0, COPY_ROWS), :].astype(_BF16)
    for r in range(d1):
        for c0 in range(0, cls_len, COPY_ROWS):
            src = pl.ds(r + d1 * c0, COPY_ROWS, stride=d1)
            dst = pl.ds(r * cls_len + c0, COPY_ROWS)
            padded = pl.ds(HALF_WINDOW + r * cls_len + c0, COPY_ROWS)
            perm_ref[0, dst, :] = q_ref[src, :]
            for a, (x_ref, xp_ref) in enumerate(((k_ref, kp_ref), (v_ref, vp_ref)), start=1):
                x = x_ref[src, :]
                perm_ref[a, dst, :] = x
                xp_ref[1, padded, :] = x.astype(_BF16)
    sub_len = cls_len // d1
    for r2 in range(d1 * d1):
        src = pl.ds((r2 % d1) * cls_len + r2 // d1, sub_len, stride=d1)
        padded = pl.ds(HALF_WINDOW + r2 * sub_len, sub_len)
        kp_ref[2, padded, :] = perm_ref[1, src, :].astype(_BF16)
        vp_ref[2, padded, :] = perm_ref[2, src, :].astype(_BF16)

    def group_tiles(i, jg):
        n_tiles = seq // (DILATED_BRANCHES[i][1] * Q_TILE)
        tiles = []
        for g in range(TILE_GROUP):
            j = jg * TILE_GROUP + g
            n = j % n_tiles
            static = isinstance(j, int)
            if static:
                variant = 0 if n == 0 else (2 if n == n_tiles - 1 else 1)
            else:
                variant = jnp.where(n == 0, 0, jnp.where(n == n_tiles - 1, 2, 1))
            tile_start = j * Q_TILE if static else pl.multiple_of(j * Q_TILE, Q_TILE)
            if i < 2:
                rows = pl.ds(tile_start, Q_TILE)
            else:
                r2 = j // n_tiles
                rows = pl.ds((r2 % d1) * cls_len + r2 // d1 + d1 * Q_TILE * n, Q_TILE, stride=d1)
            tiles.append((rows, pl.ds(tile_start, K_TILE), variant))
        return tiles

    def stage_scores(i, jg, slot):
        tiles = group_tiles(i, jg)
        scores = []
        for rows, win, variant in tiles:
            q = q_ref[rows, :] if i == 0 else perm_ref[0, rows, :]
            q2 = jnp.concatenate([jnp.where(lane_q, q, 0.0), jnp.where(lane_q, 0.0, q)], axis=0)
            s = lax.dot_general(q2.astype(_BF16), kp_ref[i, win, :], (((1,), (1,)), ((), ())),
                                preferred_element_type=_F32)
            scores.append(s + bias_ref[i, variant])
        maxes = [jnp.max(s, axis=-1, keepdims=True) for s in scores]
        for g, ((rows, _, _), s, m2) in enumerate(zip(tiles, scores, maxes)):
            p_ref[slot, g] = jnp.exp2(s - m2).astype(_BF16)
            mb_ref[i, rows, :] = jnp.where(lane_q, m2[:Q_TILE], m2[Q_TILE:])

    ones = jnp.ones((K_TILE, LANES), _BF16)

    def stage_values(i, jg, slot):
        for g, (rows, win, _) in enumerate(group_tiles(i, jg)):
            v1 = jnp.concatenate([vp_ref[i, win, :], ones], axis=1)
            pv = jnp.dot(p_ref[slot, g], v1, preferred_element_type=_F32)
            ob_ref[i, rows, :] = jnp.where(lane_q, pv[:Q_TILE, :LANES], pv[Q_TILE:, :LANES])
            lb_ref[i, rows, :] = jnp.where(lane_q, pv[:Q_TILE, LANES:], pv[Q_TILE:, LANES:])

    stage_scores(0, 0, 0)
    for i in range(n_br):
        for t in range(n_groups):
            if t + 1 < n_groups:
                stage_scores(i, t + 1, (t + 1) % 2)
            elif i + 1 < n_br:
                stage_scores(i + 1, 0, 0)
            stage_values(i, t, t % 2)

    blocks_per_class = cls_len // MERGE_ROWS

    def merge_body(t, carry):
        cls_rows = pl.ds(pl.multiple_of(t * MERGE_ROWS, MERGE_ROWS), MERGE_ROWS)
        nat_rows = pl.ds(t // blocks_per_class + d1 * MERGE_ROWS * (t % blocks_per_class), MERGE_ROWS, stride=d1)
        rows = [nat_rows] + [cls_rows] * (n_br - 1)
        ms = [mb_ref[i, rows[i], :] for i in range(n_br)]
        m = functools.reduce(jnp.maximum, ms)
        num = jnp.zeros((MERGE_ROWS, LANES), _F32)
        den = jnp.zeros((MERGE_ROWS, LANES), _F32)
        for i, m_i in enumerate(ms):
            w = jnp.exp2(m_i - m)
            num = num + w * ob_ref[i, rows[i], :]
            den = den + w * lb_ref[i, rows[i], :]
        perm_ref[1, nat_rows, :] = num / den
        return carry

    lax.fori_loop(0, seq // MERGE_ROWS, merge_body, 0)
    for c0 in range(0, seq, COPY_ROWS):
        o_ref[pl.ds(c0, COPY_ROWS), :] = perm_ref[1, pl.ds(c0, COPY_ROWS), :].astype(o_ref.dtype)


def _attn_call(qkv, rel_bias, *, batch, seq, d_attn):
    n_pairs = d_attn // LANES
    qkv3 = qkv.reshape(batch, seq, 3 * d_attn)
    slab = lambda off: pl.BlockSpec((None, seq, LANES), lambda hp, b: (b, 0, off + hp))
    n_br = len(DILATED_BRANCHES)
    return pl.pallas_call(
        functools.partial(_attn_kernel, seq=seq),
        grid=(n_pairs, batch),
        in_specs=[
            pl.BlockSpec(memory_space=pltpu.SMEM),
            pl.BlockSpec((n_br, 8, K_TILE), lambda hp, b: (0, 0, 0)),
            slab(0), slab(n_pairs), slab(2 * n_pairs),
        ],
        out_specs=pl.BlockSpec((None, seq, LANES), lambda hp, b: (b, 0, hp)),
        out_shape=jax.ShapeDtypeStruct((batch, seq, d_attn), _BF16),
        scratch_shapes=[
            pltpu.VMEM((n_br, seq + 2 * HALF_WINDOW, LANES), _BF16),
            pltpu.VMEM((n_br, seq + 2 * HALF_WINDOW, LANES), _BF16),
            pltpu.VMEM((3, seq, LANES), _F32),
            pltpu.VMEM((n_br, 3, 2 * Q_TILE, K_TILE), _F32),
            pltpu.VMEM((2, TILE_GROUP, 2 * Q_TILE, K_TILE), _BF16),
            pltpu.VMEM((n_br, seq, LANES), _F32),
            pltpu.VMEM((n_br, seq, LANES), _F32),
            pltpu.VMEM((n_br, seq, LANES), _F32),
        ],
        compiler_params=pltpu.CompilerParams(
            dimension_semantics=("arbitrary", "arbitrary"), vmem_limit_bytes=VMEM_LIMIT_BYTES),
        name="dilated_attn",
    )(rel_bias, _bucket_tables(), qkv3, qkv3, qkv3)


def _outproj_ffn_kernel(attn_ref, sgu_ref, x_ref, wo_ref, g1_ref, g2_ref, w1_ref, w2_ref, g3_ref,
                        o_ref, *, d_attn, ff_chunk, n_sub):
    sub = x_ref.shape[0] // n_sub
    rows = [slice(t * sub, (t + 1) * sub) for t in range(n_sub)]
    chunks = range(0, w1_ref.shape[1], ff_chunk)

    def out_proj(t):
        return (jnp.dot(attn_ref[rows[t], :], wo_ref[:d_attn, :], preferred_element_type=_F32)
                + jnp.dot(sgu_ref[rows[t], :], wo_ref[d_attn:, :], preferred_element_type=_F32))

    def mid_norms(t, mix):
        x1 = x_ref[rows[t], :] + _rms_norm(mix, g1_ref[...])
        return x1, _rms_norm(x1, g2_ref[...]).astype(_BF16)

    def ffn_chunk(h, f, c0):
        a = jnp.maximum(jnp.dot(h, w1_ref[:, c0:c0 + ff_chunk], preferred_element_type=_F32), 0.0)
        part = jnp.dot((a * a).astype(_BF16), w2_ref[c0:c0 + ff_chunk, :], preferred_element_type=_F32)
        return part if f is None else f + part

    mix = out_proj(0)
    prev = None
    for t in range(n_sub):
        nxt_mix = out_proj(t + 1) if t + 1 < n_sub else None
        x1, h = mid_norms(t, mix)
        f = None
        for ci, c0 in enumerate(chunks):
            f = ffn_chunk(h, f, c0)
            if ci == 0 and prev is not None:
                pt, px1, pf = prev
                o_ref[rows[pt], :] = px1 + _rms_norm(pf, g3_ref[...])
        prev = (t, x1, f)
        mix = nxt_mix
    pt, px1, pf = prev
    o_ref[rows[pt], :] = px1 + _rms_norm(pf, g3_ref[...])


def _outproj_ffn_call(attn2d, sgu2d, x2d, w_out, g_post_mix, g_pre_ffn, w_ff1, w_ff2, g_post_ffn, *, tm):
    n_tok, d_model = x2d.shape
    d_attn = attn2d.shape[1]
    d_sgu = sgu2d.shape[1]
    d_ff = w_ff1.shape[1]
    const = lambda *shape: pl.BlockSpec(shape, lambda i: (0,) * len(shape), pipeline_mode=pl.Buffered(1))
    return pl.pallas_call(
        functools.partial(_outproj_ffn_kernel, d_attn=d_attn, ff_chunk=1024, n_sub=4),
        grid=(n_tok // tm,),
        in_specs=[
            pl.BlockSpec((tm, d_attn), lambda i: (i, 0)),
            pl.BlockSpec((tm, d_sgu), lambda i: (i, 0)),
            pl.BlockSpec((tm, d_model), lambda i: (i, 0)),
            const(d_attn + d_sgu, d_model),
            const(1, d_model),
            const(1, d_model),
            const(d_model, d_ff),
            const(d_ff, d_model),
            const(1, d_model),
        ],
        out_specs=pl.BlockSpec((tm, d_model), lambda i: (i, 0)),
        out_shape=jax.ShapeDtypeStruct((n_tok, d_model), _F32),
        compiler_params=pltpu.CompilerParams(
            dimension_semantics=("arbitrary",), vmem_limit_bytes=VMEM_LIMIT_BYTES),
        name="outproj_ffn",
    )(attn2d, sgu2d, x2d, w_out, g_post_mix.reshape(1, -1), g_pre_ffn.reshape(1, -1),
      w_ff1, w_ff2, g_post_ffn.reshape(1, -1))


def kernel(x, g_pre_mix, w_in, sgu_ln_g, sgu_ln_b, sgu_w, sgu_b, w_out, g_post_mix, g_pre_ffn,
           w_ff1, w_ff2, g_post_ffn, rel_bias):
    batch, seq, d_model = x.shape
    d_sgu = sgu_ln_g.shape[-1]
    d_attn = w_out.shape[1] - d_sgu
    assert d_attn % LANES == 0 and d_sgu % LANES == 0 and sgu_w.shape[-1] == SGU_CHUNK
    assert all(seq % (dil * Q_TILE) == 0 and win == 2 * HALF_WINDOW * dil for win, dil in DILATED_BRANCHES)
    x2d = x.reshape(batch * seq, d_model)
    for layer in range(g_pre_mix.shape[0]):
        qkv, sgu, w_out_bf, w_ff1_bf, w_ff2_bf = _inproj_call(
            x2d, g_pre_mix[layer], w_in[layer], sgu_ln_g[layer], sgu_ln_b[layer], sgu_w[layer], sgu_b[layer],
            (w_out[layer], w_ff1[layer], w_ff2[layer]), d_attn=d_attn, d_sgu=d_sgu, tm=1024)
        attn = _attn_call(qkv, rel_bias, batch=batch, seq=seq, d_attn=d_attn)
        x2d = _outproj_ffn_call(attn.reshape(batch * seq, d_attn), sgu, x2d, w_out_bf,
                                g_post_mix[layer], g_pre_ffn[layer], w_ff1_bf, w_ff2_bf,
                                g_post_ffn[layer], tm=1024)
    return x2d.reshape(batch, seq, d_model)
```

```python
import functools
import math

import jax
import jax.numpy as jnp
from jax import lax
from jax.experimental import pallas as pl
from jax.experimental.pallas import tpu as pltpu

HEAD_DIM = 64
DILATED_BRANCHES = ((128, 1), (512, 4), (2048, 16))
SGU_CHUNK = 128
N_REL_BUCKETS = 32
REL_MAX_DISTANCE = 1024
RMS_EPS = 1e-6
LN_EPS = 1e-5
NEG_INF = -1e30

LANES = 128
HALF_WINDOW = 64
Q_TILE = 128
K_TILE = Q_TILE + 2 * HALF_WINDOW
TILE_GROUP = 1
MERGE_ROWS = 256
COPY_ROWS = 512
CAST_COLS = 512
LOG2E = math.log2(math.e)
VMEM_LIMIT_BYTES = 56 * 1024 * 1024

_F32 = jnp.float32
_BF16 = jnp.bfloat16


def _rms_norm(x, g):
    return x * lax.rsqrt(jnp.mean(x * x, axis=-1, keepdims=True) + RMS_EPS) * g


def _inproj_kernel(x_ref, g_ref, w32_ref, lng_ref, lnb_ref, wcat_ref, bs_ref, *rest,
                   d_attn, d_sgu, n_sub, n_later):
    later_f32, (qkv_ref, sgu_ref), later_bf16 = rest[:n_later], rest[n_later:n_later + 2], rest[n_later + 2:-1]
    w_ref = rest[-1]

    @pl.when(pl.program_id(0) == 0)
    def _():
        for c0 in range(0, w_ref.shape[1], CAST_COLS):
            w_ref[:, c0:c0 + CAST_COLS] = w32_ref[:, c0:c0 + CAST_COLS].astype(_BF16)

    sub = x_ref.shape[0] // n_sub
    n_qkv = 3 * d_attn
    lane = lax.broadcasted_iota(jnp.int32, (SGU_CHUNK, LANES), 1)
    lo = lane < HEAD_DIM
    pair = 2 * SGU_CHUNK
    assert sub % pair == 0

    def normed(t):
        return _rms_norm(x_ref[t * sub:(t + 1) * sub, :], g_ref[...]).astype(_BF16)

    def gating(t, zg):
        zg = jax.nn.gelu(zg)
        u = zg[:, :d_sgu]
        v = zg[:, d_sgu:]
        mu = jnp.mean(v, axis=-1, keepdims=True)
        vc = v - mu
        vn = vc * lax.rsqrt(jnp.mean(vc * vc, axis=-1, keepdims=True) + LN_EPS) * lng_ref[...] + lnb_ref[...]
        for c0 in range(0, sub, pair):
            for j in range(d_sgu // LANES):
                cols = slice(j * LANES, (j + 1) * LANES)
                stacked = []
                for c in (c0, c0 + SGU_CHUNK):
                    slab = vn[c:c + SGU_CHUNK, cols]
                    stacked.append(jnp.concatenate([jnp.where(lo, slab, 0.0), jnp.where(lo, 0.0, slab)], axis=0))
                rhs = jnp.concatenate(stacked, axis=1).astype(_BF16)
                mixed = jnp.dot(wcat_ref[j], rhs, preferred_element_type=_F32)
                for k, c in enumerate((c0, c0 + SGU_CHUNK)):
                    m = mixed[:, k * LANES:(k + 1) * LANES] + bs_ref[j]
                    rows = slice(t * sub + c, t * sub + c + SGU_CHUNK)
                    sgu_ref[rows, cols] = (u[c:c + SGU_CHUNK, cols] * m).astype(_BF16)

    h = normed(0)
    pending = None
    for t in range(n_sub):
        h_next = normed(t + 1) if t + 1 < n_sub else None
        rows = slice(t * sub, (t + 1) * sub)
        z = jnp.dot(h, w_ref[:, :n_qkv], preferred_element_type=_F32)
        qkv_ref[rows, :d_attn] = z[:, :d_attn] * (HEAD_DIM ** -0.5 * LOG2E)
        qkv_ref[rows, d_attn:] = z[:, d_attn:]
        zg = jnp.dot(h, w_ref[:, n_qkv:], preferred_element_type=_F32)
        if t == 0:
            for src_ref, dst_ref in zip(later_f32, later_bf16):
                dst_ref[...] = src_ref[...].astype(_BF16)
        if pending is not None:
            gating(*pending)
        pending = (t, zg)
        h = h_next
    gating(*pending)


def _inproj_call(x2d, g_pre_mix, w_in, sgu_ln_g, sgu_ln_b, sgu_w, sgu_b, later_weights, *, d_attn, d_sgu, tm):
    n_tok, d_model = x2d.shape
    n_steps = n_tok // tm
    row_block = lambda w: pl.BlockSpec((w.shape[0] // n_steps, w.shape[1]), lambda i: (i, 0))
    assert all(w.shape[0] % (16 * n_steps) == 0 for w in later_weights)
    n_slab = d_sgu // LANES
    n_groups = sgu_w.shape[0]
    wcat = (sgu_w.reshape(n_slab, n_groups // n_slab, SGU_CHUNK, SGU_CHUNK)
            .transpose(0, 2, 1, 3).reshape(n_slab, SGU_CHUNK, 2 * SGU_CHUNK).astype(_BF16))
    bs = jnp.repeat(sgu_b.reshape(n_slab, 2, SGU_CHUNK).transpose(0, 2, 1), HEAD_DIM, axis=2)
    d_in = w_in.shape[1]
    const = lambda *shape: pl.BlockSpec(shape, lambda i: (0,) * len(shape))
    return pl.pallas_call(
        functools.partial(_inproj_kernel, d_attn=d_attn, d_sgu=d_sgu, n_sub=tm // 256,
                          n_later=len(later_weights)),
        grid=(n_steps,),
        in_specs=[
            pl.BlockSpec((tm, d_model), lambda i: (i, 0)),
            const(1, d_model),
            pl.BlockSpec((d_model, d_in), lambda i: (0, 0), pipeline_mode=pl.Buffered(1)),
            const(1, d_sgu),
            const(1, d_sgu),
            const(n_slab, SGU_CHUNK, 2 * SGU_CHUNK),
            const(n_slab, SGU_CHUNK, LANES),
        ] + [row_block(w) for w in later_weights],
        out_specs=[
            pl.BlockSpec((tm, 3 * d_attn), lambda i: (i, 0)),
            pl.BlockSpec((tm, d_sgu), lambda i: (i, 0)),
        ] + [row_block(w) for w in later_weights],
        out_shape=[
            jax.ShapeDtypeStruct((n_tok, 3 * d_attn), _F32),
            jax.ShapeDtypeStruct((n_tok, d_sgu), _BF16),
        ] + [jax.ShapeDtypeStruct(w.shape, _BF16) for w in later_weights],
        scratch_shapes=[pltpu.VMEM((d_model, d_in), _BF16)],
        compiler_params=pltpu.CompilerParams(
            dimension_semantics=("arbitrary",), vmem_limit_bytes=VMEM_LIMIT_BYTES),
        name="inproj_sgu",
    )(x2d, g_pre_mix.reshape(1, -1), w_in, sgu_ln_g.reshape(1, -1),
      sgu_ln_b.reshape(1, -1), wcat, bs, *later_weights)


def _t5_bucket(rel):
    half = N_REL_BUCKETS // 2
    max_exact = half // 2
    ret = jnp.where(rel > 0, half, 0)
    n = jnp.abs(rel)
    nf = jnp.maximum(n, 1).astype(jnp.float32)
    large = max_exact + (jnp.log(nf / max_exact) / math.log(REL_MAX_DISTANCE / max_exact)
                         * (half - max_exact)).astype(jnp.int32)
    large = jnp.minimum(large, half - 1)
    return ret + jnp.where(n < max_exact, n, large)


def _bucket_tables():
    rel = jnp.arange(K_TILE) - HALF_WINDOW
    tabs = [jnp.where(jnp.abs(rel) <= HALF_WINDOW, _t5_bucket(rel * dil), -1)
            for _, dil in DILATED_BRANCHES]
    return jnp.broadcast_to(jnp.stack(tabs)[:, None, :], (len(tabs), 8, K_TILE)).astype(jnp.int32)


def _attn_kernel(relb_ref, bucket_ref, q_ref, k_ref, v_ref, o_ref,
                 kp_ref, vp_ref, perm_ref, bias_ref, p_ref, ob_ref, mb_ref, lb_ref, *, seq):
    hp = pl.program_id(0)
    n_br = len(DILATED_BRANCHES)
    n_groups = seq // (Q_TILE * TILE_GROUP)
    assert n_groups % 2 == 0
    d1 = DILATED_BRANCHES[1][1]
    assert [d for _, d in DILATED_BRANCHES] == [1, d1, d1 * d1]
    cls_len = seq // d1
    lane_q = lax.broadcasted_iota(jnp.int32, (Q_TILE, LANES), 1) < HEAD_DIM
    col = lax.broadcasted_iota(jnp.int32, (Q_TILE, K_TILE), 1)

    @pl.when(pl.program_id(1) == 0)
    def _():
        for i in range(n_br):
            bucket = bucket_ref[i]
            for hh in range(2):
                row = jnp.full(bucket.shape, relb_ref[0, 2 * hp + hh] * LOG2E, _F32)
                for b in range(1, N_REL_BUCKETS):
                    row = jnp.where(bucket >= b, relb_ref[b, 2 * hp + hh] * LOG2E, row)
                row = jnp.where(bucket < 0, NEG_INF, row)
                tile = pltpu.roll(jnp.broadcast_to(row[:1], (Q_TILE, K_TILE)), 0, 1, stride=1, stride_axis=0)
                rows = slice(hh * Q_TILE, (hh + 1) * Q_TILE)
                bias_ref[i, 0, rows, :] = jnp.where(col < HALF_WINDOW, NEG_INF, tile)
                bias_ref[i, 1, rows, :] = tile
                bias_ref[i, 2, rows, :] = jnp.where(col >= K_TILE - HALF_WINDOW, NEG_INF, tile)
    zero_pad = jnp.zeros((HALF_WINDOW, LANES), _BF16)
    for i in range(n_br):
        for ref in (kp_ref, vp_ref):
            ref[i, :HALF_WINDOW, :] = zero_pad
            ref[i, HALF_WINDOW + seq:, :] = zero_pad
    for c0 in range(0, seq, COPY_ROWS):
        kp_ref[0, pl.ds(HALF_WINDOW + c0, COPY_ROWS), :] = k_ref[pl.ds(c0, COPY_ROWS), :].astype(_BF16)
        vp_ref[0, pl.ds(HALF_WINDOW + c0, COPY_ROWS), :] = v_ref[pl.ds(c0, COPY_ROWS), :].astype(_BF16)
    for r in range(d1):
        for c0 in range(0, cls_len, COPY_ROWS):
            src = pl.ds(r + d1 * c0, COPY_ROWS, stride=d1)
            dst = pl.ds(r * cls_len + c0, COPY_ROWS)
            padded = pl.ds(HALF_WINDOW + r * cls_len + c0, COPY_ROWS)
            perm_ref[0, dst, :] = q_ref[src, :]
            for a, (x_ref, xp_ref) in enumerate(((k_ref, kp_ref), (v_ref, vp_ref)), start=1):
                x = x_ref[src, :]
                perm_ref[a, dst, :] = x
                xp_ref[1, padded, :] = x.astype(_BF16)
    sub_len = cls_len // d1
    for r2 in range(d1 * d1):
        src = pl.ds((r2 % d1) * cls_len + r2 // d1, sub_len, stride=d1)
        padded = pl.ds(HALF_WINDOW + r2 * sub_len, sub_len)
        kp_ref[2, padded, :] = perm_ref[1, src, :].astype(_BF16)
        vp_ref[2, padded, :] = perm_ref[2, src, :].astype(_BF16)

    def group_tiles(i, jg):
        n_tiles = seq // (DILATED_BRANCHES[i][1] * Q_TILE)
        tiles = []
        for g in range(TILE_GROUP):
            j = jg * TILE_GROUP + g
            n = j % n_tiles
            static = isinstance(j, int)
            if static:
                variant = 0 if n == 0 else (2 if n == n_tiles - 1 else 1)
            else:
                variant = jnp.where(n == 0, 0, jnp.where(n == n_tiles - 1, 2, 1))
            tile_start = j * Q_TILE if static else pl.multiple_of(j * Q_TILE, Q_TILE)
            if i < 2:
                rows = pl.ds(tile_start, Q_TILE)
            else:
                r2 = j // n_tiles
                rows = pl.ds((r2 % d1) * cls_len + r2 // d1 + d1 * Q_TILE * n, Q_TILE, stride=d1)
            tiles.append((rows, pl.ds(tile_start, K_TILE), variant))
        return tiles

    def stage_scores(i, jg, slot):
        tiles = group_tiles(i, jg)
        scores = []
        for rows, win, variant in tiles:
            q = q_ref[rows, :] if i == 0 else perm_ref[0, rows, :]
            q2 = jnp.concatenate([jnp.where(lane_q, q, 0.0), jnp.where(lane_q, 0.0, q)], axis=0)
            s = lax.dot_general(q2.astype(_BF16), kp_ref[i, win, :], (((1,), (1,)), ((), ())),
                                preferred_element_type=_F32)
            scores.append(s + bias_ref[i, variant])
        maxes = [jnp.max(s, axis=-1, keepdims=True) for s in scores]
        for g, ((rows, _, _), s, m2) in enumerate(zip(tiles, scores, maxes)):
            p_ref[slot, g] = jnp.exp2(s - m2).astype(_BF16)
            mb_ref[i, rows, :] = jnp.where(lane_q, m2[:Q_TILE], m2[Q_TILE:])

    ones = jnp.ones((K_TILE, LANES), _BF16)

    def stage_values(i, jg, slot):
        for g, (rows, win, _) in enumerate(group_tiles(i, jg)):
            v1 = jnp.concatenate([vp_ref[i, win, :], ones], axis=1)
            pv = jnp.dot(p_ref[slot, g], v1, preferred_element_type=_F32)
            ob_ref[i, rows, :] = jnp.where(lane_q, pv[:Q_TILE, :LANES], pv[Q_TILE:, :LANES])
            lb_ref[i, rows, :] = jnp.where(lane_q, pv[:Q_TILE, LANES:], pv[Q_TILE:, LANES:])

    stage_scores(0, 0, 0)
    for i in range(n_br):
        for t in range(n_groups):
            if t + 1 < n_groups:
                stage_scores(i, t + 1, (t + 1) % 2)
            elif i + 1 < n_br:
                stage_scores(i + 1, 0, 0)
            stage_values(i, t, t % 2)

    blocks_per_class = cls_len // MERGE_ROWS

    def merge_body(t, carry):
        cls_rows = pl.ds(pl.multiple_of(t * MERGE_ROWS, MERGE_ROWS), MERGE_ROWS)
        nat_rows = pl.ds(t // blocks_per_class + d1 * MERGE_ROWS * (t % blocks_per_class), MERGE_ROWS, stride=d1)
        rows = [nat_rows] + [cls_rows] * (n_br - 1)
        ms = [mb_ref[i, rows[i], :] for i in range(n_br)]
        m = functools.reduce(jnp.maximum, ms)
        num = jnp.zeros((MERGE_ROWS, LANES), _F32)
        den = jnp.zeros((MERGE_ROWS, LANES), _F32)
        for i, m_i in enumerate(ms):
            w = jnp.exp2(m_i - m)
            num = num + w * ob_ref[i, rows[i], :]
            den = den + w * lb_ref[i, rows[i], :]
        perm_ref[1, nat_rows, :] = num / den
        return carry

    lax.fori_loop(0, seq // MERGE_ROWS, merge_body, 0)
    for c0 in range(0, seq, COPY_ROWS):
        o_ref[pl.ds(c0, COPY_ROWS), :] = perm_ref[1, pl.ds(c0, COPY_ROWS), :].astype(o_ref.dtype)


def _attn_call(qkv, rel_bias, *, batch, seq, d_attn):
    n_pairs = d_attn // LANES
    qkv3 = qkv.reshape(batch, seq, 3 * d_attn)
    slab = lambda off: pl.BlockSpec((None, seq, LANES), lambda hp, b: (b, 0, off + hp))
    n_br = len(DILATED_BRANCHES)
    return pl.pallas_call(
        functools.partial(_attn_kernel, seq=seq),
        grid=(n_pairs, batch),
        in_specs=[
            pl.BlockSpec(memory_space=pltpu.SMEM),
            pl.BlockSpec((n_br, 8, K_TILE), lambda hp, b: (0, 0, 0)),
            slab(0), slab(n_pairs), slab(2 * n_pairs),
        ],
        out_specs=pl.BlockSpec((None, seq, LANES), lambda hp, b: (b, 0, hp)),
        out_shape=jax.ShapeDtypeStruct((batch, seq, d_attn), _BF16),
        scratch_shapes=[
            pltpu.VMEM((n_br, seq + 2 * HALF_WINDOW, LANES), _BF16),
            pltpu.VMEM((n_br, seq + 2 * HALF_WINDOW, LANES), _BF16),
            pltpu.VMEM((3, seq, LANES), _F32),
            pltpu.VMEM((n_br, 3, 2 * Q_TILE, K_TILE), _F32),
            pltpu.VMEM((2, TILE_GROUP, 2 * Q_TILE, K_TILE), _BF16),
            pltpu.VMEM((n_br, seq, LANES), _F32),
            pltpu.VMEM((n_br, seq, LANES), _F32),
            pltpu.VMEM((n_br, seq, LANES), _F32),
        ],
        compiler_params=pltpu.CompilerParams(
            dimension_semantics=("arbitrary", "arbitrary"), vmem_limit_bytes=VMEM_LIMIT_BYTES),
        name="dilated_attn",
    )(rel_bias, _bucket_tables(), qkv3, qkv3, qkv3)


def _outproj_ffn_kernel(attn_ref, sgu_ref, x_ref, wo_ref, g1_ref, g2_ref, w1_ref, w2_ref, g3_ref,
                        o_ref, *, d_attn, ff_chunk, n_sub):
    sub = x_ref.shape[0] // n_sub
    rows = [slice(t * sub, (t + 1) * sub) for t in range(n_sub)]
    chunks = range(0, w1_ref.shape[1], ff_chunk)

    def out_proj(t):
        return (jnp.dot(attn_ref[rows[t], :], wo_ref[:d_attn, :], preferred_element_type=_F32)
                + jnp.dot(sgu_ref[rows[t], :], wo_ref[d_attn:, :], preferred_element_type=_F32))

    def mid_norms(t, mix):
        x1 = x_ref[rows[t], :] + _rms_norm(mix, g1_ref[...])
        return x1, _rms_norm(x1, g2_ref[...]).astype(_BF16)

    def ffn_chunk(h, f, c0):
        a = jnp.maximum(jnp.dot(h, w1_ref[:, c0:c0 + ff_chunk], preferred_element_type=_F32), 0.0)
        part = jnp.dot((a * a).astype(_BF16), w2_ref[c0:c0 + ff_chunk, :], preferred_element_type=_F32)
        return part if f is None else f + part

    mix = out_proj(0)
    prev = None
    for t in range(n_sub):
        nxt_mix = out_proj(t + 1) if t + 1 < n_sub else None
        x1, h = mid_norms(t, mix)
        f = None
        for ci, c0 in enumerate(chunks):
            f = ffn_chunk(h, f, c0)
            if ci == 0 and prev is not None:
                pt, px1, pf = prev
                o_ref[rows[pt], :] = px1 + _rms_norm(pf, g3_ref[...])
        prev = (t, x1, f)
        mix = nxt_mix
    pt, px1, pf = prev
    o_ref[rows[pt], :] = px1 + _rms_norm(pf, g3_ref[...])


def _outproj_ffn_call(attn2d, sgu2d, x2d, w_out, g_post_mix, g_pre_ffn, w_ff1, w_ff2, g_post_ffn, *, tm):
    n_tok, d_model = x2d.shape
    d_attn = attn2d.shape[1]
    d_sgu = sgu2d.shape[1]
    d_ff = w_ff1.shape[1]
    const = lambda *shape: pl.BlockSpec(shape, lambda i: (0,) * len(shape), pipeline_mode=pl.Buffered(1))
    return pl.pallas_call(
        functools.partial(_outproj_ffn_kernel, d_attn=d_attn, ff_chunk=1024, n_sub=4),
        grid=(n_tok // tm,),
        in_specs=[
            pl.BlockSpec((tm, d_attn), lambda i: (i, 0)),
            pl.BlockSpec((tm, d_sgu), lambda i: (i, 0)),
            pl.BlockSpec((tm, d_model), lambda i: (i, 0)),
            const(d_attn + d_sgu, d_model),
            const(1, d_model),
            const(1, d_model),
            const(d_model, d_ff),
            const(d_ff, d_model),
            const(1, d_model),
        ],
        out_specs=pl.BlockSpec((tm, d_model), lambda i: (i, 0)),
        out_shape=jax.ShapeDtypeStruct((n_tok, d_model), _F32),
        compiler_params=pltpu.CompilerParams(
            dimension_semantics=("arbitrary",), vmem_limit_bytes=VMEM_LIMIT_BYTES),
        name="outproj_ffn",
    )(attn2d, sgu2d, x2d, w_out, g_post_mix.reshape(1, -1), g_pre_ffn.reshape(1, -1),
      w_ff1, w_ff2, g_post_ffn.reshape(1, -1))


def kernel(x, g_pre_mix, w_in, sgu_ln_g, sgu_ln_b, sgu_w, sgu_b, w_out, g_post_mix, g_pre_ffn,
           w_ff1, w_ff2, g_post_ffn, rel_bias):
    batch, seq, d_model = x.shape
    d_sgu = sgu_ln_g.shape[-1]
    d_attn = w_out.shape[1] - d_sgu
    assert d_attn % LANES == 0 and d_sgu % LANES == 0 and sgu_w.shape[-1] == SGU_CHUNK
    assert all(seq % (dil * Q_TILE) == 0 and win == 2 * HALF_WINDOW * dil for win, dil in DILATED_BRANCHES)
    x2d = x.reshape(batch * seq, d_model)
    for layer in range(g_pre_mix.shape[0]):
        qkv, sgu, w_out_bf, w_ff1_bf, w_ff2_bf = _inproj_call(
            x2d, g_pre_mix[layer], w_in[layer], sgu_ln_g[layer], sgu_ln_b[layer], sgu_w[layer], sgu_b[layer],
            (w_out[layer], w_ff1[layer], w_ff2[layer]), d_attn=d_attn, d_sgu=d_sgu, tm=1024)
        attn = _attn_call(qkv, rel_bias, batch=batch, seq=seq, d_attn=d_attn)
        x2d = _outproj_ffn_call(attn.reshape(batch * seq, d_attn), sgu, x2d, w_out_bf,
                                g_post_mix[layer], g_pre_ffn[layer], w_ff1_bf, w_ff2_bf,
                                g_post_ffn[layer], tm=1024)
    return x2d.reshape(batch, seq, d_model)
```

```python
import functools
import math

import jax
import jax.numpy as jnp
from jax import lax
from jax.experimental import pallas as pl
from jax.experimental.pallas import tpu as pltpu

HEAD_DIM = 64
DILATED_BRANCHES = ((128, 1), (512, 4), (2048, 16))
SGU_CHUNK = 128
N_REL_BUCKETS = 32
REL_MAX_DISTANCE = 1024
RMS_EPS = 1e-6
LN_EPS = 1e-5
NEG_INF = -1e30

LANES = 128
HALF_WINDOW = 64
Q_TILE = 128
K_TILE = Q_TILE + 2 * HALF_WINDOW
MERGE_ROWS = 256
COPY_ROWS = 512
CAST_COLS = 512
LOG2E = math.log2(math.e)
V7X_VMEM_BYTES = 64 * 1024 * 1024
VMEM_LIMIT_BYTES = V7X_VMEM_BYTES - 8 * 1024 * 1024

_F32 = jnp.float32
_BF16 = jnp.bfloat16


def _rms_norm(x, g):
    return x * lax.rsqrt(jnp.mean(x * x, axis=-1, keepdims=True) + RMS_EPS) * g


def _qkv_kernel(x_ref, g_ref, w32_ref, *rest, n_qkv, n_sub, n_later):
    later_f32, qkv_ref, wgate_ref = rest[:n_later], rest[n_later], rest[n_later + 1]
    later_bf16, w_ref = rest[n_later + 2:-1], rest[-1]

    @pl.when(pl.program_id(0) == 0)
    def _():
        for c0 in range(0, w32_ref.shape[1], CAST_COLS):
            w = w32_ref[:, c0:c0 + CAST_COLS].astype(_BF16)
            if c0 < n_qkv:
                w_ref[:, c0:c0 + CAST_COLS] = w
            else:
                wgate_ref[:, c0 - n_qkv:c0 - n_qkv + CAST_COLS] = w

    sub = x_ref.shape[0] // n_sub
    d_attn = n_qkv // 3

    def normed(t):
        return _rms_norm(x_ref[t * sub:(t + 1) * sub, :], g_ref[...]).astype(_BF16)

    h = normed(0)
    for t in range(n_sub):
        h_next = normed(t + 1) if t + 1 < n_sub else None
        rows = slice(t * sub, (t + 1) * sub)
        z = jnp.dot(h, w_ref[...], preferred_element_type=_F32)
        qkv_ref[rows, :d_attn] = z[:, :d_attn] * (HEAD_DIM ** -0.5 * LOG2E)
        qkv_ref[rows, d_attn:] = z[:, d_attn:]
        if t == 0:
            for src_ref, dst_ref in zip(later_f32, later_bf16):
                dst_ref[...] = src_ref[...].astype(_BF16)
        h = h_next


def _qkv_call(x2d, g_pre_mix, w_in, later_weights, *, n_qkv, tm):
    n_tok, d_model = x2d.shape
    n_steps = n_tok // tm
    d_in = w_in.shape[1]
    assert n_qkv % CAST_COLS == 0 and d_in % CAST_COLS == 0
    row_block = lambda w: pl.BlockSpec((w.shape[0] // n_steps, w.shape[1]), lambda i: (i, 0))
    assert all(w.shape[0] % (16 * n_steps) == 0 for w in later_weights)
    return pl.pallas_call(
        functools.partial(_qkv_kernel, n_qkv=n_qkv, n_sub=tm // 256, n_later=len(later_weights)),
        grid=(n_steps,),
        in_specs=[
            pl.BlockSpec((tm, d_model), lambda i: (i, 0)),
            pl.BlockSpec((1, d_model), lambda i: (0, 0)),
            pl.BlockSpec((d_model, d_in), lambda i: (0, 0), pipeline_mode=pl.Buffered(1)),
        ] + [row_block(w) for w in later_weights],
        out_specs=[
            pl.BlockSpec((tm, n_qkv), lambda i: (i, 0)),
            pl.BlockSpec((d_model, d_in - n_qkv), lambda i: (0, 0)),
        ] + [row_block(w) for w in later_weights],
        out_shape=[
            jax.ShapeDtypeStruct((n_tok, n_qkv), _F32),
            jax.ShapeDtypeStruct((d_model, d_in - n_qkv), _BF16),
        ] + [jax.ShapeDtypeStruct(w.shape, _BF16) for w in later_weights],
        scratch_shapes=[pltpu.VMEM((d_model, n_qkv), _BF16)],
        compiler_params=pltpu.CompilerParams(
            dimension_semantics=("arbitrary",), vmem_limit_bytes=VMEM_LIMIT_BYTES),
        name="qkv_proj",
    )(x2d, g_pre_mix.reshape(1, -1), w_in, *later_weights)


def _t5_bucket(rel):
    half = N_REL_BUCKETS // 2
    max_exact = half // 2
    ret = jnp.where(rel > 0, half, 0)
    n = jnp.abs(rel)
    nf = jnp.maximum(n, 1).astype(jnp.float32)
    large = max_exact + (jnp.log(nf / max_exact) / math.log(REL_MAX_DISTANCE / max_exact)
                         * (half - max_exact)).astype(jnp.int32)
    large = jnp.minimum(large, half - 1)
    return ret + jnp.where(n < max_exact, n, large)


def _bucket_tables():
    rel = jnp.arange(K_TILE) - HALF_WINDOW
    tabs = [jnp.where(jnp.abs(rel) <= HALF_WINDOW, _t5_bucket(rel * dil), -1)
            for _, dil in DILATED_BRANCHES]
    return jnp.broadcast_to(jnp.stack(tabs)[:, None, :], (len(tabs), 8, K_TILE)).astype(jnp.int32)


def _attn_kernel(relb_ref, bucket_ref, q_ref, k_ref, v_ref, o_ref,
                 kp_ref, vp_ref, perm_ref, bias_ref, p_ref, ob_ref, mb_ref, lb_ref, *, seq):
    hp = pl.program_id(0)
    n_br = len(DILATED_BRANCHES)
    n_tiles_total = seq // Q_TILE
    assert n_tiles_total % 2 == 0
    d1 = DILATED_BRANCHES[1][1]
    assert [d for _, d in DILATED_BRANCHES] == [1, d1, d1 * d1]
    cls_len = seq // d1
    lane_q = lax.broadcasted_iota(jnp.int32, (Q_TILE, LANES), 1) < HEAD_DIM
    col = lax.broadcasted_iota(jnp.int32, (Q_TILE, K_TILE), 1)

    @pl.when(pl.program_id(1) == 0)
    def _():
        for i in range(n_br):
            bucket = bucket_ref[i]
            for hh in range(2):
                row = jnp.full(bucket.shape, relb_ref[0, 2 * hp + hh] * LOG2E, _F32)
                for b in range(1, N_REL_BUCKETS):
                    row = jnp.where(bucket >= b, relb_ref[b, 2 * hp + hh] * LOG2E, row)
                row = jnp.where(bucket < 0, NEG_INF, row)
                tile = pltpu.roll(jnp.broadcast_to(row[:1], (Q_TILE, K_TILE)), 0, 1, stride=1, stride_axis=0)
                rows = slice(hh * Q_TILE, (hh + 1) * Q_TILE)
                bias_ref[i, 0, rows, :] = jnp.where(col < HALF_WINDOW, NEG_INF, tile)
                bias_ref[i, 1, rows, :] = tile
                bias_ref[i, 2, rows, :] = jnp.where(col >= K_TILE - HALF_WINDOW, NEG_INF, tile)

    zero_pad = jnp.zeros((HALF_WINDOW, LANES), _BF16)
    for i in range(n_br):
        for ref in (kp_ref, vp_ref):
            ref[i, :HALF_WINDOW, :] = zero_pad
            ref[i, HALF_WINDOW + seq:, :] = zero_pad
    for c0 in range(0, seq, COPY_ROWS):
        kp_ref[0, pl.ds(HALF_WINDOW + c0, COPY_ROWS), :] = k_ref[pl.ds(c0, COPY_ROWS), :].astype(_BF16)
        vp_ref[0, pl.ds(HALF_WINDOW + c0, COPY_ROWS), :] = v_ref[pl.ds(c0, COPY_ROWS), :].astype(_BF16)
    for r in range(d1):
        for c0 in range(0, cls_len, COPY_ROWS):
            src = pl.ds(r + d1 * c0, COPY_ROWS, stride=d1)
            dst = pl.ds(r * cls_len + c0, COPY_ROWS)
            padded = pl.ds(HALF_WINDOW + r * cls_len + c0, COPY_ROWS)
            perm_ref[0, dst, :] = q_ref[src, :]
            for a, (x_ref, xp_ref) in enumerate(((k_ref, kp_ref), (v_ref, vp_ref)), start=1):
                x = x_ref[src, :]
                perm_ref[a, dst, :] = x
                xp_ref[1, padded, :] = x.astype(_BF16)
    sub_len = cls_len // d1
    for r2 in range(d1 * d1):
        src = pl.ds((r2 % d1) * cls_len + r2 // d1, sub_len, stride=d1)
        padded = pl.ds(HALF_WINDOW + r2 * sub_len, sub_len)
        kp_ref[2, padded, :] = perm_ref[1, src, :].astype(_BF16)
        vp_ref[2, padded, :] = perm_ref[2, src, :].astype(_BF16)

    def tile_geometry(i, j):
        n_tiles = seq // (DILATED_BRANCHES[i][1] * Q_TILE)
        n = j % n_tiles
        variant = 0 if n == 0 else (2 if n == n_tiles - 1 else 1)
        if i < 2:
            rows = pl.ds(j * Q_TILE, Q_TILE)
        else:
            r2 = j // n_tiles
            rows = pl.ds((r2 % d1) * cls_len + r2 // d1 + d1 * Q_TILE * n, Q_TILE, stride=d1)
        return rows, pl.ds(j * Q_TILE, K_TILE), variant

    def stage_scores(i, j, slot):
        rows, win, variant = tile_geometry(i, j)
        q = q_ref[rows, :] if i == 0 else perm_ref[0, rows, :]
        q2 = jnp.concatenate([jnp.where(lane_q, q, 0.0), jnp.where(lane_q, 0.0, q)], axis=0)
        s = lax.dot_general(q2.astype(_BF16), kp_ref[i, win, :], (((1,), (1,)), ((), ())),
                            preferred_element_type=_F32)
        s = s + bias_ref[i, variant]
        m2 = jnp.max(s, axis=-1, keepdims=True)
        p_ref[slot] = jnp.exp2(s - m2).astype(_BF16)
        mb_ref[i, rows, :] = jnp.where(lane_q, m2[:Q_TILE], m2[Q_TILE:])

    ones = jnp.ones((K_TILE, LANES), _BF16)

    def stage_values(i, j, slot):
        rows, win, _ = tile_geometry(i, j)
        v1 = jnp.concatenate([vp_ref[i, win, :], ones], axis=1)
        pv = jnp.dot(p_ref[slot], v1, preferred_element_type=_F32)
        ob_ref[i, rows, :] = jnp.where(lane_q, pv[:Q_TILE, :LANES], pv[Q_TILE:, :LANES])
        lb_ref[i, rows, :] = jnp.where(lane_q, pv[:Q_TILE, LANES:], pv[Q_TILE:, LANES:])

    stage_scores(0, 0, 0)
    for i in range(n_br):
        for j in range(n_tiles_total):
            if j + 1 < n_tiles_total:
                stage_scores(i, j + 1, (j + 1) % 2)
            elif i + 1 < n_br:
                stage_scores(i + 1, 0, 0)
            stage_values(i, j, j % 2)

    blocks_per_class = cls_len // MERGE_ROWS

    def merge_body(t, carry):
        cls_rows = pl.ds(pl.multiple_of(t * MERGE_ROWS, MERGE_ROWS), MERGE_ROWS)
        nat_rows = pl.ds(t // blocks_per_class + d1 * MERGE_ROWS * (t % blocks_per_class), MERGE_ROWS, stride=d1)
        rows = [nat_rows] + [cls_rows] * (n_br - 1)
        ms = [mb_ref[i, rows[i], :] for i in range(n_br)]
        m = functools.reduce(jnp.maximum, ms)
        num = jnp.zeros((MERGE_ROWS, LANES), _F32)
        den = jnp.zeros((MERGE_ROWS, LANES), _F32)
        for i, m_i in enumerate(ms):
            w = jnp.exp2(m_i - m)
            num = num + w * ob_ref[i, rows[i], :]
            den = den + w * lb_ref[i, rows[i], :]
        perm_ref[1, nat_rows, :] = num / den
        return carry

    lax.fori_loop(0, seq // MERGE_ROWS, merge_body, 0)
    for c0 in range(0, seq, COPY_ROWS):
        o_ref[pl.ds(c0, COPY_ROWS), :] = perm_ref[1, pl.ds(c0, COPY_ROWS), :].astype(o_ref.dtype)


def _attn_call(qkv, rel_bias, *, batch, seq, d_attn):
    n_pairs = d_attn // LANES
    qkv3 = qkv.reshape(batch, seq, 3 * d_attn)
    slab = lambda off: pl.BlockSpec((None, seq, LANES), lambda hp, b: (b, 0, off + hp))
    n_br = len(DILATED_BRANCHES)
    return pl.pallas_call(
        functools.partial(_attn_kernel, seq=seq),
        grid=(n_pairs, batch),
        in_specs=[
            pl.BlockSpec(memory_space=pltpu.SMEM),
            pl.BlockSpec((n_br, 8, K_TILE), lambda hp, b: (0, 0, 0)),
            slab(0), slab(n_pairs), slab(2 * n_pairs),
        ],
        out_specs=pl.BlockSpec((None, seq, LANES), lambda hp, b: (b, 0, hp)),
        out_shape=jax.ShapeDtypeStruct((batch, seq, d_attn), _BF16),
        scratch_shapes=[
            pltpu.VMEM((n_br, seq + 2 * HALF_WINDOW, LANES), _BF16),
            pltpu.VMEM((n_br, seq + 2 * HALF_WINDOW, LANES), _BF16),
            pltpu.VMEM((3, seq, LANES), _F32),
            pltpu.VMEM((n_br, 3, 2 * Q_TILE, K_TILE), _F32),
            pltpu.VMEM((2, 2 * Q_TILE, K_TILE), _BF16),
            pltpu.VMEM((n_br, seq, LANES), _F32),
            pltpu.VMEM((n_br, seq, LANES), _F32),
            pltpu.VMEM((n_br, seq, LANES), _F32),
        ],
        compiler_params=pltpu.CompilerParams(
            dimension_semantics=("arbitrary", "arbitrary"), vmem_limit_bytes=VMEM_LIMIT_BYTES),
        name="dilated_attn",
    )(rel_bias, _bucket_tables(), qkv3, qkv3, qkv3)


def _mix_ffn_kernel(attn_ref, x_ref, gpre_ref, wgate_ref, lng_ref, lnb_ref, wcat_ref, bs_ref,
                    wo_ref, g1_ref, g2_ref, w1_ref, w2_ref, g3_ref, o_ref, *, ff_chunk, n_sub):
    sub = x_ref.shape[0] // n_sub
    rows = [slice(t * sub, (t + 1) * sub) for t in range(n_sub)]
    chunks = list(range(0, w1_ref.shape[1], ff_chunk))
    d_attn = attn_ref.shape[1]
    d_sgu = wgate_ref.shape[1] // 2
    lo = lax.broadcasted_iota(jnp.int32, (SGU_CHUNK, LANES), 1) < HEAD_DIM
    pair = 2 * SGU_CHUNK
    assert sub % pair == 0 and len(chunks) >= 3

    def gate_proj(t):
        h = _rms_norm(x_ref[rows[t], :], gpre_ref[...]).astype(_BF16)
        return jnp.dot(h, wgate_ref[...], preferred_element_type=_F32)

    def gating(zg):
        zg = jax.nn.gelu(zg)
        u = zg[:, :d_sgu]
        v = zg[:, d_sgu:]
        mu = jnp.mean(v, axis=-1, keepdims=True)
        vc = v - mu
        vn = vc * lax.rsqrt(jnp.mean(vc * vc, axis=-1, keepdims=True) + LN_EPS) * lng_ref[...] + lnb_ref[...]
        out_rows = []
        for c0 in range(0, sub, pair):
            slabs = [[], []]
            for j in range(d_sgu // LANES):
                cols = slice(j * LANES, (j + 1) * LANES)
                stacked = []
                for c in (c0, c0 + SGU_CHUNK):
                    slab = vn[c:c + SGU_CHUNK, cols]
                    stacked.append(jnp.concatenate([jnp.where(lo, slab, 0.0), jnp.where(lo, 0.0, slab)], axis=0))
                rhs = jnp.concatenate(stacked, axis=1).astype(_BF16)
                mixed = jnp.dot(wcat_ref[j], rhs, preferred_element_type=_F32)
                for k, c in enumerate((c0, c0 + SGU_CHUNK)):
                    m = mixed[:, k * LANES:(k + 1) * LANES] + bs_ref[j]
                    slabs[k].append((u[c:c + SGU_CHUNK, cols] * m).astype(_BF16))
            out_rows += [jnp.concatenate(sl, axis=1) for sl in slabs]
        return jnp.concatenate(out_rows, axis=0)

    def out_proj(t, sgu):
        return (jnp.dot(attn_ref[rows[t], :], wo_ref[:d_attn, :], preferred_element_type=_F32)
                + jnp.dot(sgu, wo_ref[d_attn:, :], preferred_element_type=_F32))

    def mid_norms(t, mix):
        x1 = x_ref[rows[t], :] + _rms_norm(mix, g1_ref[...])
        return x1, _rms_norm(x1, g2_ref[...]).astype(_BF16)

    def ffn_chunk(h, f, c0):
        a = jnp.maximum(jnp.dot(h, w1_ref[:, c0:c0 + ff_chunk], preferred_element_type=_F32), 0.0)
        part = jnp.dot((a * a).astype(_BF16), w2_ref[c0:c0 + ff_chunk, :], preferred_element_type=_F32)
        return part if f is None else f + part

    ahead = 2
    zgs = [gate_proj(t) for t in range(n_sub)]
    mixes = {t: out_proj(t, gating(zgs[t])) for t in range(min(ahead, n_sub))}
    prev = None
    for t in range(n_sub):
        x1, h = mid_norms(t, mixes.pop(t))
        f = None
        for ci, c0 in enumerate(chunks):
            f = ffn_chunk(h, f, c0)
            if ci == 0 and prev is not None:
                pt, px1, pf = prev
                o_ref[rows[pt], :] = px1 + _rms_norm(pf, g3_ref[...])
            elif ci == 1 and t + ahead < n_sub:
                sgu = gating(zgs[t + ahead])
            elif ci == 2 and t + ahead < n_sub:
                mixes[t + ahead] = out_proj(t + ahead, sgu)
        prev = (t, x1, f)
    pt, px1, pf = prev
    o_ref[rows[pt], :] = px1 + _rms_norm(pf, g3_ref[...])


def _mix_ffn_call(attn2d, x2d, g_pre_mix, w_gate, sgu_ln_g, sgu_ln_b, sgu_w, sgu_b, w_out, g_post_mix,
                  g_pre_ffn, w_ff1, w_ff2, g_post_ffn, *, tm):
    n_tok, d_model = x2d.shape
    d_attn = attn2d.shape[1]
    d_sgu = w_gate.shape[1] // 2
    d_ff = w_ff1.shape[1]
    n_slab = d_sgu // LANES
    n_groups = sgu_w.shape[0]
    wcat = (sgu_w.reshape(n_slab, n_groups // n_slab, SGU_CHUNK, SGU_CHUNK)
            .transpose(0, 2, 1, 3).reshape(n_slab, SGU_CHUNK, 2 * SGU_CHUNK).astype(_BF16))
    bs = jnp.repeat(sgu_b.reshape(n_slab, 2, SGU_CHUNK).transpose(0, 2, 1), HEAD_DIM, axis=2)
    const = lambda *shape: pl.BlockSpec(shape, lambda i: (0,) * len(shape), pipeline_mode=pl.Buffered(1))
    return pl.pallas_call(
        functools.partial(_mix_ffn_kernel, ff_chunk=1024, n_sub=4),
        grid=(n_tok // tm,),
        in_specs=[
            pl.BlockSpec((tm, d_attn), lambda i: (i, 0)),
            pl.BlockSpec((tm, d_model), lambda i: (i, 0)),
            const(1, d_model),
            const(d_model, 2 * d_sgu),
            const(1, d_sgu),
            const(1, d_sgu),
            const(n_slab, SGU_CHUNK, 2 * SGU_CHUNK),
            const(n_slab, SGU_CHUNK, LANES),
            const(d_attn + d_sgu, d_model),
            const(1, d_model),
            const(1, d_model),
            const(d_model, d_ff),
            const(d_ff, d_model),
            const(1, d_model),
        ],
        out_specs=pl.BlockSpec((tm, d_model), lambda i: (i, 0)),
        out_shape=jax.ShapeDtypeStruct((n_tok, d_model), _F32),
        compiler_params=pltpu.CompilerParams(
            dimension_semantics=("arbitrary",), vmem_limit_bytes=VMEM_LIMIT_BYTES),
        name="sgu_outproj_ffn",
    )(attn2d, x2d, g_pre_mix.reshape(1, -1), w_gate, sgu_ln_g.reshape(1, -1), sgu_ln_b.reshape(1, -1),
      wcat, bs, w_out, g_post_mix.reshape(1, -1), g_pre_ffn.reshape(1, -1), w_ff1, w_ff2,
      g_post_ffn.reshape(1, -1))


def kernel(x, g_pre_mix, w_in, sgu_ln_g, sgu_ln_b, sgu_w, sgu_b, w_out, g_post_mix, g_pre_ffn,
           w_ff1, w_ff2, g_post_ffn, rel_bias):
    batch, seq, d_model = x.shape
    d_sgu = sgu_ln_g.shape[-1]
    d_attn = w_out.shape[1] - d_sgu
    assert d_attn % LANES == 0 and d_sgu % LANES == 0 and sgu_w.shape[-1] == SGU_CHUNK
    assert all(seq % (dil * Q_TILE) == 0 and win == 2 * HALF_WINDOW * dil for win, dil in DILATED_BRANCHES)
    x2d = x.reshape(batch * seq, d_model)
    for layer in range(g_pre_mix.shape[0]):
        qkv, w_gate_bf, w_out_bf, w_ff1_bf, w_ff2_bf = _qkv_call(
            x2d, g_pre_mix[layer], w_in[layer], (w_out[layer], w_ff1[layer], w_ff2[layer]),
            n_qkv=3 * d_attn, tm=1024)
        attn = _attn_call(qkv, rel_bias, batch=batch, seq=seq, d_attn=d_attn)
        x2d = _mix_ffn_call(attn.reshape(batch * seq, d_attn), x2d, g_pre_mix[layer], w_gate_bf,
                            sgu_ln_g[layer], sgu_ln_b[layer], sgu_w[layer], sgu_b[layer], w_out_bf,
                            g_post_mix[layer], g_pre_ffn[layer], w_ff1_bf, w_ff2_bf, g_post_ffn[layer],
                            tm=1024)
    return x2d.reshape(batch, seq, d_model)
```

```python
import functools
import math

import jax
import jax.numpy as jnp
from jax import lax
from jax.experimental import pallas as pl
from jax.experimental.pallas import tpu as pltpu

HEAD_DIM = 64
DILATED_BRANCHES = ((128, 1), (512, 4), (2048, 16))
SGU_CHUNK = 128
N_REL_BUCKETS = 32
REL_MAX_DISTANCE = 1024
RMS_EPS = 1e-6
LN_EPS = 1e-5
NEG_INF = -1e30

LANES = 128
HALF_WINDOW = 64
Q_TILE = 128
K_TILE = Q_TILE + 2 * HALF_WINDOW
MERGE_ROWS = 256
COPY_ROWS = 512
CAST_COLS = 512
LOG2E = math.log2(math.e)
V7X_VMEM_BYTES = 64 * 1024 * 1024
VMEM_LIMIT_BYTES = V7X_VMEM_BYTES - 8 * 1024 * 1024

_F32 = jnp.float32
_BF16 = jnp.bfloat16


def _rms_norm(x, g):
    return x * lax.rsqrt(jnp.mean(x * x, axis=-1, keepdims=True) + RMS_EPS) * g


def _inproj_kernel(x_ref, g_ref, w32_ref, lng_ref, lnb_ref, wcat_ref, bs_ref, *rest,
                   d_attn, d_sgu, n_sub, n_later):
    later_f32, (qkv_ref, sgu_ref), later_bf16 = rest[:n_later], rest[n_later:n_later + 2], rest[n_later + 2:-1]
    w_ref = rest[-1]

    @pl.when(pl.program_id(0) == 0)
    def _():
        for c0 in range(0, w_ref.shape[1], CAST_COLS):
            w_ref[:, c0:c0 + CAST_COLS] = w32_ref[:, c0:c0 + CAST_COLS].astype(_BF16)

    sub = x_ref.shape[0] // n_sub
    n_qkv = 3 * d_attn
    lane = lax.broadcasted_iota(jnp.int32, (SGU_CHUNK, LANES), 1)
    lo = lane < HEAD_DIM
    pair = 2 * SGU_CHUNK
    assert sub % pair == 0

    def normed(t):
        return _rms_norm(x_ref[t * sub:(t + 1) * sub, :], g_ref[...]).astype(_BF16)

    def gating(t, zg):
        zg = jax.nn.gelu(zg)
        u = zg[:, :d_sgu]
        v = zg[:, d_sgu:]
        mu = jnp.mean(v, axis=-1, keepdims=True)
        vc = v - mu
        vn = vc * lax.rsqrt(jnp.mean(vc * vc, axis=-1, keepdims=True) + LN_EPS) * lng_ref[...] + lnb_ref[...]
        for c0 in range(0, sub, pair):
            for j in range(d_sgu // LANES):
                cols = slice(j * LANES, (j + 1) * LANES)
                stacked = []
                for c in (c0, c0 + SGU_CHUNK):
                    slab = vn[c:c + SGU_CHUNK, cols]
                    stacked.append(jnp.concatenate([jnp.where(lo, slab, 0.0), jnp.where(lo, 0.0, slab)], axis=0))
                rhs = jnp.concatenate(stacked, axis=1).astype(_BF16)
                mixed = jnp.dot(wcat_ref[j], rhs, preferred_element_type=_F32)
                for k, c in enumerate((c0, c0 + SGU_CHUNK)):
                    m = mixed[:, k * LANES:(k + 1) * LANES] + bs_ref[j]
                    rows = slice(t * sub + c, t * sub + c + SGU_CHUNK)
                    sgu_ref[rows, cols] = (u[c:c + SGU_CHUNK, cols] * m).astype(_BF16)

    h = normed(0)
    pending = None
    for t in range(n_sub):
        h_next = normed(t + 1) if t + 1 < n_sub else None
        rows = slice(t * sub, (t + 1) * sub)
        z = jnp.dot(h, w_ref[:, :n_qkv], preferred_element_type=_F32)
        qkv_ref[rows, :d_attn] = z[:, :d_attn] * (HEAD_DIM ** -0.5 * LOG2E)
        qkv_ref[rows, d_attn:] = z[:, d_attn:]
        zg = jnp.dot(h, w_ref[:, n_qkv:], preferred_element_type=_F32)
        if t == 0:
            for src_ref, dst_ref in zip(later_f32, later_bf16):
                dst_ref[...] = src_ref[...].astype(_BF16)
        if pending is not None:
            gating(*pending)
        pending = (t, zg)
        h = h_next
    gating(*pending)


def _inproj_call(x2d, g_pre_mix, w_in, sgu_ln_g, sgu_ln_b, sgu_w, sgu_b, later_weights, *, d_attn, d_sgu, tm):
    n_tok, d_model = x2d.shape
    n_steps = n_tok // tm
    row_block = lambda w: pl.BlockSpec((w.shape[0] // n_steps, w.shape[1]), lambda i: (i, 0))
    assert all(w.shape[0] % (16 * n_steps) == 0 for w in later_weights)
    n_slab = d_sgu // LANES
    n_groups = sgu_w.shape[0]
    wcat = (sgu_w.reshape(n_slab, n_groups // n_slab, SGU_CHUNK, SGU_CHUNK)
            .transpose(0, 2, 1, 3).reshape(n_slab, SGU_CHUNK, 2 * SGU_CHUNK).astype(_BF16))
    bs = jnp.repeat(sgu_b.reshape(n_slab, 2, SGU_CHUNK).transpose(0, 2, 1), HEAD_DIM, axis=2)
    d_in = w_in.shape[1]
    const = lambda *shape: pl.BlockSpec(shape, lambda i: (0,) * len(shape))
    return pl.pallas_call(
        functools.partial(_inproj_kernel, d_attn=d_attn, d_sgu=d_sgu, n_sub=tm // 256,
                          n_later=len(later_weights)),
        grid=(n_steps,),
        in_specs=[
            pl.BlockSpec((tm, d_model), lambda i: (i, 0)),
            const(1, d_model),
            pl.BlockSpec((d_model, d_in), lambda i: (0, 0), pipeline_mode=pl.Buffered(1)),
            const(1, d_sgu),
            const(1, d_sgu),
            const(n_slab, SGU_CHUNK, 2 * SGU_CHUNK),
            const(n_slab, SGU_CHUNK, LANES),
        ] + [row_block(w) for w in later_weights],
        out_specs=[
            pl.BlockSpec((tm, 3 * d_attn), lambda i: (i, 0)),
            pl.BlockSpec((tm, d_sgu), lambda i: (i, 0)),
        ] + [row_block(w) for w in later_weights],
        out_shape=[
            jax.ShapeDtypeStruct((n_tok, 3 * d_attn), _F32),
            jax.ShapeDtypeStruct((n_tok, d_sgu), _BF16),
        ] + [jax.ShapeDtypeStruct(w.shape, _BF16) for w in later_weights],
        scratch_shapes=[pltpu.VMEM((d_model, d_in), _BF16)],
        compiler_params=pltpu.CompilerParams(
            dimension_semantics=("arbitrary",), vmem_limit_bytes=VMEM_LIMIT_BYTES),
        name="inproj_sgu",
    )(x2d, g_pre_mix.reshape(1, -1), w_in, sgu_ln_g.reshape(1, -1),
      sgu_ln_b.reshape(1, -1), wcat, bs, *later_weights)


def _t5_bucket(rel):
    half = N_REL_BUCKETS // 2
    max_exact = half // 2
    ret = jnp.where(rel > 0, half, 0)
    n = jnp.abs(rel)
    nf = jnp.maximum(n, 1).astype(jnp.float32)
    large = max_exact + (jnp.log(nf / max_exact) / math.log(REL_MAX_DISTANCE / max_exact)
                         * (half - max_exact)).astype(jnp.int32)
    large = jnp.minimum(large, half - 1)
    return ret + jnp.where(n < max_exact, n, large)


def _bucket_tables():
    rel = jnp.arange(K_TILE) - HALF_WINDOW
    tabs = [jnp.where(jnp.abs(rel) <= HALF_WINDOW, _t5_bucket(rel * dil), -1)
            for _, dil in DILATED_BRANCHES]
    return jnp.broadcast_to(jnp.stack(tabs)[:, None, :], (len(tabs), 8, K_TILE)).astype(jnp.int32)


def _attn_kernel(relb_ref, bucket_ref, q_ref, k_ref, v_ref, o_ref,
                 kp_ref, vp_ref, perm_ref, bias_ref, p_ref, ob_ref, mb_ref, lb_ref, *, seq):
    hp = pl.program_id(0)
    n_br = len(DILATED_BRANCHES)
    n_tiles_total = seq // Q_TILE
    assert n_tiles_total % 2 == 0
    d1 = DILATED_BRANCHES[1][1]
    assert [d for _, d in DILATED_BRANCHES] == [1, d1, d1 * d1]
    cls_len = seq // d1
    lane_q = lax.broadcasted_iota(jnp.int32, (Q_TILE, LANES), 1) < HEAD_DIM
    col = lax.broadcasted_iota(jnp.int32, (Q_TILE, K_TILE), 1)

    @pl.when(pl.program_id(1) == 0)
    def _():
        for i in range(n_br):
            bucket = bucket_ref[i]
            for hh in range(2):
                row = jnp.full(bucket.shape, relb_ref[0, 2 * hp + hh] * LOG2E, _F32)
                for b in range(1, N_REL_BUCKETS):
                    row = jnp.where(bucket >= b, relb_ref[b, 2 * hp + hh] * LOG2E, row)
                row = jnp.where(bucket < 0, NEG_INF, row)
                tile = pltpu.roll(jnp.broadcast_to(row[:1], (Q_TILE, K_TILE)), 0, 1, stride=1, stride_axis=0)
                rows = slice(hh * Q_TILE, (hh + 1) * Q_TILE)
                bias_ref[i, 0, rows, :] = jnp.where(col < HALF_WINDOW, NEG_INF, tile)
                bias_ref[i, 1, rows, :] = tile
                bias_ref[i, 2, rows, :] = jnp.where(col >= K_TILE - HALF_WINDOW, NEG_INF, tile)

    zero_pad = jnp.zeros((HALF_WINDOW, LANES), _BF16)
    for i in range(n_br):
        for ref in (kp_ref, vp_ref):
            ref[i, :HALF_WINDOW, :] = zero_pad
            ref[i, HALF_WINDOW + seq:, :] = zero_pad
    for c0 in range(0, seq, COPY_ROWS):
        kp_ref[0, pl.ds(HALF_WINDOW + c0, COPY_ROWS), :] = k_ref[pl.ds(c0, COPY_ROWS), :].astype(_BF16)
        vp_ref[0, pl.ds(HALF_WINDOW + c0, COPY_ROWS), :] = v_ref[pl.ds(c0, COPY_ROWS), :].astype(_BF16)
    for r in range(d1):
        for c0 in range(0, cls_len, COPY_ROWS):
            src = pl.ds(r + d1 * c0, COPY_ROWS, stride=d1)
            dst = pl.ds(r * cls_len + c0, COPY_ROWS)
            padded = pl.ds(HALF_WINDOW + r * cls_len + c0, COPY_ROWS)
            perm_ref[0, dst, :] = q_ref[src, :]
            for a, (x_ref, xp_ref) in enumerate(((k_ref, kp_ref), (v_ref, vp_ref)), start=1):
                x = x_ref[src, :]
                perm_ref[a, dst, :] = x
                xp_ref[1, padded, :] = x.astype(_BF16)
    sub_len = cls_len // d1
    for r2 in range(d1 * d1):
        src = pl.ds((r2 % d1) * cls_len + r2 // d1, sub_len, stride=d1)
        padded = pl.ds(HALF_WINDOW + r2 * sub_len, sub_len)
        kp_ref[2, padded, :] = perm_ref[1, src, :].astype(_BF16)
        vp_ref[2, padded, :] = perm_ref[2, src, :].astype(_BF16)

    def tile_geometry(i, j):
        n_tiles = seq // (DILATED_BRANCHES[i][1] * Q_TILE)
        n = j % n_tiles
        variant = 0 if n == 0 else (2 if n == n_tiles - 1 else 1)
        if i < 2:
            rows = pl.ds(j * Q_TILE, Q_TILE)
        else:
            r2 = j // n_tiles
            rows = pl.ds((r2 % d1) * cls_len + r2 // d1 + d1 * Q_TILE * n, Q_TILE, stride=d1)
        return rows, pl.ds(j * Q_TILE, K_TILE), variant

    def stage_scores(i, j, slot):
        rows, win, variant = tile_geometry(i, j)
        q = q_ref[rows, :] if i == 0 else perm_ref[0, rows, :]
        q2 = jnp.concatenate([jnp.where(lane_q, q, 0.0), jnp.where(lane_q, 0.0, q)], axis=0)
        s = lax.dot_general(q2.astype(_BF16), kp_ref[i, win, :], (((1,), (1,)), ((), ())),
                            preferred_element_type=_F32)
        s = s + bias_ref[i, variant]
        m2 = jnp.max(s, axis=-1, keepdims=True)
        p_ref[slot] = jnp.exp2(s - m2).astype(_BF16)
        mb_ref[i, rows, :] = jnp.where(lane_q, m2[:Q_TILE], m2[Q_TILE:])

    ones = jnp.ones((K_TILE, LANES), _BF16)

    def stage_values(i, j, slot):
        rows, win, _ = tile_geometry(i, j)
        v1 = jnp.concatenate([vp_ref[i, win, :], ones], axis=1)
        pv = jnp.dot(p_ref[slot], v1, preferred_element_type=_F32)
        ob_ref[i, rows, :] = jnp.where(lane_q, pv[:Q_TILE, :LANES], pv[Q_TILE:, :LANES])
        lb_ref[i, rows, :] = jnp.where(lane_q, pv[:Q_TILE, LANES:], pv[Q_TILE:, LANES:])

    stage_scores(0, 0, 0)
    for i in range(n_br):
        for j in range(n_tiles_total):
            if j + 1 < n_tiles_total:
                stage_scores(i, j + 1, (j + 1) % 2)
            elif i + 1 < n_br:
                stage_scores(i + 1, 0, 0)
            stage_values(i, j, j % 2)

    blocks_per_class = cls_len // MERGE_ROWS

    def merge_body(t, carry):
        cls_rows = pl.ds(pl.multiple_of(t * MERGE_ROWS, MERGE_ROWS), MERGE_ROWS)
        nat_rows = pl.ds(t // blocks_per_class + d1 * MERGE_ROWS * (t % blocks_per_class), MERGE_ROWS, stride=d1)
        rows = [nat_rows] + [cls_rows] * (n_br - 1)
        ms = [mb_ref[i, rows[i], :] for i in range(n_br)]
        m = functools.reduce(jnp.maximum, ms)
        ws = [jnp.exp2(m_i - m) for m_i in ms]
        num = functools.reduce(jnp.add, [w * ob_ref[i, rows[i], :] for i, w in enumerate(ws)])
        den = functools.reduce(jnp.add, [w * lb_ref[i, rows[i], :] for i, w in enumerate(ws)])
        perm_ref[1, nat_rows, :] = num / den
        return carry

    lax.fori_loop(0, seq // MERGE_ROWS, merge_body, 0, unroll=2)
    for c0 in range(0, seq, COPY_ROWS):
        o_ref[pl.ds(c0, COPY_ROWS), :] = perm_ref[1, pl.ds(c0, COPY_ROWS), :].astype(o_ref.dtype)


def _attn_call(qkv, rel_bias, *, batch, seq, d_attn):
    n_pairs = d_attn // LANES
    qkv3 = qkv.reshape(batch, seq, 3 * d_attn)
    slab = lambda off: pl.BlockSpec((None, seq, LANES), lambda hp, b: (b, 0, off + hp))
    n_br = len(DILATED_BRANCHES)
    return pl.pallas_call(
        functools.partial(_attn_kernel, seq=seq),
        grid=(n_pairs, batch),
        in_specs=[
            pl.BlockSpec(memory_space=pltpu.SMEM),
            pl.BlockSpec((n_br, 8, K_TILE), lambda hp, b: (0, 0, 0)),
            slab(0), slab(n_pairs), slab(2 * n_pairs),
        ],
        out_specs=pl.BlockSpec((None, seq, LANES), lambda hp, b: (b, 0, hp)),
        out_shape=jax.ShapeDtypeStruct((batch, seq, d_attn), _BF16),
        scratch_shapes=[
            pltpu.VMEM((n_br, seq + 2 * HALF_WINDOW, LANES), _BF16),
            pltpu.VMEM((n_br, seq + 2 * HALF_WINDOW, LANES), _BF16),
            pltpu.VMEM((3, seq, LANES), _F32),
            pltpu.VMEM((n_br, 3, 2 * Q_TILE, K_TILE), _F32),
            pltpu.VMEM((2, 2 * Q_TILE, K_TILE), _BF16),
            pltpu.VMEM((n_br, seq, LANES), _F32),
            pltpu.VMEM((n_br, seq, LANES), _F32),
            pltpu.VMEM((n_br, seq, LANES), _F32),
        ],
        compiler_params=pltpu.CompilerParams(
            dimension_semantics=("arbitrary", "arbitrary"), vmem_limit_bytes=VMEM_LIMIT_BYTES),
        name="dilated_attn",
    )(rel_bias, _bucket_tables(), qkv3, qkv3, qkv3)


def _outproj_ffn_kernel(attn_ref, sgu_ref, x_ref, wo_ref, g1_ref, g2_ref, w1_ref, w2_ref, g3_ref,
                        o_ref, *, d_attn, ff_chunk, n_sub):
    sub = x_ref.shape[0] // n_sub
    rows = [slice(t * sub, (t + 1) * sub) for t in range(n_sub)]
    chunks = range(0, w1_ref.shape[1], ff_chunk)

    def out_proj(t):
        return (jnp.dot(attn_ref[rows[t], :], wo_ref[:d_attn, :], preferred_element_type=_F32)
                + jnp.dot(sgu_ref[rows[t], :], wo_ref[d_attn:, :], preferred_element_type=_F32))

    def mid_norms(t, mix):
        x1 = x_ref[rows[t], :] + _rms_norm(mix, g1_ref[...])
        return x1, _rms_norm(x1, g2_ref[...]).astype(_BF16)

    def ffn_chunk(h, f, c0):
        a = jnp.maximum(jnp.dot(h, w1_ref[:, c0:c0 + ff_chunk], preferred_element_type=_F32), 0.0)
        part = jnp.dot((a * a).astype(_BF16), w2_ref[c0:c0 + ff_chunk, :], preferred_element_type=_F32)
        return part if f is None else f + part

    mix = out_proj(0)
    prev = None
    for t in range(n_sub):
        nxt_mix = out_proj(t + 1) if t + 1 < n_sub else None
        x1, h = mid_norms(t, mix)
        f = None
        for ci, c0 in enumerate(chunks):
            f = ffn_chunk(h, f, c0)
            if ci == 0 and prev is not None:
                pt, px1, pf = prev
                o_ref[rows[pt], :] = px1 + _rms_norm(pf, g3_ref[...])
        prev = (t, x1, f)
        mix = nxt_mix
    pt, px1, pf = prev
    o_ref[rows[pt], :] = px1 + _rms_norm(pf, g3_ref[...])


def _outproj_ffn_call(attn2d, sgu2d, x2d, w_out, g_post_mix, g_pre_ffn, w_ff1, w_ff2, g_post_ffn, *, tm):
    n_tok, d_model = x2d.shape
    d_attn = attn2d.shape[1]
    d_sgu = sgu2d.shape[1]
    d_ff = w_ff1.shape[1]
    const = lambda *shape: pl.BlockSpec(shape, lambda i: (0,) * len(shape), pipeline_mode=pl.Buffered(1))
    return pl.pallas_call(
        functools.partial(_outproj_ffn_kernel, d_attn=d_attn, ff_chunk=1024, n_sub=4),
        grid=(n_tok // tm,),
        in_specs=[
            pl.BlockSpec((tm, d_attn), lambda i: (i, 0)),
            pl.BlockSpec((tm, d_sgu), lambda i: (i, 0)),
            pl.BlockSpec((tm, d_model), lambda i: (i, 0)),
            const(d_attn + d_sgu, d_model),
            const(1, d_model),
            const(1, d_model),
            const(d_model, d_ff),
            const(d_ff, d_model),
            const(1, d_model),
        ],
        out_specs=pl.BlockSpec((tm, d_model), lambda i: (i, 0)),
        out_shape=jax.ShapeDtypeStruct((n_tok, d_model), _F32),
        compiler_params=pltpu.CompilerParams(
            dimension_semantics=("arbitrary",), vmem_limit_bytes=VMEM_LIMIT_BYTES),
        name="outproj_ffn",
    )(attn2d, sgu2d, x2d, w_out, g_post_mix.reshape(1, -1), g_pre_ffn.reshape(1, -1),
      w_ff1, w_ff2, g_post_ffn.reshape(1, -1))


def kernel(x, g_pre_mix, w_in, sgu_ln_g, sgu_ln_b, sgu_w, sgu_b, w_out, g_post_mix, g_pre_ffn,
           w_ff1, w_ff2, g_post_ffn, rel_bias):
    batch, seq, d_model = x.shape
    d_sgu = sgu_ln_g.shape[-1]
    d_attn = w_out.shape[1] - d_sgu
    assert d_attn % LANES == 0 and d_sgu % LANES == 0 and sgu_w.shape[-1] == SGU_CHUNK
    assert all(seq % (dil * Q_TILE) == 0 and win == 2 * HALF_WINDOW * dil for win, dil in DILATED_BRANCHES)
    x2d = x.reshape(batch * seq, d_model)
    for layer in range(g_pre_mix.shape[0]):
        qkv, sgu, w_out_bf, w_ff1_bf, w_ff2_bf = _inproj_call(
            x2d, g_pre_mix[layer], w_in[layer], sgu_ln_g[layer], sgu_ln_b[layer], sgu_w[layer], sgu_b[layer],
            (w_out[layer], w_ff1[layer], w_ff2[layer]), d_attn=d_attn, d_sgu=d_sgu, tm=1024)
        attn = _attn_call(qkv, rel_bias, batch=batch, seq=seq, d_attn=d_attn)
        x2d = _outproj_ffn_call(attn.reshape(batch * seq, d_attn), sgu, x2d, w_out_bf,
                                g_post_mix[layer], g_pre_ffn[layer], w_ff1_bf, w_ff2_bf,
                                g_post_ffn[layer], tm=1024)
    return x2d.reshape(batch, seq, d_model)
```

```python
import functools
import math

import jax
import jax.numpy as jnp
from jax import lax
from jax.experimental import pallas as pl
from jax.experimental.pallas import tpu as pltpu

HEAD_DIM = 64
DILATED_BRANCHES = ((128, 1), (512, 4), (2048, 16))
SGU_CHUNK = 128
N_REL_BUCKETS = 32
REL_MAX_DISTANCE = 1024
RMS_EPS = 1e-6
LN_EPS = 1e-5
NEG_INF = -1e30

LANES = 128
HALF_WINDOW = 64
Q_TILE = 128
K_TILE = Q_TILE + 2 * HALF_WINDOW
MERGE_ROWS = 256
COPY_ROWS = 512
CAST_COLS = 512
LOG2E = math.log2(math.e)
V7X_VMEM_BYTES = 64 * 1024 * 1024
VMEM_LIMIT_BYTES = V7X_VMEM_BYTES - 8 * 1024 * 1024

_F32 = jnp.float32
_BF16 = jnp.bfloat16


def _rms_norm(x, g):
    return x * lax.rsqrt(jnp.mean(x * x, axis=-1, keepdims=True) + RMS_EPS) * g


def _inproj_kernel(x_ref, g_ref, w32_ref, lng_ref, lnb_ref, wcat_ref, bs_ref, qkv_ref, sgu_ref, w_ref,
                   *, d_attn, d_sgu, n_sub):

    @pl.when(pl.program_id(0) == 0)
    def _():
        for c0 in range(0, w_ref.shape[1], CAST_COLS):
            w_ref[:, c0:c0 + CAST_COLS] = w32_ref[:, c0:c0 + CAST_COLS].astype(_BF16)

    sub = x_ref.shape[0] // n_sub
    n_qkv = 3 * d_attn
    lane = lax.broadcasted_iota(jnp.int32, (SGU_CHUNK, LANES), 1)
    lo = lane < HEAD_DIM
    pair = 2 * SGU_CHUNK
    assert sub % pair == 0

    def normed(t):
        return _rms_norm(x_ref[t * sub:(t + 1) * sub, :], g_ref[...]).astype(_BF16)

    def gating(t, zg):
        zg = jax.nn.gelu(zg)
        u = zg[:, :d_sgu]
        v = zg[:, d_sgu:]
        mu = jnp.mean(v, axis=-1, keepdims=True)
        vc = v - mu
        vn = vc * lax.rsqrt(jnp.mean(vc * vc, axis=-1, keepdims=True) + LN_EPS) * lng_ref[...] + lnb_ref[...]
        for c0 in range(0, sub, pair):
            for j in range(d_sgu // LANES):
                cols = slice(j * LANES, (j + 1) * LANES)
                stacked = []
                for c in (c0, c0 + SGU_CHUNK):
                    slab = vn[c:c + SGU_CHUNK, cols]
                    stacked.append(jnp.concatenate([jnp.where(lo, slab, 0.0), jnp.where(lo, 0.0, slab)], axis=0))
                rhs = jnp.concatenate(stacked, axis=1).astype(_BF16)
                mixed = jnp.dot(wcat_ref[j], rhs, preferred_element_type=_F32)
                for k, c in enumerate((c0, c0 + SGU_CHUNK)):
                    m = mixed[:, k * LANES:(k + 1) * LANES] + bs_ref[j]
                    rows = slice(t * sub + c, t * sub + c + SGU_CHUNK)
                    sgu_ref[rows, cols] = (u[c:c + SGU_CHUNK, cols] * m).astype(_BF16)

    h = normed(0)
    pending = None
    for t in range(n_sub):
        h_next = normed(t + 1) if t + 1 < n_sub else None
        rows = slice(t * sub, (t + 1) * sub)
        z = jnp.dot(h, w_ref[:, :n_qkv], preferred_element_type=_F32)
        qkv_ref[rows, :d_attn] = z[:, :d_attn] * (HEAD_DIM ** -0.5 * LOG2E)
        qkv_ref[rows, d_attn:] = z[:, d_attn:]
        zg = jnp.dot(h, w_ref[:, n_qkv:], preferred_element_type=_F32)
        if pending is not None:
            gating(*pending)
        pending = (t, zg)
        h = h_next
    gating(*pending)


def _inproj_call(x2d, g_pre_mix, w_in, sgu_ln_g, sgu_ln_b, sgu_w, sgu_b, *, d_attn, d_sgu, tm):
    n_tok, d_model = x2d.shape
    n_steps = n_tok // tm
    n_slab = d_sgu // LANES
    n_groups = sgu_w.shape[0]
    wcat = (sgu_w.reshape(n_slab, n_groups // n_slab, SGU_CHUNK, SGU_CHUNK)
            .transpose(0, 2, 1, 3).reshape(n_slab, SGU_CHUNK, 2 * SGU_CHUNK).astype(_BF16))
    bs = jnp.repeat(sgu_b.reshape(n_slab, 2, SGU_CHUNK).transpose(0, 2, 1), HEAD_DIM, axis=2)
    d_in = w_in.shape[1]
    const = lambda *shape: pl.BlockSpec(shape, lambda i: (0,) * len(shape))
    return pl.pallas_call(
        functools.partial(_inproj_kernel, d_attn=d_attn, d_sgu=d_sgu, n_sub=tm // 256),
        grid=(n_steps,),
        in_specs=[
            pl.BlockSpec((tm, d_model), lambda i: (i, 0)),
            const(1, d_model),
            pl.BlockSpec((d_model, d_in), lambda i: (0, 0), pipeline_mode=pl.Buffered(1)),
            const(1, d_sgu),
            const(1, d_sgu),
            const(n_slab, SGU_CHUNK, 2 * SGU_CHUNK),
            const(n_slab, SGU_CHUNK, LANES),
        ],
        out_specs=[
            pl.BlockSpec((tm, 3 * d_attn), lambda i: (i, 0)),
            pl.BlockSpec((tm, d_sgu), lambda i: (i, 0)),
        ],
        out_shape=[
            jax.ShapeDtypeStruct((n_tok, 3 * d_attn), _F32),
            jax.ShapeDtypeStruct((n_tok, d_sgu), _BF16),
        ],
        scratch_shapes=[pltpu.VMEM((d_model, d_in), _BF16)],
        compiler_params=pltpu.CompilerParams(
            dimension_semantics=("arbitrary",), vmem_limit_bytes=VMEM_LIMIT_BYTES),
        name="inproj_sgu",
    )(x2d, g_pre_mix.reshape(1, -1), w_in, sgu_ln_g.reshape(1, -1),
      sgu_ln_b.reshape(1, -1), wcat, bs)


def _t5_bucket(rel):
    half = N_REL_BUCKETS // 2
    max_exact = half // 2
    ret = jnp.where(rel > 0, half, 0)
    n = jnp.abs(rel)
    nf = jnp.maximum(n, 1).astype(jnp.float32)
    large = max_exact + (jnp.log(nf / max_exact) / math.log(REL_MAX_DISTANCE / max_exact)
                         * (half - max_exact)).astype(jnp.int32)
    large = jnp.minimum(large, half - 1)
    return ret + jnp.where(n < max_exact, n, large)


def _bucket_tables():
    rel = jnp.arange(K_TILE) - HALF_WINDOW
    tabs = [jnp.where(jnp.abs(rel) <= HALF_WINDOW, _t5_bucket(rel * dil), -1)
            for _, dil in DILATED_BRANCHES]
    return jnp.broadcast_to(jnp.stack(tabs)[:, None, :], (len(tabs), 8, K_TILE)).astype(jnp.int32)


def _attn_kernel(relb_ref, bucket_ref, q_ref, k_ref, v_ref, *rest, seq, n_later):
    later_f32, o_ref, later_bf16 = rest[:n_later], rest[n_later], rest[n_later + 1:2 * n_later + 1]
    kp_ref, vp_ref, perm_ref, bias_ref, p_ref, ob_ref, mb_ref, lb_ref = rest[2 * n_later + 1:]
    hp = pl.program_id(0)
    n_br = len(DILATED_BRANCHES)
    n_tiles_total = seq // Q_TILE
    assert n_tiles_total % 2 == 0
    d1 = DILATED_BRANCHES[1][1]
    assert [d for _, d in DILATED_BRANCHES] == [1, d1, d1 * d1]
    cls_len = seq // d1
    lane_q = lax.broadcasted_iota(jnp.int32, (Q_TILE, LANES), 1) < HEAD_DIM
    col = lax.broadcasted_iota(jnp.int32, (Q_TILE, K_TILE), 1)

    @pl.when(pl.program_id(1) == 0)
    def _():
        for i in range(n_br):
            bucket = bucket_ref[i]
            for hh in range(2):
                row = jnp.full(bucket.shape, relb_ref[0, 2 * hp + hh] * LOG2E, _F32)
                for b in range(1, N_REL_BUCKETS):
                    row = jnp.where(bucket >= b, relb_ref[b, 2 * hp + hh] * LOG2E, row)
                row = jnp.where(bucket < 0, NEG_INF, row)
                tile = pltpu.roll(jnp.broadcast_to(row[:1], (Q_TILE, K_TILE)), 0, 1, stride=1, stride_axis=0)
                rows = slice(hh * Q_TILE, (hh + 1) * Q_TILE)
                bias_ref[i, 0, rows, :] = jnp.where(col < HALF_WINDOW, NEG_INF, tile)
                bias_ref[i, 1, rows, :] = tile
                bias_ref[i, 2, rows, :] = jnp.where(col >= K_TILE - HALF_WINDOW, NEG_INF, tile)

    for src_ref, dst_ref in zip(later_f32, later_bf16):
        dst_ref[...] = src_ref[...].astype(_BF16)

    zero_pad = jnp.zeros((HALF_WINDOW, LANES), _BF16)
    for i in range(n_br):
        for ref in (kp_ref, vp_ref):
            ref[i, :HALF_WINDOW, :] = zero_pad
            ref[i, HALF_WINDOW + seq:, :] = zero_pad
    for c0 in range(0, seq, COPY_ROWS):
        kp_ref[0, pl.ds(HALF_WINDOW + c0, COPY_ROWS), :] = k_ref[pl.ds(c0, COPY_ROWS), :].astype(_BF16)
        vp_ref[0, pl.ds(HALF_WINDOW + c0, COPY_ROWS), :] = v_ref[pl.ds(c0, COPY_ROWS), :].astype(_BF16)
    for r in range(d1):
        for c0 in range(0, cls_len, COPY_ROWS):
            src = pl.ds(r + d1 * c0, COPY_ROWS, stride=d1)
            dst = pl.ds(r * cls_len + c0, COPY_ROWS)
            padded = pl.ds(HALF_WINDOW + r * cls_len + c0, COPY_ROWS)
            perm_ref[0, dst, :] = q_ref[src, :]
            for a, (x_ref, xp_ref) in enumerate(((k_ref, kp_ref), (v_ref, vp_ref)), start=1):
                x = x_ref[src, :]
                perm_ref[a, dst, :] = x
                xp_ref[1, padded, :] = x.astype(_BF16)
    sub_len = cls_len // d1
    for r2 in range(d1 * d1):
        src = pl.ds((r2 % d1) * cls_len + r2 // d1, sub_len, stride=d1)
        padded = pl.ds(HALF_WINDOW + r2 * sub_len, sub_len)
        kp_ref[2, padded, :] = perm_ref[1, src, :].astype(_BF16)
        vp_ref[2, padded, :] = perm_ref[2, src, :].astype(_BF16)

    def tile_geometry(i, j):
        n_tiles = seq // (DILATED_BRANCHES[i][1] * Q_TILE)
        n = j % n_tiles
        variant = 0 if n == 0 else (2 if n == n_tiles - 1 else 1)
        if i < 2:
            rows = pl.ds(j * Q_TILE, Q_TILE)
        else:
            r2 = j // n_tiles
            rows = pl.ds((r2 % d1) * cls_len + r2 // d1 + d1 * Q_TILE * n, Q_TILE, stride=d1)
        return rows, pl.ds(j * Q_TILE, K_TILE), variant

    def stage_scores(i, j, slot):
        rows, win, variant = tile_geometry(i, j)
        q = q_ref[rows, :] if i == 0 else perm_ref[0, rows, :]
        q2 = jnp.concatenate([jnp.where(lane_q, q, 0.0), jnp.where(lane_q, 0.0, q)], axis=0)
        s = lax.dot_general(q2.astype(_BF16), kp_ref[i, win, :], (((1,), (1,)), ((), ())),
                            preferred_element_type=_F32)
        s = s + bias_ref[i, variant]
        m2 = jnp.max(s, axis=-1, keepdims=True)
        p_ref[slot] = jnp.exp2(s - m2).astype(_BF16)
        mb_ref[i, rows, :] = jnp.where(lane_q, m2[:Q_TILE], m2[Q_TILE:])

    ones = jnp.ones((K_TILE, LANES), _BF16)

    def stage_values(i, j, slot):
        rows, win, _ = tile_geometry(i, j)
        v1 = jnp.concatenate([vp_ref[i, win, :], ones], axis=1)
        pv = jnp.dot(p_ref[slot], v1, preferred_element_type=_F32)
        ob_ref[i, rows, :] = jnp.where(lane_q, pv[:Q_TILE, :LANES], pv[Q_TILE:, :LANES])
        lb_ref[i, rows, :] = jnp.where(lane_q, pv[:Q_TILE, LANES:], pv[Q_TILE:, LANES:])

    stage_scores(0, 0, 0)
    for i in range(n_br):
        for j in range(n_tiles_total):
            if j + 1 < n_tiles_total:
                stage_scores(i, j + 1, (j + 1) % 2)
            elif i + 1 < n_br:
                stage_scores(i + 1, 0, 0)
            stage_values(i, j, j % 2)

    blocks_per_class = cls_len // MERGE_ROWS

    def merge_body(t, carry):
        cls_rows = pl.ds(pl.multiple_of(t * MERGE_ROWS, MERGE_ROWS), MERGE_ROWS)
        nat_rows = pl.ds(t // blocks_per_class + d1 * MERGE_ROWS * (t % blocks_per_class), MERGE_ROWS, stride=d1)
        rows = [nat_rows] + [cls_rows] * (n_br - 1)
        ms = [mb_ref[i, rows[i], :] for i in range(n_br)]
        m = functools.reduce(jnp.maximum, ms)
        ws = [jnp.exp2(m_i - m) for m_i in ms]
        num = functools.reduce(jnp.add, [w * ob_ref[i, rows[i], :] for i, w in enumerate(ws)])
        den = functools.reduce(jnp.add, [w * lb_ref[i, rows[i], :] for i, w in enumerate(ws)])
        perm_ref[1, nat_rows, :] = num / den
        return carry

    lax.fori_loop(0, seq // MERGE_ROWS, merge_body, 0, unroll=2)
    for c0 in range(0, seq, COPY_ROWS):
        o_ref[pl.ds(c0, COPY_ROWS), :] = perm_ref[1, pl.ds(c0, COPY_ROWS), :].astype(o_ref.dtype)


def _attn_call(qkv, rel_bias, later_weights, *, batch, seq, d_attn):
    n_pairs = d_attn // LANES
    n_steps = n_pairs * batch
    assert all(w.shape[0] % (16 * n_steps) == 0 for w in later_weights)
    row_block = lambda w: pl.BlockSpec((w.shape[0] // n_steps, w.shape[1]), lambda hp, b: (hp * batch + b, 0))
    qkv3 = qkv.reshape(batch, seq, 3 * d_attn)
    slab = lambda off: pl.BlockSpec((None, seq, LANES), lambda hp, b: (b, 0, off + hp))
    n_br = len(DILATED_BRANCHES)
    return pl.pallas_call(
        functools.partial(_attn_kernel, seq=seq, n_later=len(later_weights)),
        grid=(n_pairs, batch),
        in_specs=[
            pl.BlockSpec(memory_space=pltpu.SMEM),
            pl.BlockSpec((n_br, 8, K_TILE), lambda hp, b: (0, 0, 0)),
            slab(0), slab(n_pairs), slab(2 * n_pairs),
        ] + [row_block(w) for w in later_weights],
        out_specs=[pl.BlockSpec((None, seq, LANES), lambda hp, b: (b, 0, hp))]
        + [row_block(w) for w in later_weights],
        out_shape=[jax.ShapeDtypeStruct((batch, seq, d_attn), _BF16)]
        + [jax.ShapeDtypeStruct(w.shape, _BF16) for w in later_weights],
        scratch_shapes=[
            pltpu.VMEM((n_br, seq + 2 * HALF_WINDOW, LANES), _BF16),
            pltpu.VMEM((n_br, seq + 2 * HALF_WINDOW, LANES), _BF16),
            pltpu.VMEM((3, seq, LANES), _F32),
            pltpu.VMEM((n_br, 3, 2 * Q_TILE, K_TILE), _F32),
            pltpu.VMEM((2, 2 * Q_TILE, K_TILE), _BF16),
            pltpu.VMEM((n_br, seq, LANES), _F32),
            pltpu.VMEM((n_br, seq, LANES), _F32),
            pltpu.VMEM((n_br, seq, LANES), _F32),
        ],
        compiler_params=pltpu.CompilerParams(
            dimension_semantics=("arbitrary", "arbitrary"), vmem_limit_bytes=VMEM_LIMIT_BYTES),
        name="dilated_attn",
    )(rel_bias, _bucket_tables(), qkv3, qkv3, qkv3, *later_weights)


def _outproj_ffn_kernel(attn_ref, sgu_ref, x_ref, wo_ref, g1_ref, g2_ref, w1_ref, w2_ref, g3_ref,
                        o_ref, *, d_attn, ff_chunk, n_sub):
    sub = x_ref.shape[0] // n_sub
    rows = [slice(t * sub, (t + 1) * sub) for t in range(n_sub)]
    chunks = range(0, w1_ref.shape[1], ff_chunk)

    def out_proj(t):
        return (jnp.dot(attn_ref[rows[t], :], wo_ref[:d_attn, :], preferred_element_type=_F32)
                + jnp.dot(sgu_ref[rows[t], :], wo_ref[d_attn:, :], preferred_element_type=_F32))

    def mid_norms(t, mix):
        x1 = x_ref[rows[t], :] + _rms_norm(mix, g1_ref[...])
        return x1, _rms_norm(x1, g2_ref[...]).astype(_BF16)

    def ffn_chunk(h, f, c0):
        a = jnp.maximum(jnp.dot(h, w1_ref[:, c0:c0 + ff_chunk], preferred_element_type=_F32), 0.0)
        part = jnp.dot((a * a).astype(_BF16), w2_ref[c0:c0 + ff_chunk, :], preferred_element_type=_F32)
        return part if f is None else f + part

    mix = out_proj(0)
    prev = None
    for t in range(n_sub):
        nxt_mix = out_proj(t + 1) if t + 1 < n_sub else None
        x1, h = mid_norms(t, mix)
        f = None
        for ci, c0 in enumerate(chunks):
            f = ffn_chunk(h, f, c0)
            if ci == 0 and prev is not None:
                pt, px1, pf = prev
                o_ref[rows[pt], :] = px1 + _rms_norm(pf, g3_ref[...])
        prev = (t, x1, f)
        mix = nxt_mix
    pt, px1, pf = prev
    o_ref[rows[pt], :] = px1 + _rms_norm(pf, g3_ref[...])


def _outproj_ffn_call(attn2d, sgu2d, x2d, w_out, g_post_mix, g_pre_ffn, w_ff1, w_ff2, g_post_ffn, *, tm):
    n_tok, d_model = x2d.shape
    d_attn = attn2d.shape[1]
    d_sgu = sgu2d.shape[1]
    d_ff = w_ff1.shape[1]
    const = lambda *shape: pl.BlockSpec(shape, lambda i: (0,) * len(shape), pipeline_mode=pl.Buffered(1))
    return pl.pallas_call(
        functools.partial(_outproj_ffn_kernel, d_attn=d_attn, ff_chunk=1024, n_sub=4),
        grid=(n_tok // tm,),
        in_specs=[
            pl.BlockSpec((tm, d_attn), lambda i: (i, 0)),
            pl.BlockSpec((tm, d_sgu), lambda i: (i, 0)),
            pl.BlockSpec((tm, d_model), lambda i: (i, 0)),
            const(d_attn + d_sgu, d_model),
            const(1, d_model),
            const(1, d_model),
            const(d_model, d_ff),
            const(d_ff, d_model),
            const(1, d_model),
        ],
        out_specs=pl.BlockSpec((tm, d_model), lambda i: (i, 0)),
        out_shape=jax.ShapeDtypeStruct((n_tok, d_model), _F32),
        compiler_params=pltpu.CompilerParams(
            dimension_semantics=("arbitrary",), vmem_limit_bytes=VMEM_LIMIT_BYTES),
        name="outproj_ffn",
    )(attn2d, sgu2d, x2d, w_out, g_post_mix.reshape(1, -1), g_pre_ffn.reshape(1, -1),
      w_ff1, w_ff2, g_post_ffn.reshape(1, -1))


def kernel(x, g_pre_mix, w_in, sgu_ln_g, sgu_ln_b, sgu_w, sgu_b, w_out, g_post_mix, g_pre_ffn,
           w_ff1, w_ff2, g_post_ffn, rel_bias):
    batch, seq, d_model = x.shape
    d_sgu = sgu_ln_g.shape[-1]
    d_attn = w_out.shape[1] - d_sgu
    assert d_attn % LANES == 0 and d_sgu % LANES == 0 and sgu_w.shape[-1] == SGU_CHUNK
    assert all(seq % (dil * Q_TILE) == 0 and win == 2 * HALF_WINDOW * dil for win, dil in DILATED_BRANCHES)
    x2d = x.reshape(batch * seq, d_model)
    for layer in range(g_pre_mix.shape[0]):
        qkv, sgu = _inproj_call(x2d, g_pre_mix[layer], w_in[layer], sgu_ln_g[layer], sgu_ln_b[layer],
                                sgu_w[layer], sgu_b[layer], d_attn=d_attn, d_sgu=d_sgu, tm=1024)
        attn, w_out_bf, w_ff1_bf, w_ff2_bf = _attn_call(
            qkv, rel_bias, (w_out[layer], w_ff1[layer], w_ff2[layer]), batch=batch, seq=seq, d_attn=d_attn)
        x2d = _outproj_ffn_call(attn.reshape(batch * seq, d_attn), sgu, x2d, w_out_bf,
                                g_post_mix[layer], g_pre_ffn[layer], w_ff1_bf, w_ff2_bf,
                                g_post_ffn[layer], tm=1024)
    return x2d.reshape(batch, seq, d_model)
```

```python
import functools
import math

import jax
import jax.numpy as jnp
from jax import lax
from jax.experimental import pallas as pl
from jax.experimental.pallas import tpu as pltpu

HEAD_DIM = 64
DILATED_BRANCHES = ((128, 1), (512, 4), (2048, 16))
SGU_CHUNK = 128
N_REL_BUCKETS = 32
REL_MAX_DISTANCE = 1024
RMS_EPS = 1e-6
LN_EPS = 1e-5
NEG_INF = -1e30

LANES = 128
HALF_WINDOW = 64
Q_TILE = 128
K_TILE = Q_TILE + 2 * HALF_WINDOW
MERGE_ROWS = 256
COPY_ROWS = 512
CAST_COLS = 512
LOG2E = math.log2(math.e)
V7X_VMEM_BYTES = 64 * 1024 * 1024
VMEM_LIMIT_BYTES = V7X_VMEM_BYTES - 8 * 1024 * 1024

_F32 = jnp.float32
_BF16 = jnp.bfloat16


def _rms_norm(x, g):
    return x * lax.rsqrt(jnp.mean(x * x, axis=-1, keepdims=True) + RMS_EPS) * g


def _inproj_kernel(x_ref, g_ref, w32_ref, lng_ref, lnb_ref, ws_ref, bst_ref, *rest,
                   d_attn, d_sgu, n_sub, n_later):
    later_f32, (qkv_ref, sgu_ref), later_bf16 = rest[:n_later], rest[n_later:n_later + 2], rest[n_later + 2:-3]
    w_ref, wcat_ref, bs_ref = rest[-3:]
    lane = lax.broadcasted_iota(jnp.int32, (SGU_CHUNK, LANES), 1)
    lo = lane < HEAD_DIM

    @pl.when(pl.program_id(0) == 0)
    def _():
        for c0 in range(0, w_ref.shape[1], CAST_COLS):
            w_ref[:, c0:c0 + CAST_COLS] = w32_ref[:, c0:c0 + CAST_COLS].astype(_BF16)
        for j in range(d_sgu // LANES):
            wcat_ref[j] = jnp.concatenate([ws_ref[2 * j], ws_ref[2 * j + 1]], axis=1).astype(_BF16)
            bs_ref[j] = jnp.where(lo, bst_ref[:, 2 * j:2 * j + 1], bst_ref[:, 2 * j + 1:2 * j + 2])

    sub = x_ref.shape[0] // n_sub
    n_qkv = 3 * d_attn
    pair = 2 * SGU_CHUNK
    assert sub % pair == 0

    def normed(t):
        return _rms_norm(x_ref[t * sub:(t + 1) * sub, :], g_ref[...]).astype(_BF16)

    def gating(t, zg):
        zg = jax.nn.gelu(zg)
        u = zg[:, :d_sgu]
        v = zg[:, d_sgu:]
        mu = jnp.mean(v, axis=-1, keepdims=True)
        vc = v - mu
        vn = vc * lax.rsqrt(jnp.mean(vc * vc, axis=-1, keepdims=True) + LN_EPS) * lng_ref[...] + lnb_ref[...]
        for c0 in range(0, sub, pair):
            for j in range(d_sgu // LANES):
                cols = slice(j * LANES, (j + 1) * LANES)
                stacked = []
                for c in (c0, c0 + SGU_CHUNK):
                    slab = vn[c:c + SGU_CHUNK, cols]
                    stacked.append(jnp.concatenate([jnp.where(lo, slab, 0.0), jnp.where(lo, 0.0, slab)], axis=0))
                rhs = jnp.concatenate(stacked, axis=1).astype(_BF16)
                mixed = jnp.dot(wcat_ref[j], rhs, preferred_element_type=_F32)
                for k, c in enumerate((c0, c0 + SGU_CHUNK)):
                    m = mixed[:, k * LANES:(k + 1) * LANES] + bs_ref[j]
                    rows = slice(t * sub + c, t * sub + c + SGU_CHUNK)
                    sgu_ref[rows, cols] = (u[c:c + SGU_CHUNK, cols] * m).astype(_BF16)

    h = normed(0)
    pending = None
    for t in range(n_sub):
        h_next = normed(t + 1) if t + 1 < n_sub else None
        rows = slice(t * sub, (t + 1) * sub)
        z = jnp.dot(h, w_ref[:, :n_qkv], preferred_element_type=_F32)
        qkv_ref[rows, :d_attn] = z[:, :d_attn] * (HEAD_DIM ** -0.5 * LOG2E)
        qkv_ref[rows, d_attn:] = z[:, d_attn:]
        zg = jnp.dot(h, w_ref[:, n_qkv:], preferred_element_type=_F32)
        if t == 0:
            for src_ref, dst_ref in zip(later_f32, later_bf16):
                dst_ref[...] = src_ref[...].astype(_BF16)
        if pending is not None:
            gating(*pending)
        pending = (t, zg)
        h = h_next
    gating(*pending)


def _inproj_call(x2d, g_pre_mix, w_in, sgu_ln_g, sgu_ln_b, sgu_w, sgu_b, later_weights, *, d_attn, d_sgu, tm):
    n_tok, d_model = x2d.shape
    n_steps = n_tok // tm
    row_block = lambda w: pl.BlockSpec((w.shape[0] // n_steps, w.shape[1]), lambda i: (i, 0))
    assert all(w.shape[0] % (16 * n_steps) == 0 for w in later_weights)
    n_slab = d_sgu // LANES
    n_groups = sgu_w.shape[0]
    assert n_groups == 2 * n_slab
    d_in = w_in.shape[1]
    const = lambda *shape: pl.BlockSpec(shape, lambda i: (0,) * len(shape))
    return pl.pallas_call(
        functools.partial(_inproj_kernel, d_attn=d_attn, d_sgu=d_sgu, n_sub=tm // 256,
                          n_later=len(later_weights)),
        grid=(n_steps,),
        in_specs=[
            pl.BlockSpec((tm, d_model), lambda i: (i, 0)),
            const(1, d_model),
            pl.BlockSpec((d_model, d_in), lambda i: (0, 0), pipeline_mode=pl.Buffered(1)),
            const(1, d_sgu),
            const(1, d_sgu),
            const(n_groups, SGU_CHUNK, SGU_CHUNK),
            const(SGU_CHUNK, n_groups),
        ] + [row_block(w) for w in later_weights],
        out_specs=[
            pl.BlockSpec((tm, 3 * d_attn), lambda i: (i, 0)),
            pl.BlockSpec((tm, d_sgu), lambda i: (i, 0)),
        ] + [row_block(w) for w in later_weights],
        out_shape=[
            jax.ShapeDtypeStruct((n_tok, 3 * d_attn), _F32),
            jax.ShapeDtypeStruct((n_tok, d_sgu), _BF16),
        ] + [jax.ShapeDtypeStruct(w.shape, _BF16) for w in later_weights],
        scratch_shapes=[
            pltpu.VMEM((d_model, d_in), _BF16),
            pltpu.VMEM((n_slab, SGU_CHUNK, 2 * SGU_CHUNK), _BF16),
            pltpu.VMEM((n_slab, SGU_CHUNK, LANES), _F32),
        ],
        compiler_params=pltpu.CompilerParams(
            dimension_semantics=("arbitrary",), vmem_limit_bytes=VMEM_LIMIT_BYTES),
        name="inproj_sgu",
    )(x2d, g_pre_mix.reshape(1, -1), w_in, sgu_ln_g.reshape(1, -1),
      sgu_ln_b.reshape(1, -1), sgu_w, sgu_b.T, *later_weights)


def _t5_bucket(rel):
    half = N_REL_BUCKETS // 2
    max_exact = half // 2
    ret = jnp.where(rel > 0, half, 0)
    n = jnp.abs(rel)
    nf = jnp.maximum(n, 1).astype(jnp.float32)
    large = max_exact + (jnp.log(nf / max_exact) / math.log(REL_MAX_DISTANCE / max_exact)
                         * (half - max_exact)).astype(jnp.int32)
    large = jnp.minimum(large, half - 1)
    return ret + jnp.where(n < max_exact, n, large)


def _bucket_tables():
    rel = jnp.arange(K_TILE) - HALF_WINDOW
    tabs = [jnp.where(jnp.abs(rel) <= HALF_WINDOW, _t5_bucket(rel * dil), -1)
            for _, dil in DILATED_BRANCHES]
    return jnp.broadcast_to(jnp.stack(tabs)[:, None, :], (len(tabs), 8, K_TILE)).astype(jnp.int32)


def _attn_kernel(relb_ref, bucket_ref, q_ref, k_ref, v_ref, o_ref,
                 kp_ref, vp_ref, perm_ref, bias_ref, p_ref, ob_ref, mb_ref, lb_ref, *, seq):
    hp = pl.program_id(0)
    n_br = len(DILATED_BRANCHES)
    n_tiles_total = seq // Q_TILE
    assert n_tiles_total % 2 == 0
    d1 = DILATED_BRANCHES[1][1]
    assert [d for _, d in DILATED_BRANCHES] == [1, d1, d1 * d1]
    cls_len = seq // d1
    lane_q = lax.broadcasted_iota(jnp.int32, (Q_TILE, LANES), 1) < HEAD_DIM
    col = lax.broadcasted_iota(jnp.int32, (Q_TILE, K_TILE), 1)

    @pl.when(pl.program_id(1) == 0)
    def _():
        for i in range(n_br):
            bucket = bucket_ref[i]
            for hh in range(2):
                row = jnp.full(bucket.shape, relb_ref[0, 2 * hp + hh] * LOG2E, _F32)
                for b in range(1, N_REL_BUCKETS):
                    row = jnp.where(bucket >= b, relb_ref[b, 2 * hp + hh] * LOG2E, row)
                row = jnp.where(bucket < 0, NEG_INF, row)
                tile = pltpu.roll(jnp.broadcast_to(row[:1], (Q_TILE, K_TILE)), 0, 1, stride=1, stride_axis=0)
                rows = slice(hh * Q_TILE, (hh + 1) * Q_TILE)
                bias_ref[i, 0, rows, :] = jnp.where(col < HALF_WINDOW, NEG_INF, tile)
                bias_ref[i, 1, rows, :] = tile
                bias_ref[i, 2, rows, :] = jnp.where(col >= K_TILE - HALF_WINDOW, NEG_INF, tile)

    zero_pad = jnp.zeros((HALF_WINDOW, LANES), _BF16)
    for i in range(n_br):
        for ref in (kp_ref, vp_ref):
            ref[i, :HALF_WINDOW, :] = zero_pad
            ref[i, HALF_WINDOW + seq:, :] = zero_pad
    for c0 in range(0, seq, COPY_ROWS):
        kp_ref[0, pl.ds(HALF_WINDOW + c0, COPY_ROWS), :] = k_ref[pl.ds(c0, COPY_ROWS), :].astype(_BF16)
        vp_ref[0, pl.ds(HALF_WINDOW + c0, COPY_ROWS), :] = v_ref[pl.ds(c0, COPY_ROWS), :].astype(_BF16)
    for r in range(d1):
        for c0 in range(0, cls_len, COPY_ROWS):
            src = pl.ds(r + d1 * c0, COPY_ROWS, stride=d1)
            dst = pl.ds(r * cls_len + c0, COPY_ROWS)
            padded = pl.ds(HALF_WINDOW + r * cls_len + c0, COPY_ROWS)
            perm_ref[0, dst, :] = q_ref[src, :]
            for a, (x_ref, xp_ref) in enumerate(((k_ref, kp_ref), (v_ref, vp_ref)), start=1):
                x = x_ref[src, :]
                perm_ref[a, dst, :] = x
                xp_ref[1, padded, :] = x.astype(_BF16)
    sub_len = cls_len // d1
    for r2 in range(d1 * d1):
        src = pl.ds((r2 % d1) * cls_len + r2 // d1, sub_len, stride=d1)
        padded = pl.ds(HALF_WINDOW + r2 * sub_len, sub_len)
        kp_ref[2, padded, :] = perm_ref[1, src, :].astype(_BF16)
        vp_ref[2, padded, :] = perm_ref[2, src, :].astype(_BF16)

    def tile_geometry(i, j):
        n_tiles = seq // (DILATED_BRANCHES[i][1] * Q_TILE)
        n = j % n_tiles
        variant = 0 if n == 0 else (2 if n == n_tiles - 1 else 1)
        if i < 2:
            rows = pl.ds(j * Q_TILE, Q_TILE)
        else:
            r2 = j // n_tiles
            rows = pl.ds((r2 % d1) * cls_len + r2 // d1 + d1 * Q_TILE * n, Q_TILE, stride=d1)
        return rows, pl.ds(j * Q_TILE, K_TILE), variant

    def stage_scores(i, j, slot):
        rows, win, variant = tile_geometry(i, j)
        q = q_ref[rows, :] if i == 0 else perm_ref[0, rows, :]
        q2 = jnp.concatenate([jnp.where(lane_q, q, 0.0), jnp.where(lane_q, 0.0, q)], axis=0)
        s = lax.dot_general(q2.astype(_BF16), kp_ref[i, win, :], (((1,), (1,)), ((), ())),
                            preferred_element_type=_F32)
        s = s + bias_ref[i, variant]
        m2 = jnp.max(s, axis=-1, keepdims=True)
        p_ref[slot] = jnp.exp2(s - m2).astype(_BF16)
        mb_ref[i, rows, :] = jnp.where(lane_q, m2[:Q_TILE], m2[Q_TILE:])

    ones = jnp.ones((K_TILE, LANES), _BF16)

    def stage_values(i, j, slot):
        rows, win, _ = tile_geometry(i, j)
        v1 = jnp.concatenate([vp_ref[i, win, :], ones], axis=1)
        pv = jnp.dot(p_ref[slot], v1, preferred_element_type=_F32)
        ob_ref[i, rows, :] = jnp.where(lane_q, pv[:Q_TILE, :LANES], pv[Q_TILE:, :LANES])
        lb_ref[i, rows, :] = jnp.where(lane_q, pv[:Q_TILE, LANES:], pv[Q_TILE:, LANES:])

    stage_scores(0, 0, 0)
    for i in range(n_br):
        for j in range(n_tiles_total):
            if j + 1 < n_tiles_total:
                stage_scores(i, j + 1, (j + 1) % 2)
            elif i + 1 < n_br:
                stage_scores(i + 1, 0, 0)
            stage_values(i, j, j % 2)

    blocks_per_class = cls_len // MERGE_ROWS

    def merge_body(t, carry):
        cls_rows = pl.ds(pl.multiple_of(t * MERGE_ROWS, MERGE_ROWS), MERGE_ROWS)
        nat_rows = pl.ds(t // blocks_per_class + d1 * MERGE_ROWS * (t % blocks_per_class), MERGE_ROWS, stride=d1)
        rows = [nat_rows] + [cls_rows] * (n_br - 1)
        ms = [mb_ref[i, rows[i], :] for i in range(n_br)]
        m = functools.reduce(jnp.maximum, ms)
        ws = [jnp.exp2(m_i - m) for m_i in ms]
        num = functools.reduce(jnp.add, [w * ob_ref[i, rows[i], :] for i, w in enumerate(ws)])
        den = functools.reduce(jnp.add, [w * lb_ref[i, rows[i], :] for i, w in enumerate(ws)])
        perm_ref[1, nat_rows, :] = num / den
        return carry

    lax.fori_loop(0, seq // MERGE_ROWS, merge_body, 0, unroll=2)
    for c0 in range(0, seq, COPY_ROWS):
        o_ref[pl.ds(c0, COPY_ROWS), :] = perm_ref[1, pl.ds(c0, COPY_ROWS), :].astype(o_ref.dtype)


def _attn_call(qkv, rel_bias, *, batch, seq, d_attn):
    n_pairs = d_attn // LANES
    qkv3 = qkv.reshape(batch, seq, 3 * d_attn)
    slab = lambda off: pl.BlockSpec((None, seq, LANES), lambda hp, b: (b, 0, off + hp))
    n_br = len(DILATED_BRANCHES)
    return pl.pallas_call(
        functools.partial(_attn_kernel, seq=seq),
        grid=(n_pairs, batch),
        in_specs=[
            pl.BlockSpec(memory_space=pltpu.SMEM),
            pl.BlockSpec((n_br, 8, K_TILE), lambda hp, b: (0, 0, 0)),
            slab(0), slab(n_pairs), slab(2 * n_pairs),
        ],
        out_specs=pl.BlockSpec((None, seq, LANES), lambda hp, b: (b, 0, hp)),
        out_shape=jax.ShapeDtypeStruct((batch, seq, d_attn), _BF16),
        scratch_shapes=[
            pltpu.VMEM((n_br, seq + 2 * HALF_WINDOW, LANES), _BF16),
            pltpu.VMEM((n_br, seq + 2 * HALF_WINDOW, LANES), _BF16),
            pltpu.VMEM((3, seq, LANES), _F32),
            pltpu.VMEM((n_br, 3, 2 * Q_TILE, K_TILE), _F32),
            pltpu.VMEM((2, 2 * Q_TILE, K_TILE), _BF16),
            pltpu.VMEM((n_br, seq, LANES), _F32),
            pltpu.VMEM((n_br, seq, LANES), _F32),
            pltpu.VMEM((n_br, seq, LANES), _F32),
        ],
        compiler_params=pltpu.CompilerParams(
            dimension_semantics=("arbitrary", "arbitrary"), vmem_limit_bytes=VMEM_LIMIT_BYTES),
        name="dilated_attn",
    )(rel_bias, _bucket_tables(), qkv3, qkv3, qkv3)


def _outproj_ffn_kernel(attn_ref, sgu_ref, x_ref, wo_ref, g1_ref, g2_ref, w1_ref, w2_ref, g3_ref,
                        o_ref, *, d_attn, ff_chunk, n_sub):
    sub = x_ref.shape[0] // n_sub
    rows = [slice(t * sub, (t + 1) * sub) for t in range(n_sub)]
    chunks = range(0, w1_ref.shape[1], ff_chunk)

    def out_proj(t):
        return (jnp.dot(attn_ref[rows[t], :], wo_ref[:d_attn, :], preferred_element_type=_F32)
                + jnp.dot(sgu_ref[rows[t], :], wo_ref[d_attn:, :], preferred_element_type=_F32))

    def mid_norms(t, mix):
        x1 = x_ref[rows[t], :] + _rms_norm(mix, g1_ref[...])
        return x1, _rms_norm(x1, g2_ref[...]).astype(_BF16)

    def ffn_chunk(h, f, c0):
        a = jnp.maximum(jnp.dot(h, w1_ref[:, c0:c0 + ff_chunk], preferred_element_type=_F32), 0.0)
        part = jnp.dot((a * a).astype(_BF16), w2_ref[c0:c0 + ff_chunk, :], preferred_element_type=_F32)
        return part if f is None else f + part

    mix = out_proj(0)
    prev = None
    for t in range(n_sub):
        nxt_mix = out_proj(t + 1) if t + 1 < n_sub else None
        x1, h = mid_norms(t, mix)
        f = None
        for ci, c0 in enumerate(chunks):
            f = ffn_chunk(h, f, c0)
            if ci == 0 and prev is not None:
                pt, px1, pf = prev
                o_ref[rows[pt], :] = px1 + _rms_norm(pf, g3_ref[...])
        prev = (t, x1, f)
        mix = nxt_mix
    pt, px1, pf = prev
    o_ref[rows[pt], :] = px1 + _rms_norm(pf, g3_ref[...])


def _outproj_ffn_call(attn2d, sgu2d, x2d, w_out, g_post_mix, g_pre_ffn, w_ff1, w_ff2, g_post_ffn, *, tm):
    n_tok, d_model = x2d.shape
    d_attn = attn2d.shape[1]
    d_sgu = sgu2d.shape[1]
    d_ff = w_ff1.shape[1]
    const = lambda *shape: pl.BlockSpec(shape, lambda i: (0,) * len(shape), pipeline_mode=pl.Buffered(1))
    return pl.pallas_call(
        functools.partial(_outproj_ffn_kernel, d_attn=d_attn, ff_chunk=1024, n_sub=4),
        grid=(n_tok // tm,),
        in_specs=[
            pl.BlockSpec((tm, d_attn), lambda i: (i, 0)),
            pl.BlockSpec((tm, d_sgu), lambda i: (i, 0)),
            pl.BlockSpec((tm, d_model), lambda i: (i, 0)),
            const(d_attn + d_sgu, d_model),
            const(1, d_model),
            const(1, d_model),
            const(d_model, d_ff),
            const(d_ff, d_model),
            const(1, d_model),
        ],
        out_specs=pl.BlockSpec((tm, d_model), lambda i: (i, 0)),
        out_shape=jax.ShapeDtypeStruct((n_tok, d_model), _F32),
        compiler_params=pltpu.CompilerParams(
            dimension_semantics=("arbitrary",), vmem_limit_bytes=VMEM_LIMIT_BYTES),
        name="outproj_ffn",
    )(attn2d, sgu2d, x2d, w_out, g_post_mix.reshape(1, -1), g_pre_ffn.reshape(1, -1),
      w_ff1, w_ff2, g_post_ffn.reshape(1, -1))


def kernel(x, g_pre_mix, w_in, sgu_ln_g, sgu_ln_b, sgu_w, sgu_b, w_out, g_post_mix, g_pre_ffn,
           w_ff1, w_ff2, g_post_ffn, rel_bias):
    batch, seq, d_model = x.shape
    d_sgu = sgu_ln_g.shape[-1]
    d_attn = w_out.shape[1] - d_sgu
    assert d_attn % LANES == 0 and d_sgu % LANES == 0 and sgu_w.shape[-1] == SGU_CHUNK
    assert all(seq % (dil * Q_TILE) == 0 and win == 2 * HALF_WINDOW * dil for win, dil in DILATED_BRANCHES)
    x2d = x.reshape(batch * seq, d_model)
    for layer in range(g_pre_mix.shape[0]):
        qkv, sgu, w_out_bf, w_ff1_bf, w_ff2_bf = _inproj_call(
            x2d, g_pre_mix[layer], w_in[layer], sgu_ln_g[layer], sgu_ln_b[layer], sgu_w[layer], sgu_b[layer],
            (w_out[layer], w_ff1[layer], w_ff2[layer]), d_attn=d_attn, d_sgu=d_sgu, tm=1024)
        attn = _attn_call(qkv, rel_bias, batch=batch, seq=seq, d_attn=d_attn)
        x2d = _outproj_ffn_call(attn.reshape(batch * seq, d_attn), sgu, x2d, w_out_bf,
                                g_post_mix[layer], g_pre_ffn[layer], w_ff1_bf, w_ff2_bf,
                                g_post_ffn[layer], tm=1024)
    return x2d.reshape(batch, seq, d_model)
```

```python
import functools
import math

import jax
import jax.numpy as jnp
from jax import lax
from jax.experimental import pallas as pl
from jax.experimental.pallas import tpu as pltpu

HEAD_DIM = 64
DILATED_BRANCHES = ((128, 1), (512, 4), (2048, 16))
SGU_CHUNK = 128
N_REL_BUCKETS = 32
REL_MAX_DISTANCE = 1024
RMS_EPS = 1e-6
LN_EPS = 1e-5
NEG_INF = -1e30

LANES = 128
HALF_WINDOW = 64
Q_TILE = 128
K_TILE = Q_TILE + 2 * HALF_WINDOW
MERGE_ROWS = 256
COPY_ROWS = 512
CAST_COLS = 512
LOG2E = math.log2(math.e)
V7X_VMEM_BYTES = 64 * 1024 * 1024
VMEM_LIMIT_BYTES = V7X_VMEM_BYTES - 8 * 1024 * 1024

_F32 = jnp.float32
_BF16 = jnp.bfloat16


def _rms_norm(x, g):
    return x * lax.rsqrt(jnp.mean(x * x, axis=-1, keepdims=True) + RMS_EPS) * g


def _inproj_kernel(x_ref, g_ref, w32_ref, lng_ref, lnb_ref, ws_ref, bsg_ref, *rest,
                   d_attn, d_sgu, n_sub, n_later):
    later_f32, (qkv_ref, sgu_ref), later_bf16 = rest[:n_later], rest[n_later:n_later + 2], rest[n_later + 2:-3]
    w_ref, wcat_ref, bs_ref = rest[-3:]
    lane = lax.broadcasted_iota(jnp.int32, (SGU_CHUNK, LANES), 1)
    lo = lane < HEAD_DIM

    @pl.when(pl.program_id(0) == 0)
    def _():
        for c0 in range(0, w_ref.shape[1], CAST_COLS):
            w_ref[:, c0:c0 + CAST_COLS] = w32_ref[:, c0:c0 + CAST_COLS].astype(_BF16)
        bst = bsg_ref[...].T
        for j in range(d_sgu // LANES):
            wcat_ref[j] = jnp.concatenate([ws_ref[2 * j], ws_ref[2 * j + 1]], axis=1).astype(_BF16)
            bs_ref[j] = jnp.where(lo, bst[:, 2 * j:2 * j + 1], bst[:, 2 * j + 1:2 * j + 2])

    sub = x_ref.shape[0] // n_sub
    n_qkv = 3 * d_attn
    pair = 2 * SGU_CHUNK
    assert sub % pair == 0

    def normed(t):
        return _rms_norm(x_ref[t * sub:(t + 1) * sub, :], g_ref[...]).astype(_BF16)

    def gating(t, zg):
        zg = jax.nn.gelu(zg)
        u = zg[:, :d_sgu]
        v = zg[:, d_sgu:]
        mu = jnp.mean(v, axis=-1, keepdims=True)
        vc = v - mu
        vn = vc * lax.rsqrt(jnp.mean(vc * vc, axis=-1, keepdims=True) + LN_EPS) * lng_ref[...] + lnb_ref[...]
        for c0 in range(0, sub, pair):
            for j in range(d_sgu // LANES):
                cols = slice(j * LANES, (j + 1) * LANES)
                stacked = []
                for c in (c0, c0 + SGU_CHUNK):
                    slab = vn[c:c + SGU_CHUNK, cols]
                    stacked.append(jnp.concatenate([jnp.where(lo, slab, 0.0), jnp.where(lo, 0.0, slab)], axis=0))
                rhs = jnp.concatenate(stacked, axis=1).astype(_BF16)
                mixed = jnp.dot(wcat_ref[j], rhs, preferred_element_type=_F32)
                for k, c in enumerate((c0, c0 + SGU_CHUNK)):
                    m = mixed[:, k * LANES:(k + 1) * LANES] + bs_ref[j]
                    rows = slice(t * sub + c, t * sub + c + SGU_CHUNK)
                    sgu_ref[rows, cols] = (u[c:c + SGU_CHUNK, cols] * m).astype(_BF16)

    h = normed(0)
    pending = None
    for t in range(n_sub):
        h_next = normed(t + 1) if t + 1 < n_sub else None
        rows = slice(t * sub, (t + 1) * sub)
        z = jnp.dot(h, w_ref[:, :n_qkv], preferred_element_type=_F32)
        qkv_ref[rows, :d_attn] = z[:, :d_attn] * (HEAD_DIM ** -0.5 * LOG2E)
        qkv_ref[rows, d_attn:] = z[:, d_attn:]
        zg = jnp.dot(h, w_ref[:, n_qkv:], preferred_element_type=_F32)
        if t == 0:
            for src_ref, dst_ref in zip(later_f32, later_bf16):
                dst_ref[...] = src_ref[...].astype(_BF16)
        if pending is not None:
            gating(*pending)
        pending = (t, zg)
        h = h_next
    gating(*pending)


def _inproj_call(x2d, g_pre_mix, w_in, sgu_ln_g, sgu_ln_b, sgu_w, sgu_b, later_weights, *, d_attn, d_sgu, tm):
    n_tok, d_model = x2d.shape
    n_steps = n_tok // tm
    row_block = lambda w: pl.BlockSpec((w.shape[0] // n_steps, w.shape[1]), lambda i: (i, 0))
    assert all(w.shape[0] % (16 * n_steps) == 0 for w in later_weights)
    n_slab = d_sgu // LANES
    n_groups = sgu_w.shape[0]
    assert n_groups == 2 * n_slab
    d_in = w_in.shape[1]
    const = lambda *shape: pl.BlockSpec(shape, lambda i: (0,) * len(shape))
    return pl.pallas_call(
        functools.partial(_inproj_kernel, d_attn=d_attn, d_sgu=d_sgu, n_sub=tm // 256,
                          n_later=len(later_weights)),
        grid=(n_steps,),
        in_specs=[
            pl.BlockSpec((tm, d_model), lambda i: (i, 0)),
            const(1, d_model),
            pl.BlockSpec((d_model, d_in), lambda i: (0, 0), pipeline_mode=pl.Buffered(1)),
            const(1, d_sgu),
            const(1, d_sgu),
            const(n_groups, SGU_CHUNK, SGU_CHUNK),
            const(n_groups, SGU_CHUNK),
        ] + [row_block(w) for w in later_weights],
        out_specs=[
            pl.BlockSpec((tm, 3 * d_attn), lambda i: (i, 0)),
            pl.BlockSpec((tm, d_sgu), lambda i: (i, 0)),
        ] + [row_block(w) for w in later_weights],
        out_shape=[
            jax.ShapeDtypeStruct((n_tok, 3 * d_attn), _F32),
            jax.ShapeDtypeStruct((n_tok, d_sgu), _BF16),
        ] + [jax.ShapeDtypeStruct(w.shape, _BF16) for w in later_weights],
        scratch_shapes=[
            pltpu.VMEM((d_model, d_in), _BF16),
            pltpu.VMEM((n_slab, SGU_CHUNK, 2 * SGU_CHUNK), _BF16),
            pltpu.VMEM((n_slab, SGU_CHUNK, LANES), _F32),
        ],
        compiler_params=pltpu.CompilerParams(
            dimension_semantics=("arbitrary",), vmem_limit_bytes=VMEM_LIMIT_BYTES),
        name="inproj_sgu",
    )(x2d, g_pre_mix.reshape(1, -1), w_in, sgu_ln_g.reshape(1, -1),
      sgu_ln_b.reshape(1, -1), sgu_w, sgu_b, *later_weights)


def _t5_bucket(rel):
    half = N_REL_BUCKETS // 2
    max_exact = half // 2
    ret = jnp.where(rel > 0, half, 0)
    n = jnp.abs(rel)
    nf = jnp.maximum(n, 1).astype(jnp.float32)
    large = max_exact + (jnp.log(nf / max_exact) / math.log(REL_MAX_DISTANCE / max_exact)
                         * (half - max_exact)).astype(jnp.int32)
    large = jnp.minimum(large, half - 1)
    return ret + jnp.where(n < max_exact, n, large)


def _bucket_tables():
    rel = jnp.arange(K_TILE) - HALF_WINDOW
    tabs = [jnp.where(jnp.abs(rel) <= HALF_WINDOW, _t5_bucket(rel * dil), -1)
            for _, dil in DILATED_BRANCHES]
    return jnp.stack(tabs)[:, None, :].astype(jnp.int32)


def _attn_kernel(relb_ref, bucket_ref, q_ref, k_ref, v_ref, o_ref,
                 kp_ref, vp_ref, perm_ref, bias_ref, p_ref, ob_ref, mb_ref, lb_ref, *, seq):
    hp = pl.program_id(0)
    n_br = len(DILATED_BRANCHES)
    n_tiles_total = seq // Q_TILE
    assert n_tiles_total % 2 == 0
    d1 = DILATED_BRANCHES[1][1]
    assert [d for _, d in DILATED_BRANCHES] == [1, d1, d1 * d1]
    cls_len = seq // d1
    lane_q = lax.broadcasted_iota(jnp.int32, (Q_TILE, LANES), 1) < HEAD_DIM
    col = lax.broadcasted_iota(jnp.int32, (Q_TILE, K_TILE), 1)

    @pl.when(pl.program_id(1) == 0)
    def _():
        for i in range(n_br):
            bucket = bucket_ref[i]
            for hh in range(2):
                row = jnp.full(bucket.shape, relb_ref[2 * hp + hh, 0] * LOG2E, _F32)
                for b in range(1, N_REL_BUCKETS):
                    row = jnp.where(bucket >= b, relb_ref[2 * hp + hh, b] * LOG2E, row)
                row = jnp.where(bucket < 0, NEG_INF, row)
                tile = pltpu.roll(jnp.broadcast_to(row, (Q_TILE, K_TILE)), 0, 1, stride=1, stride_axis=0)
                rows = slice(hh * Q_TILE, (hh + 1) * Q_TILE)
                bias_ref[i, 0, rows, :] = jnp.where(col < HALF_WINDOW, NEG_INF, tile)
                bias_ref[i, 1, rows, :] = tile
                bias_ref[i, 2, rows, :] = jnp.where(col >= K_TILE - HALF_WINDOW, NEG_INF, tile)

    zero_pad = jnp.zeros((HALF_WINDOW, LANES), _BF16)
    for i in range(n_br):
        for ref in (kp_ref, vp_ref):
            ref[i, :HALF_WINDOW, :] = zero_pad
            ref[i, HALF_WINDOW + seq:, :] = zero_pad
    for c0 in range(0, seq, COPY_ROWS):
        kp_ref[0, pl.ds(HALF_WINDOW + c0, COPY_ROWS), :] = k_ref[pl.ds(c0, COPY_ROWS), :].astype(_BF16)
        vp_ref[0, pl.ds(HALF_WINDOW + c0, COPY_ROWS), :] = v_ref[pl.ds(c0, COPY_ROWS), :].astype(_BF16)
    for r in range(d1):
        for c0 in range(0, cls_len, COPY_ROWS):
            src = pl.ds(r + d1 * c0, COPY_ROWS, stride=d1)
            dst = pl.ds(r * cls_len + c0, COPY_ROWS)
            padded = pl.ds(HALF_WINDOW + r * cls_len + c0, COPY_ROWS)
            perm_ref[0, dst, :] = q_ref[src, :]
            for a, (x_ref, xp_ref) in enumerate(((k_ref, kp_ref), (v_ref, vp_ref)), start=1):
                x = x_ref[src, :]
                perm_ref[a, dst, :] = x
                xp_ref[1, padded, :] = x.astype(_BF16)
    sub_len = cls_len // d1
    for r2 in range(d1 * d1):
        src = pl.ds((r2 % d1) * cls_len + r2 // d1, sub_len, stride=d1)
        padded = pl.ds(HALF_WINDOW + r2 * sub_len, sub_len)
        kp_ref[2, padded, :] = perm_ref[1, src, :].astype(_BF16)
        vp_ref[2, padded, :] = perm_ref[2, src, :].astype(_BF16)

    def tile_geometry(i, j):
        n_tiles = seq // (DILATED_BRANCHES[i][1] * Q_TILE)
        n = j % n_tiles
        variant = 0 if n == 0 else (2 if n == n_tiles - 1 else 1)
        if i < 2:
            rows = pl.ds(j * Q_TILE, Q_TILE)
        else:
            r2 = j // n_tiles
            rows = pl.ds((r2 % d1) * cls_len + r2 // d1 + d1 * Q_TILE * n, Q_TILE, stride=d1)
        return rows, pl.ds(j * Q_TILE, K_TILE), variant

    def stage_scores(i, j, slot):
        rows, win, variant = tile_geometry(i, j)
        q = q_ref[rows, :] if i == 0 else perm_ref[0, rows, :]
        q2 = jnp.concatenate([jnp.where(lane_q, q, 0.0), jnp.where(lane_q, 0.0, q)], axis=0)
        s = lax.dot_general(q2.astype(_BF16), kp_ref[i, win, :], (((1,), (1,)), ((), ())),
                            preferred_element_type=_F32)
        s = s + bias_ref[i, variant]
        m2 = jnp.max(s, axis=-1, keepdims=True)
        p_ref[slot] = jnp.exp2(s - m2).astype(_BF16)
        mb_ref[i, rows, :] = jnp.where(lane_q, m2[:Q_TILE], m2[Q_TILE:])

    ones = jnp.ones((K_TILE, LANES), _BF16)

    def stage_values(i, j, slot):
        rows, win, _ = tile_geometry(i, j)
        v1 = jnp.concatenate([vp_ref[i, win, :], ones], axis=1)
        pv = jnp.dot(p_ref[slot], v1, preferred_element_type=_F32)
        ob_ref[i, rows, :] = jnp.where(lane_q, pv[:Q_TILE, :LANES], pv[Q_TILE:, :LANES])
        lb_ref[i, rows, :] = jnp.where(lane_q, pv[:Q_TILE, LANES:], pv[Q_TILE:, LANES:])

    stage_scores(0, 0, 0)
    for i in range(n_br):
        for j in range(n_tiles_total):
            if j + 1 < n_tiles_total:
                stage_scores(i, j + 1, (j + 1) % 2)
            elif i + 1 < n_br:
                stage_scores(i + 1, 0, 0)
            stage_values(i, j, j % 2)

    blocks_per_class = cls_len // MERGE_ROWS

    def merge_body(t, carry):
        cls_rows = pl.ds(pl.multiple_of(t * MERGE_ROWS, MERGE_ROWS), MERGE_ROWS)
        nat_rows = pl.ds(t // blocks_per_class + d1 * MERGE_ROWS * (t % blocks_per_class), MERGE_ROWS, stride=d1)
        rows = [nat_rows] + [cls_rows] * (n_br - 1)
        ms = [mb_ref[i, rows[i], :] for i in range(n_br)]
        m = functools.reduce(jnp.maximum, ms)
        ws = [jnp.exp2(m_i - m) for m_i in ms]
        num = functools.reduce(jnp.add, [w * ob_ref[i, rows[i], :] for i, w in enumerate(ws)])
        den = functools.reduce(jnp.add, [w * lb_ref[i, rows[i], :] for i, w in enumerate(ws)])
        perm_ref[1, nat_rows, :] = num / den
        return carry

    lax.fori_loop(0, seq // MERGE_ROWS, merge_body, 0, unroll=2)
    for c0 in range(0, seq, COPY_ROWS):
        o_ref[pl.ds(c0, COPY_ROWS), :] = perm_ref[1, pl.ds(c0, COPY_ROWS), :].astype(o_ref.dtype)


def _attn_call(qkv, rel_bias, *, batch, seq, d_attn):
    n_pairs = d_attn // LANES
    qkv3 = qkv.reshape(batch, seq, 3 * d_attn)
    slab = lambda off: pl.BlockSpec((None, seq, LANES), lambda hp, b: (b, 0, off + hp))
    n_br = len(DILATED_BRANCHES)
    return pl.pallas_call(
        functools.partial(_attn_kernel, seq=seq),
        grid=(n_pairs, batch),
        in_specs=[
            pl.BlockSpec(memory_space=pltpu.SMEM),
            pl.BlockSpec((n_br, 1, K_TILE), lambda hp, b: (0, 0, 0)),
            slab(0), slab(n_pairs), slab(2 * n_pairs),
        ],
        out_specs=pl.BlockSpec((None, seq, LANES), lambda hp, b: (b, 0, hp)),
        out_shape=jax.ShapeDtypeStruct((batch, seq, d_attn), _BF16),
        scratch_shapes=[
            pltpu.VMEM((n_br, seq + 2 * HALF_WINDOW, LANES), _BF16),
            pltpu.VMEM((n_br, seq + 2 * HALF_WINDOW, LANES), _BF16),
            pltpu.VMEM((3, seq, LANES), _F32),
            pltpu.VMEM((n_br, 3, 2 * Q_TILE, K_TILE), _F32),
            pltpu.VMEM((2, 2 * Q_TILE, K_TILE), _BF16),
            pltpu.VMEM((n_br, seq, LANES), _F32),
            pltpu.VMEM((n_br, seq, LANES), _F32),
            pltpu.VMEM((n_br, seq, LANES), _F32),
        ],
        compiler_params=pltpu.CompilerParams(
            dimension_semantics=("arbitrary", "arbitrary"), vmem_limit_bytes=VMEM_LIMIT_BYTES),
        name="dilated_attn",
    )(rel_bias.T, _bucket_tables(), qkv3, qkv3, qkv3)


def _outproj_ffn_kernel(attn_ref, sgu_ref, x_ref, wo_ref, g1_ref, g2_ref, w1_ref, w2_ref, g3_ref,
                        o_ref, *, d_attn, ff_chunk, n_sub):
    sub = x_ref.shape[0] // n_sub
    rows = [slice(t * sub, (t + 1) * sub) for t in range(n_sub)]
    chunks = range(0, w1_ref.shape[1], ff_chunk)

    def out_proj(t):
        return (jnp.dot(attn_ref[rows[t], :], wo_ref[:d_attn, :], preferred_element_type=_F32)
                + jnp.dot(sgu_ref[rows[t], :], wo_ref[d_attn:, :], preferred_element_type=_F32))

    def mid_norms(t, mix):
        x1 = x_ref[rows[t], :] + _rms_norm(mix, g1_ref[...])
        return x1, _rms_norm(x1, g2_ref[...]).astype(_BF16)

    def ffn_chunk(h, f, c0):
        a = jnp.maximum(jnp.dot(h, w1_ref[:, c0:c0 + ff_chunk], preferred_element_type=_F32), 0.0)
        part = jnp.dot((a * a).astype(_BF16), w2_ref[c0:c0 + ff_chunk, :], preferred_element_type=_F32)
        return part if f is None else f + part

    mix = out_proj(0)
    prev = None
    for t in range(n_sub):
        nxt_mix = out_proj(t + 1) if t + 1 < n_sub else None
        x1, h = mid_norms(t, mix)
        f = None
        for ci, c0 in enumerate(chunks):
            f = ffn_chunk(h, f, c0)
            if ci == 0 and prev is not None:
                pt, px1, pf = prev
                o_ref[rows[pt], :] = px1 + _rms_norm(pf, g3_ref[...])
        prev = (t, x1, f)
        mix = nxt_mix
    pt, px1, pf = prev
    o_ref[rows[pt], :] = px1 + _rms_norm(pf, g3_ref[...])


def _outproj_ffn_call(attn2d, sgu2d, x2d, w_out, g_post_mix, g_pre_ffn, w_ff1, w_ff2, g_post_ffn, *, tm):
    n_tok, d_model = x2d.shape
    d_attn = attn2d.shape[1]
    d_sgu = sgu2d.shape[1]
    d_ff = w_ff1.shape[1]
    const = lambda *shape: pl.BlockSpec(shape, lambda i: (0,) * len(shape), pipeline_mode=pl.Buffered(1))
    return pl.pallas_call(
        functools.partial(_outproj_ffn_kernel, d_attn=d_attn, ff_chunk=1024, n_sub=4),
        grid=(n_tok // tm,),
        in_specs=[
            pl.BlockSpec((tm, d_attn), lambda i: (i, 0)),
            pl.BlockSpec((tm, d_sgu), lambda i: (i, 0)),
            pl.BlockSpec((tm, d_model), lambda i: (i, 0)),
            const(d_attn + d_sgu, d_model),
            const(1, d_model),
            const(1, d_model),
            const(d_model, d_ff),
            const(d_ff, d_model),
            const(1, d_model),
        ],
        out_specs=pl.BlockSpec((tm, d_model), lambda i: (i, 0)),
        out_shape=jax.ShapeDtypeStruct((n_tok, d_model), _F32),
        compiler_params=pltpu.CompilerParams(
            dimension_semantics=("arbitrary",), vmem_limit_bytes=VMEM_LIMIT_BYTES),
        name="outproj_ffn",
    )(attn2d, sgu2d, x2d, w_out, g_post_mix.reshape(1, -1), g_pre_ffn.reshape(1, -1),
      w_ff1, w_ff2, g_post_ffn.reshape(1, -1))


def kernel(x, g_pre_mix, w_in, sgu_ln_g, sgu_ln_b, sgu_w, sgu_b, w_out, g_post_mix, g_pre_ffn,
           w_ff1, w_ff2, g_post_ffn, rel_bias):
    batch, seq, d_model = x.shape
    d_sgu = sgu_ln_g.shape[-1]
    d_attn = w_out.shape[1] - d_sgu
    assert d_attn % LANES == 0 and d_sgu % LANES == 0 and sgu_w.shape[-1] == SGU_CHUNK
    assert all(seq % (dil * Q_TILE) == 0 and win == 2 * HALF_WINDOW * dil for win, dil in DILATED_BRANCHES)
    x2d = x.reshape(batch * seq, d_model)
    for layer in range(g_pre_mix.shape[0]):
        qkv, sgu, w_out_bf, w_ff1_bf, w_ff2_bf = _inproj_call(
            x2d, g_pre_mix[layer], w_in[layer], sgu_ln_g[layer], sgu_ln_b[layer], sgu_w[layer], sgu_b[layer],
            (w_out[layer], w_ff1[layer], w_ff2[layer]), d_attn=d_attn, d_sgu=d_sgu, tm=1024)
        attn = _attn_call(qkv, rel_bias, batch=batch, seq=seq, d_attn=d_attn)
        x2d = _outproj_ffn_call(attn.reshape(batch * seq, d_attn), sgu, x2d, w_out_bf,
                                g_post_mix[layer], g_pre_ffn[layer], w_ff1_bf, w_ff2_bf,
                                g_post_ffn[layer], tm=1024)
    return x2d.reshape(batch, seq, d_model)
```

```python
import functools
import math

import jax
import jax.numpy as jnp
from jax import lax
from jax.experimental import pallas as pl
from jax.experimental.pallas import tpu as pltpu

HEAD_DIM = 64
DILATED_BRANCHES = ((128, 1), (512, 4), (2048, 16))
SGU_CHUNK = 128
N_REL_BUCKETS = 32
REL_MAX_DISTANCE = 1024
RMS_EPS = 1e-6
LN_EPS = 1e-5
NEG_INF = -1e30

LANES = 128
HALF_WINDOW = 64
Q_TILE = 128
K_TILE = Q_TILE + 2 * HALF_WINDOW
MERGE_ROWS = 256
COPY_ROWS = 512
CAST_COLS = 512
TOKEN_TILE = 1024
SUB_ROWS = 256
FF_CHUNK = 1024
LOG2E = math.log2(math.e)
V7X_VMEM_BYTES = 64 * 1024 * 1024
VMEM_LIMIT_BYTES = V7X_VMEM_BYTES - 8 * 1024 * 1024

_F32 = jnp.float32
_BF16 = jnp.bfloat16


def _rms_norm(x, g):
    return x * lax.rsqrt(jnp.mean(x * x, axis=-1, keepdims=True) + RMS_EPS) * g


def _inproj_kernel(x_ref, g_ref, w32_ref, lng_ref, lnb_ref, ws_ref, bsg_ref, *rest,
                   d_attn, d_sgu, n_sub, n_later):
    later_f32, (qkv_ref, sgu_ref), later_bf16 = rest[:n_later], rest[n_later:n_later + 2], rest[n_later + 2:-3]
    w_ref, wcat_ref, bs_ref = rest[-3:]
    lane = lax.broadcasted_iota(jnp.int32, (SGU_CHUNK, LANES), 1)
    lo = lane < HEAD_DIM

    @pl.when(pl.program_id(0) == 0)
    def _():
        for c0 in range(0, w_ref.shape[1], CAST_COLS):
            w_ref[:, c0:c0 + CAST_COLS] = w32_ref[:, c0:c0 + CAST_COLS].astype(_BF16)
        bst = bsg_ref[...].T
        for j in range(d_sgu // LANES):
            wcat_ref[j] = jnp.concatenate([ws_ref[2 * j], ws_ref[2 * j + 1]], axis=1).astype(_BF16)
            bs_ref[j] = jnp.where(lo, bst[:, 2 * j:2 * j + 1], bst[:, 2 * j + 1:2 * j + 2])

    sub = x_ref.shape[0] // n_sub
    n_qkv = 3 * d_attn
    pair = 2 * SGU_CHUNK
    assert sub % pair == 0

    def normed(t):
        return _rms_norm(x_ref[t * sub:(t + 1) * sub, :], g_ref[...]).astype(_BF16)

    def gating(t, zg):
        zg = jax.nn.gelu(zg)
        u = zg[:, :d_sgu]
        v = zg[:, d_sgu:]
        mu = jnp.mean(v, axis=-1, keepdims=True)
        vc = v - mu
        vn = vc * lax.rsqrt(jnp.mean(vc * vc, axis=-1, keepdims=True) + LN_EPS) * lng_ref[...] + lnb_ref[...]
        for c0 in range(0, sub, pair):
            for j in range(d_sgu // LANES):
                cols = slice(j * LANES, (j + 1) * LANES)
                stacked = []
                for c in (c0, c0 + SGU_CHUNK):
                    slab = vn[c:c + SGU_CHUNK, cols]
                    stacked.append(jnp.concatenate([jnp.where(lo, slab, 0.0), jnp.where(lo, 0.0, slab)], axis=0))
                rhs = jnp.concatenate(stacked, axis=1).astype(_BF16)
                mixed = jnp.dot(wcat_ref[j], rhs, preferred_element_type=_F32)
                for k, c in enumerate((c0, c0 + SGU_CHUNK)):
                    m = mixed[:, k * LANES:(k + 1) * LANES] + bs_ref[j]
                    rows = slice(t * sub + c, t * sub + c + SGU_CHUNK)
                    sgu_ref[rows, cols] = (u[c:c + SGU_CHUNK, cols] * m).astype(_BF16)

    h = normed(0)
    pending = None
    for t in range(n_sub):
        h_next = normed(t + 1) if t + 1 < n_sub else None
        rows = slice(t * sub, (t + 1) * sub)
        z = jnp.dot(h, w_ref[:, :n_qkv], preferred_element_type=_F32)
        qkv_ref[rows, :d_attn] = z[:, :d_attn] * (HEAD_DIM ** -0.5 * LOG2E)
        qkv_ref[rows, d_attn:] = z[:, d_attn:]
        zg = jnp.dot(h, w_ref[:, n_qkv:], preferred_element_type=_F32)
        if t == 0:
            for src_ref, dst_ref in zip(later_f32, later_bf16):
                dst_ref[...] = src_ref[...].astype(_BF16)
        if pending is not None:
            gating(*pending)
        pending = (t, zg)
        h = h_next
    gating(*pending)


def _inproj_call(x2d, g_pre_mix, w_in, sgu_ln_g, sgu_ln_b, sgu_w, sgu_b, later_weights, *, d_attn, d_sgu, tm):
    n_tok, d_model = x2d.shape
    n_steps = n_tok // tm
    row_block = lambda w: pl.BlockSpec((w.shape[0] // n_steps, w.shape[1]), lambda i: (i, 0))
    assert all(w.shape[0] % (16 * n_steps) == 0 for w in later_weights)
    n_slab = d_sgu // LANES
    n_groups = sgu_w.shape[0]
    assert n_groups == 2 * n_slab
    d_in = w_in.shape[1]
    const = lambda *shape: pl.BlockSpec(shape, lambda i: (0,) * len(shape))
    return pl.pallas_call(
        functools.partial(_inproj_kernel, d_attn=d_attn, d_sgu=d_sgu, n_sub=tm // SUB_ROWS,
                          n_later=len(later_weights)),
        grid=(n_steps,),
        in_specs=[
            pl.BlockSpec((tm, d_model), lambda i: (i, 0)),
            const(1, d_model),
            pl.BlockSpec((d_model, d_in), lambda i: (0, 0), pipeline_mode=pl.Buffered(1)),
            const(1, d_sgu),
            const(1, d_sgu),
            const(n_groups, SGU_CHUNK, SGU_CHUNK),
            const(n_groups, SGU_CHUNK),
        ] + [row_block(w) for w in later_weights],
        out_specs=[
            pl.BlockSpec((tm, 3 * d_attn), lambda i: (i, 0)),
            pl.BlockSpec((tm, d_sgu), lambda i: (i, 0)),
        ] + [row_block(w) for w in later_weights],
        out_shape=[
            jax.ShapeDtypeStruct((n_tok, 3 * d_attn), _F32),
            jax.ShapeDtypeStruct((n_tok, d_sgu), _BF16),
        ] + [jax.ShapeDtypeStruct(w.shape, _BF16) for w in later_weights],
        scratch_shapes=[
            pltpu.VMEM((d_model, d_in), _BF16),
            pltpu.VMEM((n_slab, SGU_CHUNK, 2 * SGU_CHUNK), _BF16),
            pltpu.VMEM((n_slab, SGU_CHUNK, LANES), _F32),
        ],
        compiler_params=pltpu.CompilerParams(
            dimension_semantics=("arbitrary",), vmem_limit_bytes=VMEM_LIMIT_BYTES),
        name="inproj_sgu",
    )(x2d, g_pre_mix.reshape(1, -1), w_in, sgu_ln_g.reshape(1, -1),
      sgu_ln_b.reshape(1, -1), sgu_w, sgu_b, *later_weights)


def _t5_bucket(rel):
    half = N_REL_BUCKETS // 2
    max_exact = half // 2
    ret = jnp.where(rel > 0, half, 0)
    n = jnp.abs(rel)
    nf = jnp.maximum(n, 1).astype(jnp.float32)
    large = max_exact + (jnp.log(nf / max_exact) / math.log(REL_MAX_DISTANCE / max_exact)
                         * (half - max_exact)).astype(jnp.int32)
    large = jnp.minimum(large, half - 1)
    return ret + jnp.where(n < max_exact, n, large)


def _bucket_tables():
    rel = jnp.arange(K_TILE) - HALF_WINDOW
    tabs = [jnp.where(jnp.abs(rel) <= HALF_WINDOW, _t5_bucket(rel * dil), -1)
            for _, dil in DILATED_BRANCHES]
    return jnp.stack(tabs)[:, None, :].astype(jnp.int32)


def _attn_kernel(relb_ref, bucket_ref, q_ref, k_ref, v_ref, o_ref,
                 kp_ref, vp_ref, perm_ref, bias_ref, p_ref, ob_ref, mb_ref, lb_ref, *, seq):
    hp = pl.program_id(0)
    n_br = len(DILATED_BRANCHES)
    n_tiles_total = seq // Q_TILE
    assert n_tiles_total % 2 == 0
    d1 = DILATED_BRANCHES[1][1]
    assert [d for _, d in DILATED_BRANCHES] == [1, d1, d1 * d1]
    cls_len = seq // d1
    lane_q = lax.broadcasted_iota(jnp.int32, (Q_TILE, LANES), 1) < HEAD_DIM
    col = lax.broadcasted_iota(jnp.int32, (Q_TILE, K_TILE), 1)

    @pl.when(pl.program_id(1) == 0)
    def _():
        for i in range(n_br):
            bucket = bucket_ref[i]
            for hh in range(2):
                row = jnp.full(bucket.shape, relb_ref[2 * hp + hh, 0] * LOG2E, _F32)
                for b in range(1, N_REL_BUCKETS):
                    row = jnp.where(bucket >= b, relb_ref[2 * hp + hh, b] * LOG2E, row)
                row = jnp.where(bucket < 0, NEG_INF, row)
                tile = pltpu.roll(jnp.broadcast_to(row, (Q_TILE, K_TILE)), 0, 1, stride=1, stride_axis=0)
                rows = slice(hh * Q_TILE, (hh + 1) * Q_TILE)
                bias_ref[i, 0, rows, :] = jnp.where(col < HALF_WINDOW, NEG_INF, tile)
                bias_ref[i, 1, rows, :] = tile
                bias_ref[i, 2, rows, :] = jnp.where(col >= K_TILE - HALF_WINDOW, NEG_INF, tile)

    zero_pad = jnp.zeros((HALF_WINDOW, LANES), _BF16)
    for i in range(n_br):
        for ref in (kp_ref, vp_ref):
            ref[i, :HALF_WINDOW, :] = zero_pad
            ref[i, HALF_WINDOW + seq:, :] = zero_pad
    for c0 in range(0, seq, COPY_ROWS):
        kp_ref[0, pl.ds(HALF_WINDOW + c0, COPY_ROWS), :] = k_ref[pl.ds(c0, COPY_ROWS), :].astype(_BF16)
        vp_ref[0, pl.ds(HALF_WINDOW + c0, COPY_ROWS), :] = v_ref[pl.ds(c0, COPY_ROWS), :].astype(_BF16)
    for r in range(d1):
        for c0 in range(0, cls_len, COPY_ROWS):
            src = pl.ds(r + d1 * c0, COPY_ROWS, stride=d1)
            dst = pl.ds(r * cls_len + c0, COPY_ROWS)
            padded = pl.ds(HALF_WINDOW + r * cls_len + c0, COPY_ROWS)
            perm_ref[0, dst, :] = q_ref[src, :]
            for a, (x_ref, xp_ref) in enumerate(((k_ref, kp_ref), (v_ref, vp_ref)), start=1):
                x = x_ref[src, :]
                perm_ref[a, dst, :] = x
                xp_ref[1, padded, :] = x.astype(_BF16)
    sub_len = cls_len // d1
    for r2 in range(d1 * d1):
        src = pl.ds((r2 % d1) * cls_len + r2 // d1, sub_len, stride=d1)
        padded = pl.ds(HALF_WINDOW + r2 * sub_len, sub_len)
        kp_ref[2, padded, :] = perm_ref[1, src, :].astype(_BF16)
        vp_ref[2, padded, :] = perm_ref[2, src, :].astype(_BF16)

    def tile_geometry(i, j):
        n_tiles = seq // (DILATED_BRANCHES[i][1] * Q_TILE)
        n = j % n_tiles
        variant = 0 if n == 0 else (2 if n == n_tiles - 1 else 1)
        if i < 2:
            rows = pl.ds(j * Q_TILE, Q_TILE)
        else:
            r2 = j // n_tiles
            rows = pl.ds((r2 % d1) * cls_len + r2 // d1 + d1 * Q_TILE * n, Q_TILE, stride=d1)
        return rows, pl.ds(j * Q_TILE, K_TILE), variant

    def stage_scores(i, j, slot):
        rows, win, variant = tile_geometry(i, j)
        q = q_ref[rows, :] if i == 0 else perm_ref[0, rows, :]
        q2 = jnp.concatenate([jnp.where(lane_q, q, 0.0), jnp.where(lane_q, 0.0, q)], axis=0)
        s = lax.dot_general(q2.astype(_BF16), kp_ref[i, win, :], (((1,), (1,)), ((), ())),
                            preferred_element_type=_F32)
        s = s + bias_ref[i, variant]
        m2 = jnp.max(s, axis=-1, keepdims=True)
        p_ref[slot] = jnp.exp2(s - m2).astype(_BF16)
        mb_ref[i, rows, :] = jnp.where(lane_q, m2[:Q_TILE], m2[Q_TILE:])

    ones = jnp.ones((K_TILE, LANES), _BF16)

    def stage_values(i, j, slot):
        rows, win, _ = tile_geometry(i, j)
        v1 = jnp.concatenate([vp_ref[i, win, :], ones], axis=1)
        pv = jnp.dot(p_ref[slot], v1, preferred_element_type=_F32)
        ob_ref[i, rows, :] = jnp.where(lane_q, pv[:Q_TILE, :LANES], pv[Q_TILE:, :LANES])
        lb_ref[i, rows, :] = jnp.where(lane_q, pv[:Q_TILE, LANES:], pv[Q_TILE:, LANES:])

    stage_scores(0, 0, 0)
    for i in range(n_br):
        for j in range(n_tiles_total):
            if j + 1 < n_tiles_total:
                stage_scores(i, j + 1, (j + 1) % 2)
            elif i + 1 < n_br:
                stage_scores(i + 1, 0, 0)
            stage_values(i, j, j % 2)

    blocks_per_class = cls_len // MERGE_ROWS

    def merge_body(t, carry):
        cls_rows = pl.ds(pl.multiple_of(t * MERGE_ROWS, MERGE_ROWS), MERGE_ROWS)
        nat_rows = pl.ds(t // blocks_per_class + d1 * MERGE_ROWS * (t % blocks_per_class), MERGE_ROWS, stride=d1)
        rows = [nat_rows] + [cls_rows] * (n_br - 1)
        ms = [mb_ref[i, rows[i], :] for i in range(n_br)]
        m = functools.reduce(jnp.maximum, ms)
        ws = [jnp.exp2(m_i - m) for m_i in ms]
        num = functools.reduce(jnp.add, [w * ob_ref[i, rows[i], :] for i, w in enumerate(ws)])
        den = functools.reduce(jnp.add, [w * lb_ref[i, rows[i], :] for i, w in enumerate(ws)])
        perm_ref[1, nat_rows, :] = num / den
        return carry

    lax.fori_loop(0, seq // MERGE_ROWS, merge_body, 0, unroll=2)
    for c0 in range(0, seq, COPY_ROWS):
        o_ref[pl.ds(c0, COPY_ROWS), :] = perm_ref[1, pl.ds(c0, COPY_ROWS), :].astype(o_ref.dtype)


def _attn_call(qkv, rel_bias, *, batch, seq, d_attn):
    n_pairs = d_attn // LANES
    qkv3 = qkv.reshape(batch, seq, 3 * d_attn)
    slab = lambda off: pl.BlockSpec((None, seq, LANES), lambda hp, b: (b, 0, off + hp))
    n_br = len(DILATED_BRANCHES)
    return pl.pallas_call(
        functools.partial(_attn_kernel, seq=seq),
        grid=(n_pairs, batch),
        in_specs=[
            pl.BlockSpec(memory_space=pltpu.SMEM),
            pl.BlockSpec((n_br, 1, K_TILE), lambda hp, b: (0, 0, 0)),
            slab(0), slab(n_pairs), slab(2 * n_pairs),
        ],
        out_specs=pl.BlockSpec((None, seq, LANES), lambda hp, b: (b, 0, hp)),
        out_shape=jax.ShapeDtypeStruct((batch, seq, d_attn), _BF16),
        scratch_shapes=[
            pltpu.VMEM((n_br, seq + 2 * HALF_WINDOW, LANES), _BF16),
            pltpu.VMEM((n_br, seq + 2 * HALF_WINDOW, LANES), _BF16),
            pltpu.VMEM((3, seq, LANES), _F32),
            pltpu.VMEM((n_br, 3, 2 * Q_TILE, K_TILE), _F32),
            pltpu.VMEM((2, 2 * Q_TILE, K_TILE), _BF16),
            pltpu.VMEM((n_br, seq, LANES), _F32),
            pltpu.VMEM((n_br, seq, LANES), _F32),
            pltpu.VMEM((n_br, seq, LANES), _F32),
        ],
        compiler_params=pltpu.CompilerParams(
            dimension_semantics=("arbitrary", "arbitrary"), vmem_limit_bytes=VMEM_LIMIT_BYTES),
        name="dilated_attn",
    )(rel_bias.T, _bucket_tables(), qkv3, qkv3, qkv3)


def _outproj_ffn_kernel(attn_ref, sgu_ref, x_ref, wo_ref, g1_ref, g2_ref, w1_ref, w2_ref, g3_ref,
                        o_ref, *, d_attn, ff_chunk, n_sub):
    sub = x_ref.shape[0] // n_sub
    rows = [slice(t * sub, (t + 1) * sub) for t in range(n_sub)]
    chunks = range(0, w1_ref.shape[1], ff_chunk)

    def out_proj(t):
        return (jnp.dot(attn_ref[rows[t], :], wo_ref[:d_attn, :], preferred_element_type=_F32)
                + jnp.dot(sgu_ref[rows[t], :], wo_ref[d_attn:, :], preferred_element_type=_F32))

    def mid_norms(t, mix):
        x1 = x_ref[rows[t], :] + _rms_norm(mix, g1_ref[...])
        return x1, _rms_norm(x1, g2_ref[...]).astype(_BF16)

    def ffn_chunk(h, f, c0):
        a = jnp.maximum(jnp.dot(h, w1_ref[:, c0:c0 + ff_chunk], preferred_element_type=_F32), 0.0)
        part = jnp.dot((a * a).astype(_BF16), w2_ref[c0:c0 + ff_chunk, :], preferred_element_type=_F32)
        return part if f is None else f + part

    mix = out_proj(0)
    prev = None
    for t in range(n_sub):
        nxt_mix = out_proj(t + 1) if t + 1 < n_sub else None
        x1, h = mid_norms(t, mix)
        f = None
        for ci, c0 in enumerate(chunks):
            f = ffn_chunk(h, f, c0)
            if ci == 0 and prev is not None:
                pt, px1, pf = prev
                o_ref[rows[pt], :] = px1 + _rms_norm(pf, g3_ref[...])
        prev = (t, x1, f)
        mix = nxt_mix
    pt, px1, pf = prev
    o_ref[rows[pt], :] = px1 + _rms_norm(pf, g3_ref[...])


def _outproj_ffn_call(attn2d, sgu2d, x2d, w_out, g_post_mix, g_pre_ffn, w_ff1, w_ff2, g_post_ffn, *, tm):
    n_tok, d_model = x2d.shape
    d_attn = attn2d.shape[1]
    d_sgu = sgu2d.shape[1]
    d_ff = w_ff1.shape[1]
    const = lambda *shape: pl.BlockSpec(shape, lambda i: (0,) * len(shape), pipeline_mode=pl.Buffered(1))
    return pl.pallas_call(
        functools.partial(_outproj_ffn_kernel, d_attn=d_attn, ff_chunk=FF_CHUNK, n_sub=tm // SUB_ROWS),
        grid=(n_tok // tm,),
        in_specs=[
            pl.BlockSpec((tm, d_attn), lambda i: (i, 0)),
            pl.BlockSpec((tm, d_sgu), lambda i: (i, 0)),
            pl.BlockSpec((tm, d_model), lambda i: (i, 0)),
            const(d_attn + d_sgu, d_model),
            const(1, d_model),
            const(1, d_model),
            const(d_model, d_ff),
            const(d_ff, d_model),
            const(1, d_model),
        ],
        out_specs=pl.BlockSpec((tm, d_model), lambda i: (i, 0)),
        out_shape=jax.ShapeDtypeStruct((n_tok, d_model), _F32),
        compiler_params=pltpu.CompilerParams(
            dimension_semantics=("arbitrary",), vmem_limit_bytes=VMEM_LIMIT_BYTES),
        name="outproj_ffn",
    )(attn2d, sgu2d, x2d, w_out, g_post_mix.reshape(1, -1), g_pre_ffn.reshape(1, -1),
      w_ff1, w_ff2, g_post_ffn.reshape(1, -1))


def kernel(x, g_pre_mix, w_in, sgu_ln_g, sgu_ln_b, sgu_w, sgu_b, w_out, g_post_mix, g_pre_ffn,
           w_ff1, w_ff2, g_post_ffn, rel_bias):
    batch, seq, d_model = x.shape
    d_sgu = sgu_ln_g.shape[-1]
    d_attn = w_out.shape[1] - d_sgu
    assert d_attn % LANES == 0 and d_sgu % LANES == 0 and sgu_w.shape[-1] == SGU_CHUNK
    assert all(seq % (dil * Q_TILE) == 0 and win == 2 * HALF_WINDOW * dil for win, dil in DILATED_BRANCHES)
    assert (batch * seq) % TOKEN_TILE == 0 and w_ff1.shape[-1] % FF_CHUNK == 0
    x2d = x.reshape(batch * seq, d_model)
    for layer in range(g_pre_mix.shape[0]):
        qkv, sgu, w_out_bf, w_ff1_bf, w_ff2_bf = _inproj_call(
            x2d, g_pre_mix[layer], w_in[layer], sgu_ln_g[layer], sgu_ln_b[layer], sgu_w[layer], sgu_b[layer],
            (w_out[layer], w_ff1[layer], w_ff2[layer]), d_attn=d_attn, d_sgu=d_sgu, tm=TOKEN_TILE)
        attn = _attn_call(qkv, rel_bias, batch=batch, seq=seq, d_attn=d_attn)
        x2d = _outproj_ffn_call(attn.reshape(batch * seq, d_attn), sgu, x2d, w_out_bf,
                                g_post_mix[layer], g_pre_ffn[layer], w_ff1_bf, w_ff2_bf,
                                g_post_ffn[layer], tm=TOKEN_TILE)
    return x2d.reshape(batch, seq, d_model)
```

```python
import functools
import math

import jax
import jax.numpy as jnp
from jax import lax
from jax.experimental import pallas as pl
from jax.experimental.pallas import tpu as pltpu

HEAD_DIM = 64
DILATED_BRANCHES = ((128, 1), (512, 4), (2048, 16))
SGU_CHUNK = 128
N_REL_BUCKETS = 32
REL_MAX_DISTANCE = 1024
RMS_EPS = 1e-6
LN_EPS = 1e-5
NEG_INF = -1e30

LANES = 128
HALF_WINDOW = 64
Q_TILE = 128
K_TILE = Q_TILE + 2 * HALF_WINDOW
MERGE_ROWS = 256
COPY_ROWS = 512
CAST_COLS = 512
TOKEN_TILE = 1024
SUB_ROWS = 256
FF_CHUNK = 1024
LOG2E = math.log2(math.e)
V7X_VMEM_BYTES = 64 * 1024 * 1024
VMEM_LIMIT_BYTES = V7X_VMEM_BYTES - 8 * 1024 * 1024

_F32 = jnp.float32
_BF16 = jnp.bfloat16


def _rms_norm(x, g):
    return x * lax.rsqrt(jnp.mean(x * x, axis=-1, keepdims=True) + RMS_EPS) * g


def _inproj_kernel(x_ref, g_ref, w32_ref, lng_ref, lnb_ref, ws_ref, bsg_ref, *rest,
                   d_attn, d_sgu, n_sub, n_later):
    later_f32, (qkv_ref, sgu_ref), later_bf16 = rest[:n_later], rest[n_later:n_later + 2], rest[n_later + 2:-3]
    w_ref, wcat_ref, bs_ref = rest[-3:]
    lane = lax.broadcasted_iota(jnp.int32, (SGU_CHUNK, LANES), 1)
    lo = lane < HEAD_DIM

    @pl.when(pl.program_id(0) == 0)
    def _():
        for c0 in range(0, w_ref.shape[1], CAST_COLS):
            w_ref[:, c0:c0 + CAST_COLS] = w32_ref[:, c0:c0 + CAST_COLS].astype(_BF16)
        bst = bsg_ref[...].T
        for j in range(d_sgu // LANES):
            wcat_ref[j] = jnp.concatenate([ws_ref[2 * j], ws_ref[2 * j + 1]], axis=1).astype(_BF16)
            bs_ref[j] = jnp.where(lo, bst[:, 2 * j:2 * j + 1], bst[:, 2 * j + 1:2 * j + 2])

    sub = x_ref.shape[0] // n_sub
    n_qkv = 3 * d_attn
    pair = 2 * SGU_CHUNK
    assert sub % pair == 0

    def normed(t):
        return _rms_norm(x_ref[t * sub:(t + 1) * sub, :], g_ref[...]).astype(_BF16)

    def gating(t, zg):
        zg = jax.nn.gelu(zg)
        u = zg[:, :d_sgu]
        v = zg[:, d_sgu:]
        mu = jnp.mean(v, axis=-1, keepdims=True)
        vc = v - mu
        vn = vc * lax.rsqrt(jnp.mean(vc * vc, axis=-1, keepdims=True) + LN_EPS) * lng_ref[...] + lnb_ref[...]
        for c0 in range(0, sub, pair):
            for j in range(d_sgu // LANES):
                cols = slice(j * LANES, (j + 1) * LANES)
                stacked = []
                for c in (c0, c0 + SGU_CHUNK):
                    slab = vn[c:c + SGU_CHUNK, cols]
                    stacked.append(jnp.concatenate([jnp.where(lo, slab, 0.0), jnp.where(lo, 0.0, slab)], axis=0))
                rhs = jnp.concatenate(stacked, axis=1).astype(_BF16)
                mixed = jnp.dot(wcat_ref[j], rhs, preferred_element_type=_F32)
                for k, c in enumerate((c0, c0 + SGU_CHUNK)):
                    m = mixed[:, k * LANES:(k + 1) * LANES] + bs_ref[j]
                    rows = slice(t * sub + c, t * sub + c + SGU_CHUNK)
                    sgu_ref[rows, cols] = (u[c:c + SGU_CHUNK, cols] * m).astype(_BF16)

    h = normed(0)
    for t in range(n_sub):
        h_next = normed(t + 1) if t + 1 < n_sub else None
        rows = slice(t * sub, (t + 1) * sub)
        zg = jnp.dot(h, w_ref[:, n_qkv:], preferred_element_type=_F32)
        z = jnp.dot(h, w_ref[:, :n_qkv], preferred_element_type=_F32)
        qkv_ref[rows, :d_attn] = z[:, :d_attn] * (HEAD_DIM ** -0.5 * LOG2E)
        qkv_ref[rows, d_attn:] = z[:, d_attn:]
        if t == 0:
            for src_ref, dst_ref in zip(later_f32, later_bf16):
                dst_ref[...] = src_ref[...].astype(_BF16)
        gating(t, zg)
        h = h_next


def _inproj_call(x2d, g_pre_mix, w_in, sgu_ln_g, sgu_ln_b, sgu_w, sgu_b, later_weights, *, d_attn, d_sgu, tm):
    n_tok, d_model = x2d.shape
    n_steps = n_tok // tm
    row_block = lambda w: pl.BlockSpec((w.shape[0] // n_steps, w.shape[1]), lambda i: (i, 0))
    assert all(w.shape[0] % (16 * n_steps) == 0 for w in later_weights)
    n_slab = d_sgu // LANES
    n_groups = sgu_w.shape[0]
    assert n_groups == 2 * n_slab
    d_in = w_in.shape[1]
    const = lambda *shape: pl.BlockSpec(shape, lambda i: (0,) * len(shape))
    return pl.pallas_call(
        functools.partial(_inproj_kernel, d_attn=d_attn, d_sgu=d_sgu, n_sub=tm // SUB_ROWS,
                          n_later=len(later_weights)),
        grid=(n_steps,),
        in_specs=[
            pl.BlockSpec((tm, d_model), lambda i: (i, 0)),
            const(1, d_model),
            pl.BlockSpec((d_model, d_in), lambda i: (0, 0), pipeline_mode=pl.Buffered(1)),
            const(1, d_sgu),
            const(1, d_sgu),
            const(n_groups, SGU_CHUNK, SGU_CHUNK),
            const(n_groups, SGU_CHUNK),
        ] + [row_block(w) for w in later_weights],
        out_specs=[
            pl.BlockSpec((tm, 3 * d_attn), lambda i: (i, 0)),
            pl.BlockSpec((tm, d_sgu), lambda i: (i, 0)),
        ] + [row_block(w) for w in later_weights],
        out_shape=[
            jax.ShapeDtypeStruct((n_tok, 3 * d_attn), _F32),
            jax.ShapeDtypeStruct((n_tok, d_sgu), _BF16),
        ] + [jax.ShapeDtypeStruct(w.shape, _BF16) for w in later_weights],
        scratch_shapes=[
            pltpu.VMEM((d_model, d_in), _BF16),
            pltpu.VMEM((n_slab, SGU_CHUNK, 2 * SGU_CHUNK), _BF16),
            pltpu.VMEM((n_slab, SGU_CHUNK, LANES), _F32),
        ],
        compiler_params=pltpu.CompilerParams(
            dimension_semantics=("arbitrary",), vmem_limit_bytes=VMEM_LIMIT_BYTES),
        name="inproj_sgu",
    )(x2d, g_pre_mix.reshape(1, -1), w_in, sgu_ln_g.reshape(1, -1),
      sgu_ln_b.reshape(1, -1), sgu_w, sgu_b, *later_weights)


def _t5_bucket(rel):
    half = N_REL_BUCKETS // 2
    max_exact = half // 2
    ret = jnp.where(rel > 0, half, 0)
    n = jnp.abs(rel)
    nf = jnp.maximum(n, 1).astype(jnp.float32)
    large = max_exact + (jnp.log(nf / max_exact) / math.log(REL_MAX_DISTANCE / max_exact)
                         * (half - max_exact)).astype(jnp.int32)
    large = jnp.minimum(large, half - 1)
    return ret + jnp.where(n < max_exact, n, large)


def _bucket_tables():
    rel = jnp.arange(K_TILE) - HALF_WINDOW
    tabs = [jnp.where(jnp.abs(rel) <= HALF_WINDOW, _t5_bucket(rel * dil), -1)
            for _, dil in DILATED_BRANCHES]
    return jnp.stack(tabs)[:, None, :].astype(jnp.int32)


def _attn_kernel(relb_ref, bucket_ref, q_ref, k_ref, v_ref, o_ref,
                 kp_ref, vp_ref, perm_ref, bias_ref, p_ref, ob_ref, mb_ref, lb_ref, *, seq):
    hp = pl.program_id(0)
    n_br = len(DILATED_BRANCHES)
    n_tiles_total = seq // Q_TILE
    assert n_tiles_total % 2 == 0
    d1 = DILATED_BRANCHES[1][1]
    assert [d for _, d in DILATED_BRANCHES] == [1, d1, d1 * d1]
    cls_len = seq // d1
    lane_q = lax.broadcasted_iota(jnp.int32, (Q_TILE, LANES), 1) < HEAD_DIM
    col = lax.broadcasted_iota(jnp.int32, (Q_TILE, K_TILE), 1)

    @pl.when(pl.program_id(1) == 0)
    def _():
        for i in range(n_br):
            bucket = bucket_ref[i]
            for hh in range(2):
                row = jnp.full(bucket.shape, relb_ref[2 * hp + hh, 0] * LOG2E, _F32)
                for b in range(1, N_REL_BUCKETS):
                    row = jnp.where(bucket >= b, relb_ref[2 * hp + hh, b] * LOG2E, row)
                row = jnp.where(bucket < 0, NEG_INF, row)
                tile = pltpu.roll(jnp.broadcast_to(row, (Q_TILE, K_TILE)), 0, 1, stride=1, stride_axis=0)
                rows = slice(hh * Q_TILE, (hh + 1) * Q_TILE)
                bias_ref[i, 0, rows, :] = jnp.where(col < HALF_WINDOW, NEG_INF, tile)
                bias_ref[i, 1, rows, :] = tile
                bias_ref[i, 2, rows, :] = jnp.where(col >= K_TILE - HALF_WINDOW, NEG_INF, tile)

    zero_pad = jnp.zeros((HALF_WINDOW, LANES), _BF16)
    for i in range(n_br):
        for ref in (kp_ref, vp_ref):
            ref[i, :HALF_WINDOW, :] = zero_pad
            ref[i, HALF_WINDOW + seq:, :] = zero_pad
    for c0 in range(0, seq, COPY_ROWS):
        kp_ref[0, pl.ds(HALF_WINDOW + c0, COPY_ROWS), :] = k_ref[pl.ds(c0, COPY_ROWS), :].astype(_BF16)
        vp_ref[0, pl.ds(HALF_WINDOW + c0, COPY_ROWS), :] = v_ref[pl.ds(c0, COPY_ROWS), :].astype(_BF16)
    for r in range(d1):
        for c0 in range(0, cls_len, COPY_ROWS):
            src = pl.ds(r + d1 * c0, COPY_ROWS, stride=d1)
            dst = pl.ds(r * cls_len + c0, COPY_ROWS)
            padded = pl.ds(HALF_WINDOW + r * cls_len + c0, COPY_ROWS)
            perm_ref[0, dst, :] = q_ref[src, :]
            for a, (x_ref, xp_ref) in enumerate(((k_ref, kp_ref), (v_ref, vp_ref)), start=1):
                x = x_ref[src, :]
                perm_ref[a, dst, :] = x
                xp_ref[1, padded, :] = x.astype(_BF16)
    sub_len = cls_len // d1
    for r2 in range(d1 * d1):
        src = pl.ds((r2 % d1) * cls_len + r2 // d1, sub_len, stride=d1)
        padded = pl.ds(HALF_WINDOW + r2 * sub_len, sub_len)
        kp_ref[2, padded, :] = perm_ref[1, src, :].astype(_BF16)
        vp_ref[2, padded, :] = perm_ref[2, src, :].astype(_BF16)

    def tile_geometry(i, j):
        n_tiles = seq // (DILATED_BRANCHES[i][1] * Q_TILE)
        n = j % n_tiles
        variant = 0 if n == 0 else (2 if n == n_tiles - 1 else 1)
        if i < 2:
            rows = pl.ds(j * Q_TILE, Q_TILE)
        else:
            r2 = j // n_tiles
            rows = pl.ds((r2 % d1) * cls_len + r2 // d1 + d1 * Q_TILE * n, Q_TILE, stride=d1)
        return rows, pl.ds(j * Q_TILE, K_TILE), variant

    def stage_scores(i, j, slot):
        rows, win, variant = tile_geometry(i, j)
        q = q_ref[rows, :] if i == 0 else perm_ref[0, rows, :]
        q2 = jnp.concatenate([jnp.where(lane_q, q, 0.0), jnp.where(lane_q, 0.0, q)], axis=0)
        s = lax.dot_general(q2.astype(_BF16), kp_ref[i, win, :], (((1,), (1,)), ((), ())),
                            preferred_element_type=_F32)
        s = s + bias_ref[i, variant]
        m2 = jnp.max(s, axis=-1, keepdims=True)
        p_ref[slot] = jnp.exp2(s - m2).astype(_BF16)
        mb_ref[i, rows, :] = jnp.where(lane_q, m2[:Q_TILE], m2[Q_TILE:])

    ones = jnp.ones((K_TILE, LANES), _BF16)

    def stage_values(i, j, slot):
        rows, win, _ = tile_geometry(i, j)
        v1 = jnp.concatenate([vp_ref[i, win, :], ones], axis=1)
        pv = jnp.dot(p_ref[slot], v1, preferred_element_type=_F32)
        ob_ref[i, rows, :] = jnp.where(lane_q, pv[:Q_TILE, :LANES], pv[Q_TILE:, :LANES])
        lb_ref[i, rows, :] = jnp.where(lane_q, pv[:Q_TILE, LANES:], pv[Q_TILE:, LANES:])

    stage_scores(0, 0, 0)
    for i in range(n_br):
        for j in range(n_tiles_total):
            if j + 1 < n_tiles_total:
                stage_scores(i, j + 1, (j + 1) % 2)
            elif i + 1 < n_br:
                stage_scores(i + 1, 0, 0)
            stage_values(i, j, j % 2)

    blocks_per_class = cls_len // MERGE_ROWS

    def merge_body(t, carry):
        cls_rows = pl.ds(pl.multiple_of(t * MERGE_ROWS, MERGE_ROWS), MERGE_ROWS)
        nat_rows = pl.ds(t // blocks_per_class + d1 * MERGE_ROWS * (t % blocks_per_class), MERGE_ROWS, stride=d1)
        rows = [nat_rows] + [cls_rows] * (n_br - 1)
        ms = [mb_ref[i, rows[i], :] for i in range(n_br)]
        m = functools.reduce(jnp.maximum, ms)
        ws = [jnp.exp2(m_i - m) for m_i in ms]
        num = functools.reduce(jnp.add, [w * ob_ref[i, rows[i], :] for i, w in enumerate(ws)])
        den = functools.reduce(jnp.add, [w * lb_ref[i, rows[i], :] for i, w in enumerate(ws)])
        perm_ref[1, nat_rows, :] = num / den
        return carry

    lax.fori_loop(0, seq // MERGE_ROWS, merge_body, 0, unroll=2)
    for c0 in range(0, seq, COPY_ROWS):
        o_ref[pl.ds(c0, COPY_ROWS), :] = perm_ref[1, pl.ds(c0, COPY_ROWS), :].astype(o_ref.dtype)


def _attn_call(qkv, rel_bias, *, batch, seq, d_attn):
    n_pairs = d_attn // LANES
    qkv3 = qkv.reshape(batch, seq, 3 * d_attn)
    slab = lambda off: pl.BlockSpec((None, seq, LANES), lambda hp, b: (b, 0, off + hp))
    n_br = len(DILATED_BRANCHES)
    return pl.pallas_call(
        functools.partial(_attn_kernel, seq=seq),
        grid=(n_pairs, batch),
        in_specs=[
            pl.BlockSpec(memory_space=pltpu.SMEM),
            pl.BlockSpec((n_br, 1, K_TILE), lambda hp, b: (0, 0, 0)),
            slab(0), slab(n_pairs), slab(2 * n_pairs),
        ],
        out_specs=pl.BlockSpec((None, seq, LANES), lambda hp, b: (b, 0, hp)),
        out_shape=jax.ShapeDtypeStruct((batch, seq, d_attn), _BF16),
        scratch_shapes=[
            pltpu.VMEM((n_br, seq + 2 * HALF_WINDOW, LANES), _BF16),
            pltpu.VMEM((n_br, seq + 2 * HALF_WINDOW, LANES), _BF16),
            pltpu.VMEM((3, seq, LANES), _F32),
            pltpu.VMEM((n_br, 3, 2 * Q_TILE, K_TILE), _F32),
            pltpu.VMEM((2, 2 * Q_TILE, K_TILE), _BF16),
            pltpu.VMEM((n_br, seq, LANES), _F32),
            pltpu.VMEM((n_br, seq, LANES), _F32),
            pltpu.VMEM((n_br, seq, LANES), _F32),
        ],
        compiler_params=pltpu.CompilerParams(
            dimension_semantics=("arbitrary", "arbitrary"), vmem_limit_bytes=VMEM_LIMIT_BYTES),
        name="dilated_attn",
    )(rel_bias.T, _bucket_tables(), qkv3, qkv3, qkv3)


def _outproj_ffn_kernel(attn_ref, sgu_ref, x_ref, wo_ref, g1_ref, g2_ref, w1_ref, w2_ref, g3_ref,
                        o_ref, *, d_attn, ff_chunk, n_sub):
    sub = x_ref.shape[0] // n_sub
    rows = [slice(t * sub, (t + 1) * sub) for t in range(n_sub)]
    chunks = range(0, w1_ref.shape[1], ff_chunk)

    def out_proj(t):
        return (jnp.dot(attn_ref[rows[t], :], wo_ref[:d_attn, :], preferred_element_type=_F32)
                + jnp.dot(sgu_ref[rows[t], :], wo_ref[d_attn:, :], preferred_element_type=_F32))

    def mid_norms(t, mix):
        x1 = x_ref[rows[t], :] + _rms_norm(mix, g1_ref[...])
        return x1, _rms_norm(x1, g2_ref[...]).astype(_BF16)

    def ffn_chunk(h, f, c0):
        a = jnp.maximum(jnp.dot(h, w1_ref[:, c0:c0 + ff_chunk], preferred_element_type=_F32), 0.0)
        part = jnp.dot((a * a).astype(_BF16), w2_ref[c0:c0 + ff_chunk, :], preferred_element_type=_F32)
        return part if f is None else f + part

    mix = out_proj(0)
    prev = None
    for t in range(n_sub):
        nxt_mix = out_proj(t + 1) if t + 1 < n_sub else None
        x1, h = mid_norms(t, mix)
        f = None
        for ci, c0 in enumerate(chunks):
            f = ffn_chunk(h, f, c0)
            if ci == 0 and prev is not None:
                pt, px1, pf = prev
                o_ref[rows[pt], :] = px1 + _rms_norm(pf, g3_ref[...])
        prev = (t, x1, f)
        mix = nxt_mix
    pt, px1, pf = prev
    o_ref[rows[pt], :] = px1 + _rms_norm(pf, g3_ref[...])


def _outproj_ffn_call(attn2d, sgu2d, x2d, w_out, g_post_mix, g_pre_ffn, w_ff1, w_ff2, g_post_ffn, *, tm):
    n_tok, d_model = x2d.shape
    d_attn = attn2d.shape[1]
    d_sgu = sgu2d.shape[1]
    d_ff = w_ff1.shape[1]
    const = lambda *shape: pl.BlockSpec(shape, lambda i: (0,) * len(shape), pipeline_mode=pl.Buffered(1))
    return pl.pallas_call(
        functools.partial(_outproj_ffn_kernel, d_attn=d_attn, ff_chunk=FF_CHUNK, n_sub=tm // SUB_ROWS),
        grid=(n_tok // tm,),
        in_specs=[
            pl.BlockSpec((tm, d_attn), lambda i: (i, 0)),
            pl.BlockSpec((tm, d_sgu), lambda i: (i, 0)),
            pl.BlockSpec((tm, d_model), lambda i: (i, 0)),
            const(d_attn + d_sgu, d_model),
            const(1, d_model),
            const(1, d_model),
            const(d_model, d_ff),
            const(d_ff, d_model),
            const(1, d_model),
        ],
        out_specs=pl.BlockSpec((tm, d_model), lambda i: (i, 0)),
        out_shape=jax.ShapeDtypeStruct((n_tok, d_model), _F32),
        compiler_params=pltpu.CompilerParams(
            dimension_semantics=("arbitrary",), vmem_limit_bytes=VMEM_LIMIT_BYTES),
        name="outproj_ffn",
    )(attn2d, sgu2d, x2d, w_out, g_post_mix.reshape(1, -1), g_pre_ffn.reshape(1, -1),
      w_ff1, w_ff2, g_post_ffn.reshape(1, -1))


def kernel(x, g_pre_mix, w_in, sgu_ln_g, sgu_ln_b, sgu_w, sgu_b, w_out, g_post_mix, g_pre_ffn,
           w_ff1, w_ff2, g_post_ffn, rel_bias):
    batch, seq, d_model = x.shape
    d_sgu = sgu_ln_g.shape[-1]
    d_attn = w_out.shape[1] - d_sgu
    assert d_attn % LANES == 0 and d_sgu % LANES == 0 and sgu_w.shape[-1] == SGU_CHUNK
    assert all(seq % (dil * Q_TILE) == 0 and win == 2 * HALF_WINDOW * dil for win, dil in DILATED_BRANCHES)
    assert (batch * seq) % TOKEN_TILE == 0 and w_ff1.shape[-1] % FF_CHUNK == 0
    x2d = x.reshape(batch * seq, d_model)
    for layer in range(g_pre_mix.shape[0]):
        qkv, sgu, w_out_bf, w_ff1_bf, w_ff2_bf = _inproj_call(
            x2d, g_pre_mix[layer], w_in[layer], sgu_ln_g[layer], sgu_ln_b[layer], sgu_w[layer], sgu_b[layer],
            (w_out[layer], w_ff1[layer], w_ff2[layer]), d_attn=d_attn, d_sgu=d_sgu, tm=TOKEN_TILE)
        attn = _attn_call(qkv, rel_bias, batch=batch, seq=seq, d_attn=d_attn)
        x2d = _outproj_ffn_call(attn.reshape(batch * seq, d_attn), sgu, x2d, w_out_bf,
                                g_post_mix[layer], g_pre_ffn[layer], w_ff1_bf, w_ff2_bf,
                                g_post_ffn[layer], tm=TOKEN_TILE)
    return x2d.reshape(batch, seq, d_model)
```

```python
import functools
import math

import jax
import jax.numpy as jnp
from jax import lax
from jax.experimental import pallas as pl
from jax.experimental.pallas import tpu as pltpu

HEAD_DIM = 64
DILATED_BRANCHES = ((128, 1), (512, 4), (2048, 16))
SGU_CHUNK = 128
N_REL_BUCKETS = 32
REL_MAX_DISTANCE = 1024
RMS_EPS = 1e-6
LN_EPS = 1e-5
NEG_INF = -1e30

LANES = 128
HALF_WINDOW = 64
Q_TILE = 128
K_TILE = Q_TILE + 2 * HALF_WINDOW
MERGE_ROWS = 256
COPY_ROWS = 512
CAST_COLS = 512
TOKEN_TILE = 1024
SUB_ROWS = 256
FF_CHUNK = 1024
LOG2E = math.log2(math.e)
V7X_VMEM_BYTES = 64 * 1024 * 1024
VMEM_LIMIT_BYTES = V7X_VMEM_BYTES - 8 * 1024 * 1024

_F32 = jnp.float32
_BF16 = jnp.bfloat16


def _rms_norm(x, g):
    return x * lax.rsqrt(jnp.mean(x * x, axis=-1, keepdims=True) + RMS_EPS) * g


def _inproj_kernel(x_ref, g_ref, w32_ref, lng_ref, lnb_ref, ws_ref, bsg_ref, *rest,
                   d_attn, d_sgu, n_sub, n_later):
    later_f32, (qkv_ref, sgu_ref), later_bf16 = rest[:n_later], rest[n_later:n_later + 2], rest[n_later + 2:-3]
    w_ref, wcat_ref, bs_ref = rest[-3:]
    lane = lax.broadcasted_iota(jnp.int32, (SGU_CHUNK, LANES), 1)
    lo = lane < HEAD_DIM

    @pl.when(pl.program_id(0) == 0)
    def _():
        for c0 in range(0, w_ref.shape[1], CAST_COLS):
            w_ref[:, c0:c0 + CAST_COLS] = w32_ref[:, c0:c0 + CAST_COLS].astype(_BF16)
        bst = bsg_ref[...].T
        for j in range(d_sgu // LANES):
            wcat_ref[j] = jnp.concatenate([ws_ref[2 * j], ws_ref[2 * j + 1]], axis=1).astype(_BF16)
            bs_ref[j] = jnp.where(lo, bst[:, 2 * j:2 * j + 1], bst[:, 2 * j + 1:2 * j + 2])

    sub = x_ref.shape[0] // n_sub
    n_qkv = 3 * d_attn
    pair = 2 * SGU_CHUNK
    assert sub % pair == 0

    def normed(t):
        return _rms_norm(x_ref[t * sub:(t + 1) * sub, :], g_ref[...]).astype(_BF16)

    def gating(t, zg):
        zg = jax.nn.gelu(zg)
        u = zg[:, :d_sgu]
        v = zg[:, d_sgu:]
        mu = jnp.mean(v, axis=-1, keepdims=True)
        vc = v - mu
        vn = vc * lax.rsqrt(jnp.mean(vc * vc, axis=-1, keepdims=True) + LN_EPS) * lng_ref[...] + lnb_ref[...]
        for c0 in range(0, sub, pair):
            for j in range(d_sgu // LANES):
                cols = slice(j * LANES, (j + 1) * LANES)
                stacked = []
                for c in (c0, c0 + SGU_CHUNK):
                    slab = vn[c:c + SGU_CHUNK, cols]
                    stacked.append(jnp.concatenate([jnp.where(lo, slab, 0.0), jnp.where(lo, 0.0, slab)], axis=0))
                rhs = jnp.concatenate(stacked, axis=1).astype(_BF16)
                mixed = jnp.dot(wcat_ref[j], rhs, preferred_element_type=_F32)
                for k, c in enumerate((c0, c0 + SGU_CHUNK)):
                    m = mixed[:, k * LANES:(k + 1) * LANES] + bs_ref[j]
                    rows = slice(t * sub + c, t * sub + c + SGU_CHUNK)
                    sgu_ref[rows, cols] = (u[c:c + SGU_CHUNK, cols] * m).astype(_BF16)

    h = normed(0)
    for t in range(n_sub):
        h_next = normed(t + 1) if t + 1 < n_sub else None
        rows = slice(t * sub, (t + 1) * sub)
        zg = jnp.dot(h, w_ref[:, n_qkv:], preferred_element_type=_F32)
        z = jnp.dot(h, w_ref[:, :n_qkv], preferred_element_type=_F32)
        qkv_ref[rows, :d_attn] = z[:, :d_attn] * (HEAD_DIM ** -0.5 * LOG2E)
        qkv_ref[rows, d_attn:] = z[:, d_attn:]
        if t == 0:
            for src_ref, dst_ref in zip(later_f32, later_bf16):
                dst_ref[...] = src_ref[...].astype(_BF16)
        gating(t, zg)
        h = h_next


def _inproj_call(x2d, g_pre_mix, w_in, sgu_ln_g, sgu_ln_b, sgu_w, sgu_b, later_weights, *, d_attn, d_sgu, tm):
    n_tok, d_model = x2d.shape
    n_steps = n_tok // tm
    row_block = lambda w: pl.BlockSpec((w.shape[0] // n_steps, w.shape[1]), lambda i: (i, 0))
    assert all(w.shape[0] % (16 * n_steps) == 0 for w in later_weights)
    n_slab = d_sgu // LANES
    n_groups = sgu_w.shape[0]
    assert n_groups == 2 * n_slab
    d_in = w_in.shape[1]
    const = lambda *shape: pl.BlockSpec(shape, lambda i: (0,) * len(shape))
    return pl.pallas_call(
        functools.partial(_inproj_kernel, d_attn=d_attn, d_sgu=d_sgu, n_sub=tm // SUB_ROWS,
                          n_later=len(later_weights)),
        grid=(n_steps,),
        in_specs=[
            pl.BlockSpec((tm, d_model), lambda i: (i, 0)),
            const(1, d_model),
            pl.BlockSpec((d_model, d_in), lambda i: (0, 0), pipeline_mode=pl.Buffered(1)),
            const(1, d_sgu),
            const(1, d_sgu),
            const(n_groups, SGU_CHUNK, SGU_CHUNK),
            const(n_groups, SGU_CHUNK),
        ] + [row_block(w) for w in later_weights],
        out_specs=[
            pl.BlockSpec((tm, 3 * d_attn), lambda i: (i, 0)),
            pl.BlockSpec((tm, d_sgu), lambda i: (i, 0)),
        ] + [row_block(w) for w in later_weights],
        out_shape=[
            jax.ShapeDtypeStruct((n_tok, 3 * d_attn), _F32),
            jax.ShapeDtypeStruct((n_tok, d_sgu), _BF16),
        ] + [jax.ShapeDtypeStruct(w.shape, _BF16) for w in later_weights],
        scratch_shapes=[
            pltpu.VMEM((d_model, d_in), _BF16),
            pltpu.VMEM((n_slab, SGU_CHUNK, 2 * SGU_CHUNK), _BF16),
            pltpu.VMEM((n_slab, SGU_CHUNK, LANES), _F32),
        ],
        compiler_params=pltpu.CompilerParams(
            dimension_semantics=("arbitrary",), vmem_limit_bytes=VMEM_LIMIT_BYTES),
        name="inproj_sgu",
    )(x2d, g_pre_mix.reshape(1, -1), w_in, sgu_ln_g.reshape(1, -1),
      sgu_ln_b.reshape(1, -1), sgu_w, sgu_b, *later_weights)


def _t5_bucket(rel):
    half = N_REL_BUCKETS // 2
    max_exact = half // 2
    ret = jnp.where(rel > 0, half, 0)
    n = jnp.abs(rel)
    nf = jnp.maximum(n, 1).astype(jnp.float32)
    large = max_exact + (jnp.log(nf / max_exact) / math.log(REL_MAX_DISTANCE / max_exact)
                         * (half - max_exact)).astype(jnp.int32)
    large = jnp.minimum(large, half - 1)
    return ret + jnp.where(n < max_exact, n, large)


def _bucket_tables():
    rel = jnp.arange(K_TILE) - HALF_WINDOW
    tabs = [jnp.where(jnp.abs(rel) <= HALF_WINDOW, _t5_bucket(rel * dil), -1)
            for _, dil in DILATED_BRANCHES]
    return jnp.stack(tabs)[:, None, :].astype(jnp.int32)


def _attn_kernel(relb_ref, bucket_ref, q_ref, k_ref, v_ref, o_ref,
                 kp_ref, vp_ref, perm_ref, bias_ref, p_ref, ob_ref, mb_ref, lb_ref, *, seq):
    hp = pl.program_id(0)
    n_br = len(DILATED_BRANCHES)
    n_tiles_total = seq // Q_TILE
    assert n_tiles_total % 2 == 0
    d1 = DILATED_BRANCHES[1][1]
    assert [d for _, d in DILATED_BRANCHES] == [1, d1, d1 * d1]
    cls_len = seq // d1
    lane_q = lax.broadcasted_iota(jnp.int32, (Q_TILE, LANES), 1) < HEAD_DIM
    col = lax.broadcasted_iota(jnp.int32, (Q_TILE, K_TILE), 1)

    @pl.when(pl.program_id(1) == 0)
    def _():
        for i in range(n_br):
            bucket = bucket_ref[i]
            for hh in range(2):
                row = jnp.full(bucket.shape, relb_ref[2 * hp + hh, 0] * LOG2E, _F32)
                for b in range(1, N_REL_BUCKETS):
                    row = jnp.where(bucket >= b, relb_ref[2 * hp + hh, b] * LOG2E, row)
                row = jnp.where(bucket < 0, NEG_INF, row)
                tile = pltpu.roll(jnp.broadcast_to(row, (Q_TILE, K_TILE)), 0, 1, stride=1, stride_axis=0)
                rows = slice(hh * Q_TILE, (hh + 1) * Q_TILE)
                bias_ref[i, 0, rows, :] = jnp.where(col < HALF_WINDOW, NEG_INF, tile)
                bias_ref[i, 1, rows, :] = tile
                bias_ref[i, 2, rows, :] = jnp.where(col >= K_TILE - HALF_WINDOW, NEG_INF, tile)

    zero_pad = jnp.zeros((HALF_WINDOW, LANES), _BF16)
    for i in range(n_br):
        for ref in (kp_ref, vp_ref):
            ref[i, :HALF_WINDOW, :] = zero_pad
            ref[i, HALF_WINDOW + seq:, :] = zero_pad
    for c0 in range(0, seq, COPY_ROWS):
        kp_ref[0, pl.ds(HALF_WINDOW + c0, COPY_ROWS), :] = k_ref[pl.ds(c0, COPY_ROWS), :].astype(_BF16)
        vp_ref[0, pl.ds(HALF_WINDOW + c0, COPY_ROWS), :] = v_ref[pl.ds(c0, COPY_ROWS), :].astype(_BF16)
    for r in range(d1):
        for c0 in range(0, cls_len, COPY_ROWS):
            src = pl.ds(r + d1 * c0, COPY_ROWS, stride=d1)
            dst = pl.ds(r * cls_len + c0, COPY_ROWS)
            padded = pl.ds(HALF_WINDOW + r * cls_len + c0, COPY_ROWS)
            perm_ref[0, dst, :] = q_ref[src, :]
            for a, (x_ref, xp_ref) in enumerate(((k_ref, kp_ref), (v_ref, vp_ref)), start=1):
                x = x_ref[src, :]
                perm_ref[a, dst, :] = x
                xp_ref[1, padded, :] = x.astype(_BF16)
    sub_len = cls_len // d1
    for r2 in range(d1 * d1):
        src = pl.ds((r2 % d1) * cls_len + r2 // d1, sub_len, stride=d1)
        padded = pl.ds(HALF_WINDOW + r2 * sub_len, sub_len)
        kp_ref[2, padded, :] = perm_ref[1, src, :].astype(_BF16)
        vp_ref[2, padded, :] = perm_ref[2, src, :].astype(_BF16)

    def tile_geometry(i, j):
        n_tiles = seq // (DILATED_BRANCHES[i][1] * Q_TILE)
        n = j % n_tiles
        variant = 0 if n == 0 else (2 if n == n_tiles - 1 else 1)
        if i < 2:
            rows = pl.ds(j * Q_TILE, Q_TILE)
        else:
            r2 = j // n_tiles
            rows = pl.ds((r2 % d1) * cls_len + r2 // d1 + d1 * Q_TILE * n, Q_TILE, stride=d1)
        return rows, pl.ds(j * Q_TILE, K_TILE), variant

    def stage_scores(i, j, slot):
        rows, win, variant = tile_geometry(i, j)
        q = q_ref[rows, :] if i == 0 else perm_ref[0, rows, :]
        q2 = jnp.concatenate([jnp.where(lane_q, q, 0.0), jnp.where(lane_q, 0.0, q)], axis=0)
        s = lax.dot_general(q2.astype(_BF16), kp_ref[i, win, :], (((1,), (1,)), ((), ())),
                            preferred_element_type=_F32)
        s = s + bias_ref[i, variant]
        m2 = jnp.max(s, axis=-1, keepdims=True)
        p_ref[slot] = jnp.exp2(s - m2).astype(_BF16)
        mb_ref[i, rows, :] = jnp.where(lane_q, m2[:Q_TILE], m2[Q_TILE:])

    ones = jnp.ones((K_TILE, LANES), _BF16)

    def stage_values(i, j, slot):
        rows, win, _ = tile_geometry(i, j)
        v1 = jnp.concatenate([vp_ref[i, win, :], ones], axis=1)
        pv = jnp.dot(p_ref[slot], v1, preferred_element_type=_F32)
        ob_ref[i, rows, :] = jnp.where(lane_q, pv[:Q_TILE, :LANES], pv[Q_TILE:, :LANES])
        lb_ref[i, rows, :] = jnp.where(lane_q, pv[:Q_TILE, LANES:], pv[Q_TILE:, LANES:])

    stage_scores(0, 0, 0)
    for i in range(n_br):
        for j in range(n_tiles_total):
            stage_values(i, j, j % 2)
            if j + 1 < n_tiles_total:
                stage_scores(i, j + 1, (j + 1) % 2)
            elif i + 1 < n_br:
                stage_scores(i + 1, 0, 0)

    blocks_per_class = cls_len // MERGE_ROWS

    def merge_body(t, carry):
        cls_rows = pl.ds(pl.multiple_of(t * MERGE_ROWS, MERGE_ROWS), MERGE_ROWS)
        nat_rows = pl.ds(t // blocks_per_class + d1 * MERGE_ROWS * (t % blocks_per_class), MERGE_ROWS, stride=d1)
        rows = [nat_rows] + [cls_rows] * (n_br - 1)
        ms = [mb_ref[i, rows[i], :] for i in range(n_br)]
        m = functools.reduce(jnp.maximum, ms)
        ws = [jnp.exp2(m_i - m) for m_i in ms]
        num = functools.reduce(jnp.add, [w * ob_ref[i, rows[i], :] for i, w in enumerate(ws)])
        den = functools.reduce(jnp.add, [w * lb_ref[i, rows[i], :] for i, w in enumerate(ws)])
        perm_ref[1, nat_rows, :] = num / den
        return carry

    lax.fori_loop(0, seq // MERGE_ROWS, merge_body, 0, unroll=4)
    for c0 in range(0, seq, COPY_ROWS):
        o_ref[pl.ds(c0, COPY_ROWS), :] = perm_ref[1, pl.ds(c0, COPY_ROWS), :].astype(o_ref.dtype)


def _attn_call(qkv, rel_bias, *, batch, seq, d_attn):
    n_pairs = d_attn // LANES
    qkv3 = qkv.reshape(batch, seq, 3 * d_attn)
    slab = lambda off: pl.BlockSpec((None, seq, LANES), lambda hp, b: (b, 0, off + hp))
    n_br = len(DILATED_BRANCHES)
    return pl.pallas_call(
        functools.partial(_attn_kernel, seq=seq),
        grid=(n_pairs, batch),
        in_specs=[
            pl.BlockSpec(memory_space=pltpu.SMEM),
            pl.BlockSpec((n_br, 1, K_TILE), lambda hp, b: (0, 0, 0)),
            slab(0), slab(n_pairs), slab(2 * n_pairs),
        ],
        out_specs=pl.BlockSpec((None, seq, LANES), lambda hp, b: (b, 0, hp)),
        out_shape=jax.ShapeDtypeStruct((batch, seq, d_attn), _BF16),
        scratch_shapes=[
            pltpu.VMEM((n_br, seq + 2 * HALF_WINDOW, LANES), _BF16),
            pltpu.VMEM((n_br, seq + 2 * HALF_WINDOW, LANES), _BF16),
            pltpu.VMEM((3, seq, LANES), _F32),
            pltpu.VMEM((n_br, 3, 2 * Q_TILE, K_TILE), _F32),
            pltpu.VMEM((2, 2 * Q_TILE, K_TILE), _BF16),
            pltpu.VMEM((n_br, seq, LANES), _F32),
            pltpu.VMEM((n_br, seq, LANES), _F32),
            pltpu.VMEM((n_br, seq, LANES), _F32),
        ],
        compiler_params=pltpu.CompilerParams(
            dimension_semantics=("arbitrary", "arbitrary"), vmem_limit_bytes=VMEM_LIMIT_BYTES),
        name="dilated_attn",
    )(rel_bias.T, _bucket_tables(), qkv3, qkv3, qkv3)


def _outproj_ffn_kernel(attn_ref, sgu_ref, x_ref, wo_ref, g1_ref, g2_ref, w1_ref, w2_ref, g3_ref,
                        o_ref, *, d_attn, ff_chunk, n_sub):
    sub = x_ref.shape[0] // n_sub
    rows = [slice(t * sub, (t + 1) * sub) for t in range(n_sub)]
    chunks = range(0, w1_ref.shape[1], ff_chunk)

    def out_proj(t):
        return (jnp.dot(attn_ref[rows[t], :], wo_ref[:d_attn, :], preferred_element_type=_F32)
                + jnp.dot(sgu_ref[rows[t], :], wo_ref[d_attn:, :], preferred_element_type=_F32))

    def mid_norms(t, mix):
        x1 = x_ref[rows[t], :] + _rms_norm(mix, g1_ref[...])
        return x1, _rms_norm(x1, g2_ref[...]).astype(_BF16)

    def ffn_chunk(h, f, c0):
        a = jnp.maximum(jnp.dot(h, w1_ref[:, c0:c0 + ff_chunk], preferred_element_type=_F32), 0.0)
        part = jnp.dot((a * a).astype(_BF16), w2_ref[c0:c0 + ff_chunk, :], preferred_element_type=_F32)
        return part if f is None else f + part

    mix = out_proj(0)
    prev = None
    for t in range(n_sub):
        nxt_mix = out_proj(t + 1) if t + 1 < n_sub else None
        x1, h = mid_norms(t, mix)
        f = None
        for ci, c0 in enumerate(chunks):
            f = ffn_chunk(h, f, c0)
            if ci == 0 and prev is not None:
                pt, px1, pf = prev
                o_ref[rows[pt], :] = px1 + _rms_norm(pf, g3_ref[...])
        prev = (t, x1, f)
        mix = nxt_mix
    pt, px1, pf = prev
    o_ref[rows[pt], :] = px1 + _rms_norm(pf, g3_ref[...])


def _outproj_ffn_call(attn2d, sgu2d, x2d, w_out, g_post_mix, g_pre_ffn, w_ff1, w_ff2, g_post_ffn, *, tm):
    n_tok, d_model = x2d.shape
    d_attn = attn2d.shape[1]
    d_sgu = sgu2d.shape[1]
    d_ff = w_ff1.shape[1]
    const = lambda *shape: pl.BlockSpec(shape, lambda i: (0,) * len(shape), pipeline_mode=pl.Buffered(1))
    return pl.pallas_call(
        functools.partial(_outproj_ffn_kernel, d_attn=d_attn, ff_chunk=FF_CHUNK, n_sub=tm // SUB_ROWS),
        grid=(n_tok // tm,),
        in_specs=[
            pl.BlockSpec((tm, d_attn), lambda i: (i, 0)),
            pl.BlockSpec((tm, d_sgu), lambda i: (i, 0)),
            pl.BlockSpec((tm, d_model), lambda i: (i, 0)),
            const(d_attn + d_sgu, d_model),
            const(1, d_model),
            const(1, d_model),
            const(d_model, d_ff),
            const(d_ff, d_model),
            const(1, d_model),
        ],
        out_specs=pl.BlockSpec((tm, d_model), lambda i: (i, 0)),
        out_shape=jax.ShapeDtypeStruct((n_tok, d_model), _F32),
        compiler_params=pltpu.CompilerParams(
            dimension_semantics=("arbitrary",), vmem_limit_bytes=VMEM_LIMIT_BYTES),
        name="outproj_ffn",
    )(attn2d, sgu2d, x2d, w_out, g_post_mix.reshape(1, -1), g_pre_ffn.reshape(1, -1),
      w_ff1, w_ff2, g_post_ffn.reshape(1, -1))


def kernel(x, g_pre_mix, w_in, sgu_ln_g, sgu_ln_b, sgu_w, sgu_b, w_out, g_post_mix, g_pre_ffn,
           w_ff1, w_ff2, g_post_ffn, rel_bias):
    batch, seq, d_model = x.shape
    d_sgu = sgu_ln_g.shape[-1]
    d_attn = w_out.shape[1] - d_sgu
    assert d_attn % LANES == 0 and d_sgu % LANES == 0 and sgu_w.shape[-1] == SGU_CHUNK
    assert all(seq % (dil * Q_TILE) == 0 and win == 2 * HALF_WINDOW * dil for win, dil in DILATED_BRANCHES)
    assert (batch * seq) % TOKEN_TILE == 0 and w_ff1.shape[-1] % FF_CHUNK == 0
    x2d = x.reshape(batch * seq, d_model)
    for layer in range(g_pre_mix.shape[0]):
        qkv, sgu, w_out_bf, w_ff1_bf, w_ff2_bf = _inproj_call(
            x2d, g_pre_mix[layer], w_in[layer], sgu_ln_g[layer], sgu_ln_b[layer], sgu_w[layer], sgu_b[layer],
            (w_out[layer], w_ff1[layer], w_ff2[layer]), d_attn=d_attn, d_sgu=d_sgu, tm=TOKEN_TILE)
        attn = _attn_call(qkv, rel_bias, batch=batch, seq=seq, d_attn=d_attn)
        x2d = _outproj_ffn_call(attn.reshape(batch * seq, d_attn), sgu, x2d, w_out_bf,
                                g_post_mix[layer], g_pre_ffn[layer], w_ff1_bf, w_ff2_bf,
                                g_post_ffn[layer], tm=TOKEN_TILE)
    return x2d.reshape(batch, seq, d_model)
```

```python
import functools
import math

import jax
import jax.numpy as jnp
from jax import lax
from jax.experimental import pallas as pl
from jax.experimental.pallas import tpu as pltpu

HEAD_DIM = 64
DILATED_BRANCHES = ((128, 1), (512, 4), (2048, 16))
SGU_CHUNK = 128
N_REL_BUCKETS = 32
REL_MAX_DISTANCE = 1024
RMS_EPS = 1e-6
LN_EPS = 1e-5
NEG_INF = -1e30

LANES = 128
HALF_WINDOW = 64
Q_TILE = 128
K_TILE = Q_TILE + 2 * HALF_WINDOW
MERGE_ROWS = 256
COPY_ROWS = 512
CAST_COLS = 512
X_SLOTS = 3
TOKEN_TILE = 1024
SUB_ROWS = 256
FF_CHUNK = 1024
LOG2E = math.log2(math.e)
V7X_VMEM_BYTES = 64 * 1024 * 1024
VMEM_LIMIT_BYTES = V7X_VMEM_BYTES - 8 * 1024 * 1024

_F32 = jnp.float32
_BF16 = jnp.bfloat16


def _rms_norm(x, g):
    return x * lax.rsqrt(jnp.mean(x * x, axis=-1, keepdims=True) + RMS_EPS) * g


def _inproj_kernel(x_hbm, g_ref, w32_ref, lng_ref, lnb_ref, ws_ref, bsg_ref, *rest,
                   d_attn, d_sgu, n_sub, n_later):
    later_f32, (qkv_ref, sgu_ref), later_bf16 = rest[:n_later], rest[n_later:n_later + 2], rest[n_later + 2:-5]
    w_ref, wcat_ref, bs_ref, xbuf_ref, xsem = rest[-5:]
    lane = lax.broadcasted_iota(jnp.int32, (SGU_CHUNK, LANES), 1)
    lo = lane < HEAD_DIM

    step, n_steps, tm = pl.program_id(0), pl.num_programs(0), xbuf_ref.shape[1]

    def x_copy(s):
        slot = s % X_SLOTS
        return pltpu.make_async_copy(x_hbm.at[pl.ds(s * tm, tm), :], xbuf_ref.at[slot], xsem.at[slot])

    @pl.when(step == 0)
    def _():
        for s in range(X_SLOTS - 1):
            x_copy(s).start()

    x_copy(step).wait()

    @pl.when(step + (X_SLOTS - 1) < n_steps)
    def _():
        x_copy(step + (X_SLOTS - 1)).start()

    x_ref = xbuf_ref.at[step % X_SLOTS]

    @pl.when(step == 0)
    def _():
        for c0 in range(0, w_ref.shape[1], CAST_COLS):
            w_ref[:, c0:c0 + CAST_COLS] = w32_ref[:, c0:c0 + CAST_COLS].astype(_BF16)
        bst = bsg_ref[...].T
        for j in range(d_sgu // LANES):
            wcat_ref[j] = jnp.concatenate([ws_ref[2 * j], ws_ref[2 * j + 1]], axis=1).astype(_BF16)
            bs_ref[j] = jnp.where(lo, bst[:, 2 * j:2 * j + 1], bst[:, 2 * j + 1:2 * j + 2])

    sub = x_ref.shape[0] // n_sub
    n_qkv = 3 * d_attn
    pair = 2 * SGU_CHUNK
    assert sub % pair == 0

    def normed(t):
        return _rms_norm(x_ref[t * sub:(t + 1) * sub, :], g_ref[...]).astype(_BF16)

    def gating(t, zg):
        zg = jax.nn.gelu(zg)
        u = zg[:, :d_sgu]
        v = zg[:, d_sgu:]
        mu = jnp.mean(v, axis=-1, keepdims=True)
        vc = v - mu
        vn = vc * lax.rsqrt(jnp.mean(vc * vc, axis=-1, keepdims=True) + LN_EPS) * lng_ref[...] + lnb_ref[...]
        for c0 in range(0, sub, pair):
            for j in range(d_sgu // LANES):
                cols = slice(j * LANES, (j + 1) * LANES)
                stacked = []
                for c in (c0, c0 + SGU_CHUNK):
                    slab = vn[c:c + SGU_CHUNK, cols]
                    stacked.append(jnp.concatenate([jnp.where(lo, slab, 0.0), jnp.where(lo, 0.0, slab)], axis=0))
                rhs = jnp.concatenate(stacked, axis=1).astype(_BF16)
                mixed = jnp.dot(wcat_ref[j], rhs, preferred_element_type=_F32)
                for k, c in enumerate((c0, c0 + SGU_CHUNK)):
                    m = mixed[:, k * LANES:(k + 1) * LANES] + bs_ref[j]
                    rows = slice(t * sub + c, t * sub + c + SGU_CHUNK)
                    sgu_ref[rows, cols] = (u[c:c + SGU_CHUNK, cols] * m).astype(_BF16)

    h = normed(0)
    for t in range(n_sub):
        h_next = normed(t + 1) if t + 1 < n_sub else None
        rows = slice(t * sub, (t + 1) * sub)
        zg = jnp.dot(h, w_ref[:, n_qkv:], preferred_element_type=_F32)
        z = jnp.dot(h, w_ref[:, :n_qkv], preferred_element_type=_F32)
        qkv_ref[rows, :d_attn] = z[:, :d_attn] * (HEAD_DIM ** -0.5 * LOG2E)
        qkv_ref[rows, d_attn:] = z[:, d_attn:]
        if t == 0:
            for src_ref, dst_ref in zip(later_f32, later_bf16):
                dst_ref[...] = src_ref[...].astype(_BF16)
        gating(t, zg)
        h = h_next


def _inproj_call(x2d, g_pre_mix, w_in, sgu_ln_g, sgu_ln_b, sgu_w, sgu_b, later_weights, *, d_attn, d_sgu, tm):
    n_tok, d_model = x2d.shape
    n_steps = n_tok // tm
    row_block = lambda w: pl.BlockSpec((w.shape[0] // n_steps, w.shape[1]), lambda i: (i, 0))
    assert all(w.shape[0] % (16 * n_steps) == 0 for w in later_weights)
    assert n_steps >= X_SLOTS
    n_slab = d_sgu // LANES
    n_groups = sgu_w.shape[0]
    assert n_groups == 2 * n_slab
    d_in = w_in.shape[1]
    const = lambda *shape: pl.BlockSpec(shape, lambda i: (0,) * len(shape))
    return pl.pallas_call(
        functools.partial(_inproj_kernel, d_attn=d_attn, d_sgu=d_sgu, n_sub=tm // SUB_ROWS,
                          n_later=len(later_weights)),
        grid=(n_steps,),
        in_specs=[
            pl.BlockSpec(memory_space=pl.ANY),
            const(1, d_model),
            pl.BlockSpec((d_model, d_in), lambda i: (0, 0), pipeline_mode=pl.Buffered(1)),
            const(1, d_sgu),
            const(1, d_sgu),
            const(n_groups, SGU_CHUNK, SGU_CHUNK),
            const(n_groups, SGU_CHUNK),
        ] + [row_block(w) for w in later_weights],
        out_specs=[
            pl.BlockSpec((tm, 3 * d_attn), lambda i: (i, 0)),
            pl.BlockSpec((tm, d_sgu), lambda i: (i, 0)),
        ] + [row_block(w) for w in later_weights],
        out_shape=[
            jax.ShapeDtypeStruct((n_tok, 3 * d_attn), _F32),
            jax.ShapeDtypeStruct((n_tok, d_sgu), _BF16),
        ] + [jax.ShapeDtypeStruct(w.shape, _BF16) for w in later_weights],
        scratch_shapes=[
            pltpu.VMEM((d_model, d_in), _BF16),
            pltpu.VMEM((n_slab, SGU_CHUNK, 2 * SGU_CHUNK), _BF16),
            pltpu.VMEM((n_slab, SGU_CHUNK, LANES), _F32),
            pltpu.VMEM((X_SLOTS, tm, d_model), _F32),
            pltpu.SemaphoreType.DMA((X_SLOTS,)),
        ],
        compiler_params=pltpu.CompilerParams(
            dimension_semantics=("arbitrary",), vmem_limit_bytes=VMEM_LIMIT_BYTES),
        name="inproj_sgu",
    )(x2d, g_pre_mix.reshape(1, -1), w_in, sgu_ln_g.reshape(1, -1),
      sgu_ln_b.reshape(1, -1), sgu_w, sgu_b, *later_weights)


def _t5_bucket(rel):
    half = N_REL_BUCKETS // 2
    max_exact = half // 2
    ret = jnp.where(rel > 0, half, 0)
    n = jnp.abs(rel)
    nf = jnp.maximum(n, 1).astype(jnp.float32)
    large = max_exact + (jnp.log(nf / max_exact) / math.log(REL_MAX_DISTANCE / max_exact)
                         * (half - max_exact)).astype(jnp.int32)
    large = jnp.minimum(large, half - 1)
    return ret + jnp.where(n < max_exact, n, large)


def _bucket_tables():
    rel = jnp.arange(K_TILE) - HALF_WINDOW
    tabs = [jnp.where(jnp.abs(rel) <= HALF_WINDOW, _t5_bucket(rel * dil), -1)
            for _, dil in DILATED_BRANCHES]
    return jnp.stack(tabs)[:, None, :].astype(jnp.int32)


def _attn_kernel(relb_ref, bucket_ref, q_ref, k_ref, v_ref, o_ref,
                 kp_ref, vp_ref, perm_ref, bias_ref, p_ref, ob_ref, mb_ref, lb_ref, *, seq):
    hp = pl.program_id(0)
    n_br = len(DILATED_BRANCHES)
    n_tiles_total = seq // Q_TILE
    assert n_tiles_total % 2 == 0
    d1 = DILATED_BRANCHES[1][1]
    assert [d for _, d in DILATED_BRANCHES] == [1, d1, d1 * d1]
    cls_len = seq // d1
    lane_q = lax.broadcasted_iota(jnp.int32, (Q_TILE, LANES), 1) < HEAD_DIM
    col = lax.broadcasted_iota(jnp.int32, (Q_TILE, K_TILE), 1)

    @pl.when(pl.program_id(1) == 0)
    def _():
        for i in range(n_br):
            bucket = bucket_ref[i]
            for hh in range(2):
                row = jnp.full(bucket.shape, relb_ref[2 * hp + hh, 0] * LOG2E, _F32)
                for b in range(1, N_REL_BUCKETS):
                    row = jnp.where(bucket >= b, relb_ref[2 * hp + hh, b] * LOG2E, row)
                row = jnp.where(bucket < 0, NEG_INF, row)
                tile = pltpu.roll(jnp.broadcast_to(row, (Q_TILE, K_TILE)), 0, 1, stride=1, stride_axis=0)
                rows = slice(hh * Q_TILE, (hh + 1) * Q_TILE)
                bias_ref[i, 0, rows, :] = jnp.where(col < HALF_WINDOW, NEG_INF, tile)
                bias_ref[i, 1, rows, :] = tile
                bias_ref[i, 2, rows, :] = jnp.where(col >= K_TILE - HALF_WINDOW, NEG_INF, tile)

    zero_pad = jnp.zeros((HALF_WINDOW, LANES), _BF16)
    for i in range(n_br):
        for ref in (kp_ref, vp_ref):
            ref[i, :HALF_WINDOW, :] = zero_pad
            ref[i, HALF_WINDOW + seq:, :] = zero_pad
    for c0 in range(0, seq, COPY_ROWS):
        kp_ref[0, pl.ds(HALF_WINDOW + c0, COPY_ROWS), :] = k_ref[pl.ds(c0, COPY_ROWS), :].astype(_BF16)
        vp_ref[0, pl.ds(HALF_WINDOW + c0, COPY_ROWS), :] = v_ref[pl.ds(c0, COPY_ROWS), :].astype(_BF16)
    for r in range(d1):
        for c0 in range(0, cls_len, COPY_ROWS):
            src = pl.ds(r + d1 * c0, COPY_ROWS, stride=d1)
            dst = pl.ds(r * cls_len + c0, COPY_ROWS)
            padded = pl.ds(HALF_WINDOW + r * cls_len + c0, COPY_ROWS)
            perm_ref[0, dst, :] = q_ref[src, :]
            for a, (x_ref, xp_ref) in enumerate(((k_ref, kp_ref), (v_ref, vp_ref)), start=1):
                x = x_ref[src, :]
                perm_ref[a, dst, :] = x
                xp_ref[1, padded, :] = x.astype(_BF16)
    sub_len = cls_len // d1
    for r2 in range(d1 * d1):
        src = pl.ds((r2 % d1) * cls_len + r2 // d1, sub_len, stride=d1)
        padded = pl.ds(HALF_WINDOW + r2 * sub_len, sub_len)
        kp_ref[2, padded, :] = perm_ref[1, src, :].astype(_BF16)
        vp_ref[2, padded, :] = perm_ref[2, src, :].astype(_BF16)

    def tile_geometry(i, j):
        n_tiles = seq // (DILATED_BRANCHES[i][1] * Q_TILE)
        n = j % n_tiles
        variant = 0 if n == 0 else (2 if n == n_tiles - 1 else 1)
        if i < 2:
            rows = pl.ds(j * Q_TILE, Q_TILE)
        else:
            r2 = j // n_tiles
            rows = pl.ds((r2 % d1) * cls_len + r2 // d1 + d1 * Q_TILE * n, Q_TILE, stride=d1)
        return rows, pl.ds(j * Q_TILE, K_TILE), variant

    def stage_scores(i, j, slot):
        rows, win, variant = tile_geometry(i, j)
        q = q_ref[rows, :] if i == 0 else perm_ref[0, rows, :]
        q2 = jnp.concatenate([jnp.where(lane_q, q, 0.0), jnp.where(lane_q, 0.0, q)], axis=0)
        s = lax.dot_general(q2.astype(_BF16), kp_ref[i, win, :], (((1,), (1,)), ((), ())),
                            preferred_element_type=_F32)
        s = s + bias_ref[i, variant]
        m2 = jnp.max(s, axis=-1, keepdims=True)
        p_ref[slot] = jnp.exp2(s - m2).astype(_BF16)
        mb_ref[i, rows, :] = jnp.where(lane_q, m2[:Q_TILE], m2[Q_TILE:])

    ones = jnp.ones((K_TILE, LANES), _BF16)

    def stage_values(i, j, slot):
        rows, win, _ = tile_geometry(i, j)
        v1 = jnp.concatenate([vp_ref[i, win, :], ones], axis=1)
        pv = jnp.dot(p_ref[slot], v1, preferred_element_type=_F32)
        ob_ref[i, rows, :] = jnp.where(lane_q, pv[:Q_TILE, :LANES], pv[Q_TILE:, :LANES])
        lb_ref[i, rows, :] = jnp.where(lane_q, pv[:Q_TILE, LANES:], pv[Q_TILE:, LANES:])

    stage_scores(0, 0, 0)
    for i in range(n_br):
        for j in range(n_tiles_total):
            stage_values(i, j, j % 2)
            if j + 1 < n_tiles_total:
                stage_scores(i, j + 1, (j + 1) % 2)
            elif i + 1 < n_br:
                stage_scores(i + 1, 0, 0)

    blocks_per_class = cls_len // MERGE_ROWS

    def merge_body(t, carry):
        cls_rows = pl.ds(pl.multiple_of(t * MERGE_ROWS, MERGE_ROWS), MERGE_ROWS)
        nat_rows = pl.ds(t // blocks_per_class + d1 * MERGE_ROWS * (t % blocks_per_class), MERGE_ROWS, stride=d1)
        rows = [nat_rows] + [cls_rows] * (n_br - 1)
        ms = [mb_ref[i, rows[i], :] for i in range(n_br)]
        m = functools.reduce(jnp.maximum, ms)
        ws = [jnp.exp2(m_i - m) for m_i in ms]
        num = functools.reduce(jnp.add, [w * ob_ref[i, rows[i], :] for i, w in enumerate(ws)])
        den = functools.reduce(jnp.add, [w * lb_ref[i, rows[i], :] for i, w in enumerate(ws)])
        perm_ref[1, nat_rows, :] = num / den
        return carry

    lax.fori_loop(0, seq // MERGE_ROWS, merge_body, 0, unroll=4)
    for c0 in range(0, seq, COPY_ROWS):
        o_ref[pl.ds(c0, COPY_ROWS), :] = perm_ref[1, pl.ds(c0, COPY_ROWS), :].astype(o_ref.dtype)


def _attn_call(qkv, rel_bias, *, batch, seq, d_attn):
    n_pairs = d_attn // LANES
    qkv3 = qkv.reshape(batch, seq, 3 * d_attn)
    slab = lambda off: pl.BlockSpec((None, seq, LANES), lambda hp, b: (b, 0, off + hp))
    n_br = len(DILATED_BRANCHES)
    return pl.pallas_call(
        functools.partial(_attn_kernel, seq=seq),
        grid=(n_pairs, batch),
        in_specs=[
            pl.BlockSpec(memory_space=pltpu.SMEM),
            pl.BlockSpec((n_br, 1, K_TILE), lambda hp, b: (0, 0, 0)),
            slab(0), slab(n_pairs), slab(2 * n_pairs),
        ],
        out_specs=pl.BlockSpec((None, seq, LANES), lambda hp, b: (b, 0, hp)),
        out_shape=jax.ShapeDtypeStruct((batch, seq, d_attn), _BF16),
        scratch_shapes=[
            pltpu.VMEM((n_br, seq + 2 * HALF_WINDOW, LANES), _BF16),
            pltpu.VMEM((n_br, seq + 2 * HALF_WINDOW, LANES), _BF16),
            pltpu.VMEM((3, seq, LANES), _F32),
            pltpu.VMEM((n_br, 3, 2 * Q_TILE, K_TILE), _F32),
            pltpu.VMEM((2, 2 * Q_TILE, K_TILE), _BF16),
            pltpu.VMEM((n_br, seq, LANES), _F32),
            pltpu.VMEM((n_br, seq, LANES), _F32),
            pltpu.VMEM((n_br, seq, LANES), _F32),
        ],
        compiler_params=pltpu.CompilerParams(
            dimension_semantics=("arbitrary", "arbitrary"), vmem_limit_bytes=VMEM_LIMIT_BYTES),
        name="dilated_attn",
    )(rel_bias.T, _bucket_tables(), qkv3, qkv3, qkv3)


def _outproj_ffn_kernel(attn_ref, sgu_ref, x_ref, wo_ref, g1_ref, g2_ref, w1_ref, w2_ref, g3_ref,
                        o_ref, *, d_attn, ff_chunk, n_sub):
    sub = x_ref.shape[0] // n_sub
    rows = [slice(t * sub, (t + 1) * sub) for t in range(n_sub)]
    chunks = range(0, w1_ref.shape[1], ff_chunk)

    def out_proj(t):
        return (jnp.dot(attn_ref[rows[t], :], wo_ref[:d_attn, :], preferred_element_type=_F32)
                + jnp.dot(sgu_ref[rows[t], :], wo_ref[d_attn:, :], preferred_element_type=_F32))

    def mid_norms(t, mix):
        x1 = x_ref[rows[t], :] + _rms_norm(mix, g1_ref[...])
        return x1, _rms_norm(x1, g2_ref[...]).astype(_BF16)

    def ffn_chunk(h, f, c0):
        a = jnp.maximum(jnp.dot(h, w1_ref[:, c0:c0 + ff_chunk], preferred_element_type=_F32), 0.0)
        part = jnp.dot((a * a).astype(_BF16), w2_ref[c0:c0 + ff_chunk, :], preferred_element_type=_F32)
        return part if f is None else f + part

    mix = out_proj(0)
    prev = None
    for t in range(n_sub):
        nxt_mix = out_proj(t + 1) if t + 1 < n_sub else None
        x1, h = mid_norms(t, mix)
        f = None
        for ci, c0 in enumerate(chunks):
            f = ffn_chunk(h, f, c0)
            if ci == 0 and prev is not None:
                pt, px1, pf = prev
                o_ref[rows[pt], :] = px1 + _rms_norm(pf, g3_ref[...])
        prev = (t, x1, f)
        mix = nxt_mix
    pt, px1, pf = prev
    o_ref[rows[pt], :] = px1 + _rms_norm(pf, g3_ref[...])


def _outproj_ffn_call(attn2d, sgu2d, x2d, w_out, g_post_mix, g_pre_ffn, w_ff1, w_ff2, g_post_ffn, *, tm):
    n_tok, d_model = x2d.shape
    d_attn = attn2d.shape[1]
    d_sgu = sgu2d.shape[1]
    d_ff = w_ff1.shape[1]
    const = lambda *shape: pl.BlockSpec(shape, lambda i: (0,) * len(shape), pipeline_mode=pl.Buffered(1))
    return pl.pallas_call(
        functools.partial(_outproj_ffn_kernel, d_attn=d_attn, ff_chunk=FF_CHUNK, n_sub=tm // SUB_ROWS),
        grid=(n_tok // tm,),
        in_specs=[
            pl.BlockSpec((tm, d_attn), lambda i: (i, 0)),
            pl.BlockSpec((tm, d_sgu), lambda i: (i, 0)),
            pl.BlockSpec((tm, d_model), lambda i: (i, 0)),
            const(d_attn + d_sgu, d_model),
            const(1, d_model),
            const(1, d_model),
            const(d_model, d_ff),
            const(d_ff, d_model),
            const(1, d_model),
        ],
        out_specs=pl.BlockSpec((tm, d_model), lambda i: (i, 0)),
        out_shape=jax.ShapeDtypeStruct((n_tok, d_model), _F32),
        compiler_params=pltpu.CompilerParams(
            dimension_semantics=("arbitrary",), vmem_limit_bytes=VMEM_LIMIT_BYTES),
        name="outproj_ffn",
    )(attn2d, sgu2d, x2d, w_out, g_post_mix.reshape(1, -1), g_pre_ffn.reshape(1, -1),
      w_ff1, w_ff2, g_post_ffn.reshape(1, -1))


def kernel(x, g_pre_mix, w_in, sgu_ln_g, sgu_ln_b, sgu_w, sgu_b, w_out, g_post_mix, g_pre_ffn,
           w_ff1, w_ff2, g_post_ffn, rel_bias):
    batch, seq, d_model = x.shape
    d_sgu = sgu_ln_g.shape[-1]
    d_attn = w_out.shape[1] - d_sgu
    assert d_attn % LANES == 0 and d_sgu % LANES == 0 and sgu_w.shape[-1] == SGU_CHUNK
    assert all(seq % (dil * Q_TILE) == 0 and win == 2 * HALF_WINDOW * dil for win, dil in DILATED_BRANCHES)
    assert (batch * seq) % TOKEN_TILE == 0 and w_ff1.shape[-1] % FF_CHUNK == 0
    x2d = x.reshape(batch * seq, d_model)
    for layer in range(g_pre_mix.shape[0]):
        qkv, sgu, w_out_bf, w_ff1_bf, w_ff2_bf = _inproj_call(
            x2d, g_pre_mix[layer], w_in[layer], sgu_ln_g[layer], sgu_ln_b[layer], sgu_w[layer], sgu_b[layer],
            (w_out[layer], w_ff1[layer], w_ff2[layer]), d_attn=d_attn, d_sgu=d_sgu, tm=TOKEN_TILE)
        attn = _attn_call(qkv, rel_bias, batch=batch, seq=seq, d_attn=d_attn)
        x2d = _outproj_ffn_call(attn.reshape(batch * seq, d_attn), sgu, x2d, w_out_bf,
                                g_post_mix[layer], g_pre_ffn[layer], w_ff1_bf, w_ff2_bf,
                                g_post_ffn[layer], tm=TOKEN_TILE)
    return x2d.reshape(batch, seq, d_model)
```

```python
import functools
import math

import jax
import jax.numpy as jnp
from jax import lax
from jax.experimental import pallas as pl
from jax.experimental.pallas import tpu as pltpu

HEAD_DIM = 64
DILATED_BRANCHES = ((128, 1), (512, 4), (2048, 16))
SGU_CHUNK = 128
N_REL_BUCKETS = 32
REL_MAX_DISTANCE = 1024
RMS_EPS = 1e-6
LN_EPS = 1e-5
NEG_INF = -1e30

LANES = 128
HALF_WINDOW = 64
Q_TILE = 128
K_TILE = Q_TILE + 2 * HALF_WINDOW
MERGE_ROWS = 256
COPY_ROWS = 512
CAST_COLS = 512
TOKEN_TILE = 1024
SUB_ROWS = 256
FF_CHUNK = 1024
LOG2E = math.log2(math.e)
V7X_VMEM_BYTES = 64 * 1024 * 1024
VMEM_LIMIT_BYTES = V7X_VMEM_BYTES - 8 * 1024 * 1024

_F32 = jnp.float32
_BF16 = jnp.bfloat16


def _rms_norm(x, g):
    return x * lax.rsqrt(jnp.mean(x * x, axis=-1, keepdims=True) + RMS_EPS) * g


def _inproj_kernel(x_ref, g_ref, w32_ref, lng_ref, lnb_ref, ws_ref, bsg_ref, *rest,
                   d_attn, d_sgu, n_sub, n_later):
    later_f32, (qkv_ref, sgu_ref), later_bf16 = rest[:n_later], rest[n_later:n_later + 2], rest[n_later + 2:-3]
    w_ref, wcat_ref, bs_ref = rest[-3:]
    lane = lax.broadcasted_iota(jnp.int32, (SGU_CHUNK, LANES), 1)
    lo = lane < HEAD_DIM

    @pl.when(pl.program_id(0) == 0)
    def _():
        for c0 in range(0, w_ref.shape[1], CAST_COLS):
            w_ref[:, c0:c0 + CAST_COLS] = w32_ref[:, c0:c0 + CAST_COLS].astype(_BF16)
        bst = bsg_ref[...].T
        for j in range(d_sgu // LANES):
            wcat_ref[j] = jnp.concatenate([ws_ref[2 * j], ws_ref[2 * j + 1]], axis=1).astype(_BF16)
            bs_ref[j] = jnp.where(lo, bst[:, 2 * j:2 * j + 1], bst[:, 2 * j + 1:2 * j + 2])

    sub = x_ref.shape[0] // n_sub
    n_qkv = 3 * d_attn
    pair = 2 * SGU_CHUNK
    assert sub % pair == 0

    def normed(t):
        return _rms_norm(x_ref[t * sub:(t + 1) * sub, :], g_ref[...]).astype(_BF16)

    def gating(t, zg):
        zg = jax.nn.gelu(zg)
        u = zg[:, :d_sgu]
        v = zg[:, d_sgu:]
        mu = jnp.mean(v, axis=-1, keepdims=True)
        vc = v - mu
        vn = vc * lax.rsqrt(jnp.mean(vc * vc, axis=-1, keepdims=True) + LN_EPS) * lng_ref[...] + lnb_ref[...]
        for c0 in range(0, sub, pair):
            for j in range(d_sgu // LANES):
                cols = slice(j * LANES, (j + 1) * LANES)
                stacked = []
                for c in (c0, c0 + SGU_CHUNK):
                    slab = vn[c:c + SGU_CHUNK, cols]
                    stacked.append(jnp.concatenate([jnp.where(lo, slab, 0.0), jnp.where(lo, 0.0, slab)], axis=0))
                rhs = jnp.concatenate(stacked, axis=1).astype(_BF16)
                mixed = jnp.dot(wcat_ref[j], rhs, preferred_element_type=_F32)
                for k, c in enumerate((c0, c0 + SGU_CHUNK)):
                    m = mixed[:, k * LANES:(k + 1) * LANES] + bs_ref[j]
                    rows = slice(t * sub + c, t * sub + c + SGU_CHUNK)
                    sgu_ref[rows, cols] = (u[c:c + SGU_CHUNK, cols] * m).astype(_BF16)

    h = normed(0)
    for t in range(n_sub):
        h_next = normed(t + 1) if t + 1 < n_sub else None
        rows = slice(t * sub, (t + 1) * sub)
        zg = jnp.dot(h, w_ref[:, n_qkv:], preferred_element_type=_F32)
        z = jnp.dot(h, w_ref[:, :n_qkv], preferred_element_type=_F32)
        qkv_ref[rows, :d_attn] = z[:, :d_attn] * (HEAD_DIM ** -0.5 * LOG2E)
        qkv_ref[rows, d_attn:] = z[:, d_attn:]
        if t == 0:
            for src_ref, dst_ref in zip(later_f32, later_bf16):
                dst_ref[...] = src_ref[...].astype(_BF16)
        gating(t, zg)
        h = h_next


def _inproj_call(x2d, g_pre_mix, w_in, sgu_ln_g, sgu_ln_b, sgu_w, sgu_b, later_weights, *, d_attn, d_sgu, tm):
    n_tok, d_model = x2d.shape
    n_steps = n_tok // tm
    row_block = lambda w: pl.BlockSpec((w.shape[0] // n_steps, w.shape[1]), lambda i: (i, 0))
    assert all(w.shape[0] % (16 * n_steps) == 0 for w in later_weights)
    n_slab = d_sgu // LANES
    n_groups = sgu_w.shape[0]
    assert n_groups == 2 * n_slab
    d_in = w_in.shape[1]
    const = lambda *shape: pl.BlockSpec(shape, lambda i: (0,) * len(shape))
    return pl.pallas_call(
        functools.partial(_inproj_kernel, d_attn=d_attn, d_sgu=d_sgu, n_sub=tm // SUB_ROWS,
                          n_later=len(later_weights)),
        grid=(n_steps,),
        in_specs=[
            pl.BlockSpec((tm, d_model), lambda i: (i, 0)),
            const(1, d_model),
            pl.BlockSpec((d_model, d_in), lambda i: (0, 0), pipeline_mode=pl.Buffered(1)),
            const(1, d_sgu),
            const(1, d_sgu),
            const(n_groups, SGU_CHUNK, SGU_CHUNK),
            const(n_groups, SGU_CHUNK),
        ] + [row_block(w) for w in later_weights],
        out_specs=[
            pl.BlockSpec((tm, 3 * d_attn), lambda i: (i, 0)),
            pl.BlockSpec((tm, d_sgu), lambda i: (i, 0)),
        ] + [row_block(w) for w in later_weights],
        out_shape=[
            jax.ShapeDtypeStruct((n_tok, 3 * d_attn), _F32),
            jax.ShapeDtypeStruct((n_tok, d_sgu), _BF16),
        ] + [jax.ShapeDtypeStruct(w.shape, _BF16) for w in later_weights],
        scratch_shapes=[
            pltpu.VMEM((d_model, d_in), _BF16),
            pltpu.VMEM((n_slab, SGU_CHUNK, 2 * SGU_CHUNK), _BF16),
            pltpu.VMEM((n_slab, SGU_CHUNK, LANES), _F32),
        ],
        compiler_params=pltpu.CompilerParams(
            dimension_semantics=("arbitrary",), vmem_limit_bytes=VMEM_LIMIT_BYTES),
        name="inproj_sgu",
    )(x2d, g_pre_mix.reshape(1, -1), w_in, sgu_ln_g.reshape(1, -1),
      sgu_ln_b.reshape(1, -1), sgu_w, sgu_b, *later_weights)


def _t5_bucket(rel):
    half = N_REL_BUCKETS // 2
    max_exact = half // 2
    ret = jnp.where(rel > 0, half, 0)
    n = jnp.abs(rel)
    nf = jnp.maximum(n, 1).astype(jnp.float32)
    large = max_exact + (jnp.log(nf / max_exact) / math.log(REL_MAX_DISTANCE / max_exact)
                         * (half - max_exact)).astype(jnp.int32)
    large = jnp.minimum(large, half - 1)
    return ret + jnp.where(n < max_exact, n, large)


def _bucket_tables():
    rel = jnp.arange(K_TILE) - HALF_WINDOW
    tabs = [jnp.where(jnp.abs(rel) <= HALF_WINDOW, _t5_bucket(rel * dil), -1)
            for _, dil in DILATED_BRANCHES]
    return jnp.stack(tabs)[:, None, :].astype(jnp.int32)


def _attn_kernel(relb_ref, bucket_ref, q_ref, k_ref, v_ref, o_ref,
                 kp_ref, vp_ref, perm_ref, bias_ref, p_ref, ob_ref, mb_ref, lb_ref, *, seq):
    hp = pl.program_id(0)
    n_br = len(DILATED_BRANCHES)
    n_tiles_total = seq // Q_TILE
    assert n_tiles_total % 2 == 0
    d1 = DILATED_BRANCHES[1][1]
    assert [d for _, d in DILATED_BRANCHES] == [1, d1, d1 * d1]
    cls_len = seq // d1
    lane_q = lax.broadcasted_iota(jnp.int32, (Q_TILE, LANES), 1) < HEAD_DIM
    col = lax.broadcasted_iota(jnp.int32, (Q_TILE, K_TILE), 1)

    @pl.when(pl.program_id(1) == 0)
    def _():
        for i in range(n_br):
            bucket = bucket_ref[i]
            for hh in range(2):
                row = jnp.full(bucket.shape, relb_ref[2 * hp + hh, 0] * LOG2E, _F32)
                for b in range(1, N_REL_BUCKETS):
                    row = jnp.where(bucket >= b, relb_ref[2 * hp + hh, b] * LOG2E, row)
                row = jnp.where(bucket < 0, NEG_INF, row)
                tile = pltpu.roll(jnp.broadcast_to(row, (Q_TILE, K_TILE)), 0, 1, stride=1, stride_axis=0)
                rows = slice(hh * Q_TILE, (hh + 1) * Q_TILE)
                bias_ref[i, 0, rows, :] = jnp.where(col < HALF_WINDOW, NEG_INF, tile)
                bias_ref[i, 1, rows, :] = tile
                bias_ref[i, 2, rows, :] = jnp.where(col >= K_TILE - HALF_WINDOW, NEG_INF, tile)

    zero_pad = jnp.zeros((HALF_WINDOW, LANES), _BF16)
    for i in range(n_br):
        for ref in (kp_ref, vp_ref):
            ref[i, :HALF_WINDOW, :] = zero_pad
            ref[i, HALF_WINDOW + seq:, :] = zero_pad
    for c0 in range(0, seq, COPY_ROWS):
        kp_ref[0, pl.ds(HALF_WINDOW + c0, COPY_ROWS), :] = k_ref[pl.ds(c0, COPY_ROWS), :].astype(_BF16)
        vp_ref[0, pl.ds(HALF_WINDOW + c0, COPY_ROWS), :] = v_ref[pl.ds(c0, COPY_ROWS), :].astype(_BF16)
    for r in range(d1):
        for c0 in range(0, cls_len, COPY_ROWS):
            src = pl.ds(r + d1 * c0, COPY_ROWS, stride=d1)
            dst = pl.ds(r * cls_len + c0, COPY_ROWS)
            padded = pl.ds(HALF_WINDOW + r * cls_len + c0, COPY_ROWS)
            perm_ref[0, dst, :] = q_ref[src, :]
            for a, (x_ref, xp_ref) in enumerate(((k_ref, kp_ref), (v_ref, vp_ref)), start=1):
                x = x_ref[src, :]
                perm_ref[a, dst, :] = x
                xp_ref[1, padded, :] = x.astype(_BF16)
    sub_len = cls_len // d1
    for r2 in range(d1 * d1):
        src = pl.ds((r2 % d1) * cls_len + r2 // d1, sub_len, stride=d1)
        padded = pl.ds(HALF_WINDOW + r2 * sub_len, sub_len)
        kp_ref[2, padded, :] = perm_ref[1, src, :].astype(_BF16)
        vp_ref[2, padded, :] = perm_ref[2, src, :].astype(_BF16)

    def tile_geometry(i, j):
        n_tiles = seq // (DILATED_BRANCHES[i][1] * Q_TILE)
        n = j % n_tiles
        variant = 0 if n == 0 else (2 if n == n_tiles - 1 else 1)
        if i < 2:
            rows = pl.ds(j * Q_TILE, Q_TILE)
        else:
            r2 = j // n_tiles
            rows = pl.ds((r2 % d1) * cls_len + r2 // d1 + d1 * Q_TILE * n, Q_TILE, stride=d1)
        return rows, pl.ds(j * Q_TILE, K_TILE), variant

    def stage_scores(i, j, slot):
        rows, win, variant = tile_geometry(i, j)
        q = q_ref[rows, :] if i == 0 else perm_ref[0, rows, :]
        q2 = jnp.concatenate([jnp.where(lane_q, q, 0.0), jnp.where(lane_q, 0.0, q)], axis=0)
        s = lax.dot_general(q2.astype(_BF16), kp_ref[i, win, :], (((1,), (1,)), ((), ())),
                            preferred_element_type=_F32)
        s = s + bias_ref[i, variant]
        m2 = jnp.max(s, axis=-1, keepdims=True)
        p_ref[slot] = jnp.exp2(s - m2).astype(_BF16)
        mb_ref[i, rows, :] = jnp.where(lane_q, m2[:Q_TILE], m2[Q_TILE:])

    ones = jnp.ones((K_TILE, LANES), _BF16)

    def stage_values(i, j, slot):
        rows, win, _ = tile_geometry(i, j)
        v1 = jnp.concatenate([vp_ref[i, win, :], ones], axis=1)
        pv = jnp.dot(p_ref[slot], v1, preferred_element_type=_F32)
        ob_ref[i, rows, :] = jnp.where(lane_q, pv[:Q_TILE, :LANES], pv[Q_TILE:, :LANES])
        lb_ref[i, rows, :] = jnp.where(lane_q, pv[:Q_TILE, LANES:], pv[Q_TILE:, LANES:])

    stage_scores(0, 0, 0)
    for i in range(n_br):
        for j in range(n_tiles_total):
            stage_values(i, j, j % 2)
            if j + 1 < n_tiles_total:
                stage_scores(i, j + 1, (j + 1) % 2)
            elif i + 1 < n_br:
                stage_scores(i + 1, 0, 0)

    blocks_per_class = cls_len // MERGE_ROWS

    def merge_body(t, carry):
        cls_rows = pl.ds(pl.multiple_of(t * MERGE_ROWS, MERGE_ROWS), MERGE_ROWS)
        nat_rows = pl.ds(t // blocks_per_class + d1 * MERGE_ROWS * (t % blocks_per_class), MERGE_ROWS, stride=d1)
        rows = [nat_rows] + [cls_rows] * (n_br - 1)
        ms = [mb_ref[i, rows[i], :] for i in range(n_br)]
        m = functools.reduce(jnp.maximum, ms)
        ws = [jnp.exp2(m_i - m) for m_i in ms]
        num = functools.reduce(jnp.add, [w * ob_ref[i, rows[i], :] for i, w in enumerate(ws)])
        den = functools.reduce(jnp.add, [w * lb_ref[i, rows[i], :] for i, w in enumerate(ws)])
        perm_ref[1, nat_rows, :] = num / den
        return carry

    lax.fori_loop(0, seq // MERGE_ROWS, merge_body, 0, unroll=4)
    for c0 in range(0, seq, COPY_ROWS):
        o_ref[pl.ds(c0, COPY_ROWS), :] = perm_ref[1, pl.ds(c0, COPY_ROWS), :].astype(o_ref.dtype)


def _attn_call(qkv, rel_bias, *, batch, seq, d_attn):
    n_pairs = d_attn // LANES
    qkv3 = qkv.reshape(batch, seq, 3 * d_attn)
    slab = lambda off: pl.BlockSpec((None, seq, LANES), lambda hp, b: (b, 0, off + hp))
    n_br = len(DILATED_BRANCHES)
    return pl.pallas_call(
        functools.partial(_attn_kernel, seq=seq),
        grid=(n_pairs, batch),
        in_specs=[
            pl.BlockSpec(memory_space=pltpu.SMEM),
            pl.BlockSpec((n_br, 1, K_TILE), lambda hp, b: (0, 0, 0)),
            slab(0), slab(n_pairs), slab(2 * n_pairs),
        ],
        out_specs=pl.BlockSpec((None, seq, LANES), lambda hp, b: (b, 0, hp)),
        out_shape=jax.ShapeDtypeStruct((batch, seq, d_attn), _BF16),
        scratch_shapes=[
            pltpu.VMEM((n_br, seq + 2 * HALF_WINDOW, LANES), _BF16),
            pltpu.VMEM((n_br, seq + 2 * HALF_WINDOW, LANES), _BF16),
            pltpu.VMEM((3, seq, LANES), _F32),
            pltpu.VMEM((n_br, 3, 2 * Q_TILE, K_TILE), _F32),
            pltpu.VMEM((2, 2 * Q_TILE, K_TILE), _BF16),
            pltpu.VMEM((n_br, seq, LANES), _F32),
            pltpu.VMEM((n_br, seq, LANES), _F32),
            pltpu.VMEM((n_br, seq, LANES), _F32),
        ],
        compiler_params=pltpu.CompilerParams(
            dimension_semantics=("arbitrary", "arbitrary"), vmem_limit_bytes=VMEM_LIMIT_BYTES),
        name="dilated_attn",
    )(rel_bias.T, _bucket_tables(), qkv3, qkv3, qkv3)


def _outproj_ffn_kernel(attn_ref, sgu_ref, x_ref, wo_ref, g1_ref, g2_ref, w1_ref, w2_ref, g3_ref,
                        o_ref, *, d_attn, ff_chunk, n_sub):
    sub = x_ref.shape[0] // n_sub
    rows = [slice(t * sub, (t + 1) * sub) for t in range(n_sub)]
    chunks = range(0, w1_ref.shape[1], ff_chunk)

    def out_proj(t):
        mixed = jnp.concatenate([attn_ref[rows[t], :], sgu_ref[rows[t], :]], axis=1)
        return jnp.dot(mixed, wo_ref[...], preferred_element_type=_F32)

    def mid_norms(t, mix):
        x1 = x_ref[rows[t], :] + _rms_norm(mix, g1_ref[...])
        return x1, _rms_norm(x1, g2_ref[...]).astype(_BF16)

    def ffn_chunk(h, f, c0):
        a = jnp.maximum(jnp.dot(h, w1_ref[:, c0:c0 + ff_chunk], preferred_element_type=_F32), 0.0)
        part = jnp.dot((a * a).astype(_BF16), w2_ref[c0:c0 + ff_chunk, :], preferred_element_type=_F32)
        return part if f is None else f + part

    mix = out_proj(0)
    prev = None
    for t in range(n_sub):
        nxt_mix = out_proj(t + 1) if t + 1 < n_sub else None
        x1, h = mid_norms(t, mix)
        f = None
        for ci, c0 in enumerate(chunks):
            f = ffn_chunk(h, f, c0)
            if ci == 0 and prev is not None:
                pt, px1, pf = prev
                o_ref[rows[pt], :] = px1 + _rms_norm(pf, g3_ref[...])
        prev = (t, x1, f)
        mix = nxt_mix
    pt, px1, pf = prev
    o_ref[rows[pt], :] = px1 + _rms_norm(pf, g3_ref[...])


def _outproj_ffn_call(attn2d, sgu2d, x2d, w_out, g_post_mix, g_pre_ffn, w_ff1, w_ff2, g_post_ffn, *, tm):
    n_tok, d_model = x2d.shape
    d_attn = attn2d.shape[1]
    d_sgu = sgu2d.shape[1]
    d_ff = w_ff1.shape[1]
    const = lambda *shape: pl.BlockSpec(shape, lambda i: (0,) * len(shape), pipeline_mode=pl.Buffered(1))
    return pl.pallas_call(
        functools.partial(_outproj_ffn_kernel, d_attn=d_attn, ff_chunk=FF_CHUNK, n_sub=tm // SUB_ROWS),
        grid=(n_tok // tm,),
        in_specs=[
            pl.BlockSpec((tm, d_attn), lambda i: (i, 0)),
            pl.BlockSpec((tm, d_sgu), lambda i: (i, 0)),
            pl.BlockSpec((tm, d_model), lambda i: (i, 0)),
            const(d_attn + d_sgu, d_model),
            const(1, d_model),
            const(1, d_model),
            const(d_model, d_ff),
            const(d_ff, d_model),
            const(1, d_model),
        ],
        out_specs=pl.BlockSpec((tm, d_model), lambda i: (i, 0)),
        out_shape=jax.ShapeDtypeStruct((n_tok, d_model), _F32),
        compiler_params=pltpu.CompilerParams(
            dimension_semantics=("arbitrary",), vmem_limit_bytes=VMEM_LIMIT_BYTES),
        name="outproj_ffn",
    )(attn2d, sgu2d, x2d, w_out, g_post_mix.reshape(1, -1), g_pre_ffn.reshape(1, -1),
      w_ff1, w_ff2, g_post_ffn.reshape(1, -1))


def kernel(x, g_pre_mix, w_in, sgu_ln_g, sgu_ln_b, sgu_w, sgu_b, w_out, g_post_mix, g_pre_ffn,
           w_ff1, w_ff2, g_post_ffn, rel_bias):
    batch, seq, d_model = x.shape
    d_sgu = sgu_ln_g.shape[-1]
    d_attn = w_out.shape[1] - d_sgu
    assert d_attn % LANES == 0 and d_sgu % LANES == 0 and sgu_w.shape[-1] == SGU_CHUNK
    assert all(seq % (dil * Q_TILE) == 0 and win == 2 * HALF_WINDOW * dil for win, dil in DILATED_BRANCHES)
    assert (batch * seq) % TOKEN_TILE == 0 and w_ff1.shape[-1] % FF_CHUNK == 0
    x2d = x.reshape(batch * seq, d_model)
    for layer in range(g_pre_mix.shape[0]):
        qkv, sgu, w_out_bf, w_ff1_bf, w_ff2_bf = _inproj_call(
            x2d, g_pre_mix[layer], w_in[layer], sgu_ln_g[layer], sgu_ln_b[layer], sgu_w[layer], sgu_b[layer],
            (w_out[layer], w_ff1[layer], w_ff2[layer]), d_attn=d_attn, d_sgu=d_sgu, tm=TOKEN_TILE)
        attn = _attn_call(qkv, rel_bias, batch=batch, seq=seq, d_attn=d_attn)
        x2d = _outproj_ffn_call(attn.reshape(batch * seq, d_attn), sgu, x2d, w_out_bf,
                                g_post_mix[layer], g_pre_ffn[layer], w_ff1_bf, w_ff2_bf,
                                g_post_ffn[layer], tm=TOKEN_TILE)
    return x2d.reshape(batch, seq, d_model)
```
